```python
import jax, jax.numpy as jnp
from jax import lax
import numpy as np

D_MODEL = 1024
BATCH = 32
SEQ = 2048
DEPTH = 1

CTX_LEN = 256
GRID_W = 64
N_MOD = 6
EPS = 1e-6
A_HEADS = 8
A_HEAD_DIM = 128
A_WIDTH = A_HEADS * A_HEAD_DIM
A_CHUNK = 64
B_GROUPS = 8
B_GROUP_DIM = 128
B_WIDTH = B_GROUPS * B_GROUP_DIM
B_CHUNK = 128
B_ROWS_PER_CHUNK = B_CHUNK // GRID_W
N_EXPERTS = 32
TOP_K = 4
D_EXPERT = 1024
SWIGLU_LIMIT = 7.0
SWIGLU_ALPHA = 1.702
MOE_BLOCK = 256
SPLITS = (A_WIDTH, A_WIDTH, A_WIDTH, A_WIDTH, A_WIDTH, B_WIDTH, B_WIDTH, D_MODEL, D_MODEL)
IN_WIDTH = 5 * A_WIDTH + 2 * B_WIDTH + 2 * D_MODEL

kernel_name = "hybrid_hgrn2_chunkmlp_moe_dit_block"


def rms_norm(x, g):
    xf = x.astype(jnp.float32)
    y = xf * lax.rsqrt(jnp.mean(xf * xf, axis=-1, keepdims=True) + EPS)
    return (y * g).astype(x.dtype)


def modulate(x, g, shift, scale):
    return rms_norm(x, g) * (1.0 + scale) + shift


def project(h, w_in, lb_f, lb_b):
    bsz, t, _ = h.shape
    z = h @ w_in
    q, i, f_fwd, f_bwd, og, u, v, gate_a, gate_b = jnp.split(
        z, np.cumsum(SPLITS)[:-1].tolist(), axis=-1)

    def heads(a):
        return a.astype(jnp.float32).reshape(bsz, t, A_HEADS, A_HEAD_DIM)

    def decay(f_logit, lb):
        f = lb + (1.0 - lb) * jax.nn.sigmoid(f_logit.astype(jnp.float32))
        return heads(1.0 - f), heads(jnp.log(f))

    k_f, logf_f = decay(f_fwd, lb_f)
    k_b, logf_b = decay(f_bwd, lb_b)
    return (heads(jax.nn.silu(q)), heads(i), k_f, logf_f, k_b, logf_b, og,
            jax.nn.gelu(u, approximate=False), jax.nn.gelu(v, approximate=False),
            gate_a, gate_b)


def gla_chunked(q, k, v, logf, s0):
    bsz, t, nh, dk = q.shape
    dv = v.shape[-1]
    n = t // A_CHUNK

    def to_chunks(a):
        return jnp.swapaxes(a.reshape(bsz, n, A_CHUNK, nh, a.shape[-1]), 0, 1)

    lower = jnp.tril(jnp.ones((A_CHUNK, A_CHUNK), dtype=bool))

    def step(s, xs):
        qc, kc, vc, gc = xs
        b = jnp.cumsum(gc, axis=1)
        b_ref = b[:, A_CHUNK // 2 - 1:A_CHUNK // 2]
        b_end = b[:, -1]
        scores = jnp.einsum('bthd,bshd->bhts', qc * jnp.exp(b - b_ref), kc * jnp.exp(b_ref - b))
        scores = jnp.where(lower, scores, 0.0)
        o = (jnp.einsum('bhts,bshv->bthv', scores, vc)
             + jnp.einsum('bthd,bhdv->bthv', qc * jnp.exp(b), s))
        s = (jnp.exp(b_end)[..., None] * s
             + jnp.einsum('bshd,bshv->bhdv', kc * jnp.exp(b_end[:, None] - b), vc))
        return s, o

    _, o = lax.scan(step, s0, (to_chunks(q), to_chunks(k), to_chunks(v), to_chunks(logf)))
    return jnp.swapaxes(o, 0, 1).reshape(bsz, t, nh, dv)


def final_state(k, v, logf):
    tail = lax.cumsum(logf, axis=1, reverse=True) - logf
    return jnp.einsum('bshd,bshv->bhdv', k * jnp.exp(tail), v)


def context_states(p):
    q, i, k_f, logf_f, k_b, logf_b = p[:6]
    rev = lambda a: jnp.flip(a, axis=1)
    return final_state(k_f, i, logf_f), final_state(rev(k_b), rev(i), rev(logf_b))


def chunk_mlp(u, v, w_s, b_s, n_chunks):
    bsz, t, _ = v.shape
    vf = v.astype(jnp.float32)
    mu = jnp.mean(vf, axis=-1, keepdims=True)
    var = jnp.mean(jnp.square(vf - mu), axis=-1, keepdims=True)
    vn = ((vf - mu) * lax.rsqrt(var + EPS)).astype(v.dtype)
    vn = vn.reshape(bsz, n_chunks, B_CHUNK, B_GROUPS, B_GROUP_DIM)
    s = jnp.einsum('gtp,bnpgc->bntgc', w_s, vn) + b_s.T[:, :, None]
    return u * s.reshape(bsz, t, B_WIDTH)


def token_mix(p, s_fwd0, s_bwd0, gnorm_g, w_s, b_s, w_branch_a, w_branch_b, w_out, n_chunks):
    q, i, k_f, logf_f, k_b, logf_b, og, u, v, gate_a, gate_b = p
    bsz, t = og.shape[:2]
    rev = lambda a: jnp.flip(a, axis=1)
    o = (gla_chunked(q, k_f, i, logf_f, s_fwd0)
         + rev(gla_chunked(rev(q), rev(k_b), rev(i), rev(logf_b), s_bwd0)))
    y_a = rms_norm(o, gnorm_g).reshape(bsz, t, A_WIDTH).astype(og.dtype) * jax.nn.silu(og)
    y_b = chunk_mlp(u, v, w_s, b_s, n_chunks)
    merged = (jax.nn.sigmoid(gate_a) * (y_a @ w_branch_a)
              + jax.nn.sigmoid(gate_b) * (y_b @ w_branch_b))
    return merged @ w_out


def moe_ffn(h, w_router, b_router, w1, b1, w2, b2):
    bsz, t, d = h.shape
    hf = h.reshape(bsz * t, d)
    n_tok = hf.shape[0]
    n_assign = n_tok * TOP_K
    logits = (hf @ w_router + b_router).astype(jnp.float32)
    top_logit, top_e = lax.top_k(logits, TOP_K)
    top_w = jax.nn.softmax(top_logit, axis=-1)
    e_flat = top_e.reshape(-1).astype(jnp.int32)
    w_flat = top_w.reshape(-1)
    e_sorted, order = lax.sort((e_flat, jnp.arange(n_assign, dtype=jnp.int32)),
                               num_keys=1, is_stable=True)
    counts = jnp.bincount(e_flat, length=N_EXPERTS)
    starts = jnp.cumsum(counts) - counts
    padded = (counts + MOE_BLOCK - 1) // MOE_BLOCK * MOE_BLOCK
    pad_ends = jnp.cumsum(padded)
    pad_starts = pad_ends - padded
    dest = pad_starts[e_sorted] + (jnp.arange(n_assign, dtype=jnp.int32) - starts[e_sorted])
    n_blocks = -(-n_assign // MOE_BLOCK) + N_EXPERTS
    cap = n_blocks * MOE_BLOCK
    slot_tok = jnp.full((cap,), n_tok, jnp.int32).at[dest].set(order // TOP_K)
    slot_w = jnp.zeros((cap,), jnp.float32).at[dest].set(w_flat[order])
    block_e = jnp.clip(jnp.searchsorted(pad_ends, jnp.arange(n_blocks) * MOE_BLOCK, side='right'),
                       0, N_EXPERTS - 1)
    x_pad = jnp.concatenate([hf, jnp.zeros((1, d), hf.dtype)], axis=0)
    xb = x_pad[slot_tok].reshape(n_blocks, MOE_BLOCK, d)

    def expert_block(args):
        xblk, e = args
        z = xblk @ w1[e] + b1[e]
        gate = jnp.minimum(z[:, :D_EXPERT], SWIGLU_LIMIT)
        lin = jnp.clip(z[:, D_EXPERT:], -SWIGLU_LIMIT, SWIGLU_LIMIT)
        return (gate * jax.nn.sigmoid(SWIGLU_ALPHA * gate) * (lin + 1.0)) @ w2[e] + b2[e]

    y = lax.map(expert_block, (xb, block_e)).reshape(cap, d)
    out = jnp.zeros((n_tok + 1, d), hf.dtype).at[slot_tok].add(
        (y * slot_w[:, None]).astype(hf.dtype))[:n_tok]
    return out.reshape(bsz, t, d)


def setup_inputs(seed: int = 0) -> dict:
    key = jax.random.key(seed)
    ks = jax.random.split(key, 24)
    nrm = lambda k, shape, s: s * jax.random.normal(k, shape, jnp.float32)
    L, E, F = DEPTH, N_EXPERTS, D_EXPERT
    return {
        'x': nrm(ks[0], (BATCH, SEQ, D_MODEL), 1.0),
        'c': nrm(ks[1], (BATCH, D_MODEL), 1.0),
        'ctx': nrm(ks[2], (BATCH, CTX_LEN, D_MODEL), 1.0),
        'c_ctx': nrm(ks[3], (D_MODEL,), 1.0),
        'norm1_g': 1.0 + nrm(ks[4], (L, D_MODEL), 0.02),
        'norm2_g': 1.0 + nrm(ks[5], (L, D_MODEL), 0.02),
        'w_mod': nrm(ks[6], (L, D_MODEL, N_MOD * D_MODEL), 0.5 * D_MODEL ** -0.5),
        'b_mod': nrm(ks[7], (L, N_MOD * D_MODEL), 0.02),
        'w_in': nrm(ks[8], (L, D_MODEL, IN_WIDTH), D_MODEL ** -0.5),
        'lb_fwd': nrm(ks[9], (L + 1, A_WIDTH), 0.1),
        'lb_bwd': nrm(ks[10], (L + 1, A_WIDTH), 0.1),
        'gnorm_g': 1.0 + nrm(ks[11], (L, A_HEAD_DIM), 0.02),
        'w_s': nrm(ks[12], (L, B_GROUPS, B_CHUNK, B_CHUNK), B_CHUNK ** -0.5),
        'b_s': nrm(ks[13], (L, B_GROUPS, B_CHUNK), 0.02),
        'w_branch_a': nrm(ks[14], (L, A_WIDTH, D_MODEL), A_WIDTH ** -0.5),
        'w_branch_b': nrm(ks[15], (L, B_WIDTH, D_MODEL), B_WIDTH ** -0.5),
        'w_out': nrm(ks[16], (L, D_MODEL, D_MODEL), D_MODEL ** -0.5),
        'w_router': nrm(ks[17], (L, D_MODEL, E), D_MODEL ** -0.5),
        'b_router': nrm(ks[18], (L, E), 0.01),
        'w1': nrm(ks[19], (L, E, D_MODEL, 2 * F), D_MODEL ** -0.5),
        'b1': nrm(ks[20], (L, E, 2 * F), 0.02),
        'w2': nrm(ks[21], (L, E, F, D_MODEL), F ** -0.5),
        'b2': nrm(ks[22], (L, E, D_MODEL), 0.02),
        'final_g': 1.0 + nrm(ks[23], (D_MODEL,), 0.02),
    }


def reference(x, c, ctx, c_ctx, norm1_g, norm2_g, w_mod, b_mod, w_in, lb_fwd, lb_bwd, gnorm_g,
              w_s, b_s, w_branch_a, w_branch_b, w_out, w_router, b_router, w1, b1, w2, b2, final_g):
    bsz, t, d = x.shape
    rows = t // GRID_W
    lat_chunks = rows // B_ROWS_PER_CHUNK
    ctx_chunks = ctx.shape[1] // B_CHUNK
    lb_f_all = jnp.cumsum(jax.nn.softmax(lb_fwd.astype(jnp.float32), axis=0), axis=0)
    lb_b_all = jnp.cumsum(jax.nn.softmax(lb_bwd.astype(jnp.float32), axis=0), axis=0)
    for layer in range(DEPTH):
        last = layer == DEPTH - 1
        mod = jax.nn.silu(c) @ w_mod[layer] + b_mod[layer]
        mod_c = jax.nn.silu(c_ctx) @ w_mod[layer] + b_mod[layer]
        sh1, sc1, g1, sh2, sc2, g2 = jnp.split(mod[:, None, :], N_MOD, axis=-1)
        csh1, csc1, cg1, csh2, csc2, cg2 = jnp.split(mod_c, N_MOD)
        p_lat = project(modulate(x, norm1_g[layer], sh1, sc1), w_in[layer],
                        lb_f_all[layer], lb_b_all[layer])
        p_ctx = project(modulate(ctx, norm1_g[layer], csh1, csc1), w_in[layer],
                        lb_f_all[layer], lb_b_all[layer])
        s_fwd, s_bwd = context_states(p_ctx)
        mix_w = (gnorm_g[layer], w_s[layer], b_s[layer], w_branch_a[layer], w_branch_b[layer],
                 w_out[layer])
        ffn_w = (w_router[layer], b_router[layer], w1[layer], b1[layer], w2[layer], b2[layer])
        x = x + g1 * token_mix(p_lat, s_fwd, s_bwd, *mix_w, lat_chunks)
        x = x + g2 * moe_ffn(modulate(x, norm2_g[layer], sh2, sc2), *ffn_w)
        if not last:
            zero = jnp.zeros_like(s_fwd)
            ctx = ctx + cg1 * token_mix(p_ctx, zero, zero, *mix_w, ctx_chunks)
            ctx = ctx + cg2 * moe_ffn(modulate(ctx, norm2_g[layer], csh2, csc2), *ffn_w)
    return rms_norm(x, final_g)
```

```python
import functools

import jax
import jax.numpy as jnp
from jax import lax
from jax.experimental import pallas as pl
from jax.experimental.pallas import tpu as pltpu

F32 = jnp.float32
BF16 = jnp.bfloat16

EPS = 1e-6
N_MOD = 6
N_SPLITS = 9
HEAD_DIM = 128
GLA_CHUNK = 64
MLP_CHUNK = 128
N_EXPERTS = 32
TOP_K = 4
MOE_BLOCK = 256
SWIGLU_LIMIT = 7.0
SWIGLU_ALPHA = 1.702
LANES = 128
NEG_BIG = -1e30
VMEM_LIMIT = 56 * 1024 * 1024


def _cparams(sem):
    return pltpu.CompilerParams(dimension_semantics=sem, vmem_limit_bytes=VMEM_LIMIT)


def _mod_kernel(c_ref, w_ref, b_ref, o_ref):
    c = c_ref[...]
    h = (c * jax.nn.sigmoid(c)).astype(BF16)
    o_ref[...] = jnp.dot(h, w_ref[...].astype(BF16), preferred_element_type=F32) + b_ref[...]


def _mod_call(cc, w_mod, b_mod):
    rows, d = cc.shape
    width = w_mod.shape[1]
    return pl.pallas_call(
        _mod_kernel,
        grid=(width // d,),
        in_specs=[pl.BlockSpec((rows, d), lambda j: (0, 0)),
                  pl.BlockSpec((d, d), lambda j: (0, j)),
                  pl.BlockSpec((1, d), lambda j: (0, j))],
        out_specs=pl.BlockSpec((rows, d), lambda j: (0, j)),
        out_shape=jax.ShapeDtypeStruct((rows, width), F32),
        compiler_params=_cparams(("arbitrary",)),
        name="mod",
    )(cc, w_mod, b_mod)


def _lower_bound(lb_ref):
    a = lb_ref[0:1, :]
    b = lb_ref[1:2, :]
    m = jnp.maximum(a, b)
    ea = jnp.exp(a - m)
    eb = jnp.exp(b - m)
    return ea / (ea + eb)


def _gelu(z):
    return 0.5 * z * (1.0 + lax.erf(z * (2.0 ** -0.5)))


def _proj_kernel(x_ref, g_ref, sh_ref, sc_ref, w_ref, lbf_ref, lbb_ref, o_ref, h_ref, *, split0):
    j = pl.program_id(2)

    @pl.when(j == 0)
    def _():
        x = x_ref[...]
        y = x * lax.rsqrt(jnp.mean(x * x, axis=-1, keepdims=True) + EPS) * g_ref[...]
        h_ref[...] = (y * (1.0 + sc_ref[...]) + sh_ref[...]).astype(BF16)

    z = jnp.dot(h_ref[...], w_ref[...], preferred_element_type=F32)
    s = j + split0

    @pl.when((s == 0) | (s == 4))
    def _():
        o_ref[...] = z * jax.nn.sigmoid(z)

    @pl.when(s == 1)
    def _():
        o_ref[...] = z

    @pl.when(s == 2)
    def _():
        lb = _lower_bound(lbf_ref)
        o_ref[...] = lb + (1.0 - lb) * jax.nn.sigmoid(z)

    @pl.when(s == 3)
    def _():
        lb = _lower_bound(lbb_ref)
        o_ref[...] = lb + (1.0 - lb) * jax.nn.sigmoid(z)

    @pl.when((s == 5) | (s == 6))
    def _():
        o_ref[...] = _gelu(z)

    @pl.when(s >= 7)
    def _():
        o_ref[...] = jax.nn.sigmoid(z)


def _proj_call(x, g, sh, sc, w_bf, lbf, lbb, *, split0, n_splits, tm, per_batch_mod):
    bsz, t, d = x.shape
    mod_map = (lambda b, i, j: (b, 0, 0)) if per_batch_mod else (lambda b, i, j: (0, 0, 0))
    return pl.pallas_call(
        functools.partial(_proj_kernel, split0=split0),
        grid=(bsz, t // tm, n_splits),
        in_specs=[pl.BlockSpec((None, tm, d), lambda b, i, j: (b, i, 0)),
                  pl.BlockSpec((1, d), lambda b, i, j: (0, 0)),
                  pl.BlockSpec((None, 1, d), mod_map),
                  pl.BlockSpec((None, 1, d), mod_map),
                  pl.BlockSpec((d, d), lambda b, i, j: (0, j + split0)),
                  pl.BlockSpec((2, d), lambda b, i, j: (0, 0)),
                  pl.BlockSpec((2, d), lambda b, i, j: (0, 0))],
        out_specs=pl.BlockSpec((None, tm, d), lambda b, i, j: (b, i, j)),
        out_shape=jax.ShapeDtypeStruct((bsz, t, n_splits * d), F32),
        scratch_shapes=[pltpu.VMEM((tm, d), BF16)],
        compiler_params=_cparams(("arbitrary", "arbitrary", "arbitrary")),
        name="proj",
    )(x, g, sh, sc, w_bf, lbf, lbb)


def _prefix_sum(x, reverse):
    n = x.shape[0]
    row = lax.broadcasted_iota(jnp.int32, x.shape, 0)
    s = 1
    while s < n:
        if reverse:
            x = x + jnp.where(row < n - s, pltpu.roll(x, n - s, 0), 0.0)
        else:
            x = x + jnp.where(row >= s, pltpu.roll(x, s, 0), 0.0)
        s *= 2
    return x


def _dot_nt(a, b):
    return lax.dot_general(a, b, (((1,), (1,)), ((), ())), preferred_element_type=F32)


def _dot_tn(a, b):
    return lax.dot_general(a, b, (((0,), (0,)), ((), ())), preferred_element_type=F32)


def _gla_chunk(q, v, f, st, reverse):
    c = q.shape[0]
    g = jnp.log(f)
    b = _prefix_sum(g, reverse)
    if reverse:
        b_mid = b[c // 2:c // 2 + 1, :]
        b_end = b[0:1, :]
    else:
        b_mid = b[c // 2 - 1:c // 2, :]
        b_end = b[c - 1:c, :]
    k = 1.0 - f
    qe = (q * jnp.exp(b - b_mid)).astype(BF16)
    ke = (k * jnp.exp(b_mid - b)).astype(BF16)
    scores = _dot_nt(qe, ke)
    ti = lax.broadcasted_iota(jnp.int32, (c, c), 0)
    si = lax.broadcasted_iota(jnp.int32, (c, c), 1)
    keep = (si >= ti) if reverse else (si <= ti)
    scores = jnp.where(keep, scores, 0.0)
    vb = v.astype(BF16)
    o = (jnp.dot(scores.astype(BF16), vb, preferred_element_type=F32)
         + _dot_nt((q * jnp.exp(b)).astype(BF16), st.astype(BF16)))
    kd = (k * jnp.exp(b_end - b)).astype(BF16)
    st_new = st * jnp.exp(b_end) + _dot_tn(vb, kd)
    return o, st_new


def _context_state(f, v, reverse_decay):
    g = jnp.log(f)
    tail = _prefix_sum(g, reverse_decay) - g
    kt = ((1.0 - f) * jnp.exp(tail)).astype(BF16)
    return _dot_tn(v.astype(BF16), kt)


def _gla_kernel(q_ref, i_ref, ff_ref, fb_ref, og_ref, ci_ref, cff_ref, cfb_ref, gn_ref,
                o_ref, of_ref, ob_ref, sf_ref, sb_ref):
    t = q_ref.shape[0]
    c = GLA_CHUNK
    n = t // c
    ci = ci_ref[...]
    sf_ref[...] = _context_state(cff_ref[...], ci, True)
    sb_ref[...] = _context_state(cfb_ref[...], ci, False)

    def body(step, carry):
        r0 = pl.multiple_of(step * c, c)
        o, st = _gla_chunk(q_ref[pl.ds(r0, c), :], i_ref[pl.ds(r0, c), :],
                           ff_ref[pl.ds(r0, c), :], sf_ref[...], False)
        of_ref[pl.ds(r0, c), :] = o
        sf_ref[...] = st
        r1 = pl.multiple_of((n - 1 - step) * c, c)
        o, st = _gla_chunk(q_ref[pl.ds(r1, c), :], i_ref[pl.ds(r1, c), :],
                           fb_ref[pl.ds(r1, c), :], sb_ref[...], True)
        ob_ref[pl.ds(r1, c), :] = o
        sb_ref[...] = st
        return carry

    lax.fori_loop(0, n, body, 0)
    o = of_ref[...] + ob_ref[...]
    y = o * lax.rsqrt(jnp.mean(o * o, axis=-1, keepdims=True) + EPS) * gn_ref[...]
    o_ref[...] = y * og_ref[...]


def _gla_call(z, zc, gnorm):
    bsz, t, width = z.shape
    d = width // N_SPLITS
    nh = d // HEAD_DIM
    tc = zc.shape[1]

    def zspec(split):
        return pl.BlockSpec((None, t, HEAD_DIM), lambda b, h: (b, 0, split * nh + h))

    def cspec(split):
        return pl.BlockSpec((None, tc, HEAD_DIM), lambda b, h: (b, 0, split * nh + h))

    return pl.pallas_call(
        _gla_kernel,
        grid=(bsz, nh),
        in_specs=[zspec(0), zspec(1), zspec(2), zspec(3), zspec(4),
                  cspec(0), cspec(1), cspec(2),
                  pl.BlockSpec((1, HEAD_DIM), lambda b, h: (0, 0))],
        out_specs=pl.BlockSpec((None, t, HEAD_DIM), lambda b, h: (b, 0, h)),
        out_shape=jax.ShapeDtypeStruct((bsz, t, d), F32),
        scratch_shapes=[pltpu.VMEM((t, HEAD_DIM), F32), pltpu.VMEM((t, HEAD_DIM), F32),
                        pltpu.VMEM((HEAD_DIM, HEAD_DIM), F32), pltpu.VMEM((HEAD_DIM, HEAD_DIM), F32)],
        compiler_params=_cparams(("arbitrary", "arbitrary")),
        name="gla",
    )(z, z, z, z, z, zc, zc, zc, gnorm)


def _mix_kernel(x_ref, u_ref, v_ref, ga_ref, gb_ref, ya_ref, ws_ref, bs_ref, wa_ref, wb_ref, wo_ref,
                g1_ref, n2_ref, sh2_ref, sc2_ref, wr_ref, br_ref,
                x1_ref, h2_ref, e_ref, w_ref, rank_ref, cnt_ref, yb_ref, carry_ref):
    tm, d = x_ref.shape
    first = (pl.program_id(0) == 0) & (pl.program_id(1) == 0)

    @pl.when(first)
    def _():
        carry_ref[...] = jnp.zeros_like(carry_ref)

    v = v_ref[...]
    mu = jnp.mean(v, axis=-1, keepdims=True)
    vc = v - mu
    var = jnp.mean(vc * vc, axis=-1, keepdims=True)
    vn = (vc * lax.rsqrt(var + EPS)).astype(BF16)
    for ck in range(tm // MLP_CHUNK):
        rows = slice(ck * MLP_CHUNK, (ck + 1) * MLP_CHUNK)
        for g in range(d // HEAD_DIM):
            cols = slice(g * HEAD_DIM, (g + 1) * HEAD_DIM)
            s = jnp.dot(ws_ref[g], vn[rows, cols], preferred_element_type=F32) + bs_ref[g]
            yb_ref[rows, cols] = (u_ref[rows, cols] * s).astype(BF16)

    pa = jnp.dot(ya_ref[...].astype(BF16), wa_ref[...], preferred_element_type=F32)
    pb = jnp.dot(yb_ref[...], wb_ref[...], preferred_element_type=F32)
    merged = ga_ref[...] * pa + gb_ref[...] * pb
    mixed = jnp.dot(merged.astype(BF16), wo_ref[...], preferred_element_type=F32)
    x1 = x_ref[...] + g1_ref[...] * mixed
    x1_ref[...] = x1

    y = x1 * lax.rsqrt(jnp.mean(x1 * x1, axis=-1, keepdims=True) + EPS) * n2_ref[...]
    h2 = y * (1.0 + sc2_ref[...]) + sh2_ref[...]
    h2_ref[...] = h2

    logits = jnp.dot(h2.astype(BF16), wr_ref[...], preferred_element_type=F32) + br_ref[...]
    lane = lax.broadcasted_iota(jnp.int32, (tm, LANES), 1)
    work = logits
    vals, idxs = [], []
    for _ in range(TOP_K):
        m = jnp.max(work, axis=-1, keepdims=True)
        idx = jnp.min(jnp.where(work == m, lane, LANES), axis=-1, keepdims=True)
        vals.append(m)
        idxs.append(idx)
        work = jnp.where(lane == idx, -jnp.inf, work)
    exps = [jnp.exp(m - vals[0]) for m in vals]
    denom = exps[0] + exps[1] + exps[2] + exps[3]

    sel = jnp.zeros((tm, LANES), F32)
    for idx in idxs:
        sel = sel + jnp.where(lane == idx, 1.0, 0.0)
    ri = lax.broadcasted_iota(jnp.int32, (tm, tm), 0)
    cj = lax.broadcasted_iota(jnp.int32, (tm, tm), 1)
    tri = jnp.where(cj < ri, 1.0, 0.0).astype(BF16)
    before = carry_ref[...] + jnp.dot(tri, sel.astype(BF16), preferred_element_type=F32)

    e_out = jnp.zeros((tm, LANES), jnp.int32)
    w_out = jnp.zeros((tm, LANES), F32)
    r_out = jnp.zeros((tm, LANES), F32)
    for k in range(TOP_K):
        rk = jnp.sum(jnp.where(lane == idxs[k], before, 0.0), axis=-1, keepdims=True)
        e_out = jnp.where(lane == k, idxs[k], e_out)
        w_out = jnp.where(lane == k, exps[k] / denom, w_out)
        r_out = jnp.where(lane == k, rk, r_out)
    e_ref[...] = e_out
    w_ref[...] = w_out
    rank_ref[...] = r_out.astype(jnp.int32)
    carry_ref[...] = carry_ref[...] + jnp.sum(sel, axis=0, keepdims=True)
    cnt_ref[...] = carry_ref[...]


def _mix_call(x, z, ya, ws_bf, bs_full, wa_bf, wb_bf, wo_bf, g1, n2, sh2, sc2, wr_pad, br_pad, *, tm):
    bsz, t, d = x.shape
    n = bsz * t

    def row(b, i):
        return (b, i, 0)

    def zspec(split):
        return pl.BlockSpec((None, tm, d), lambda b, i: (b, i, split))

    def const2(b, i):
        return (0, 0)

    def const3(b, i):
        return (0, 0, 0)

    def per_batch(b, i):
        return (b, 0, 0)

    nt = t // tm

    def flat(b, i):
        return (b * nt + i, 0)

    outs = pl.pallas_call(
        _mix_kernel,
        grid=(bsz, nt),
        in_specs=[pl.BlockSpec((None, tm, d), row),
                  zspec(5), zspec(6), zspec(7), zspec(8),
                  pl.BlockSpec((None, tm, d), row),
                  pl.BlockSpec(ws_bf.shape, const3),
                  pl.BlockSpec(bs_full.shape, const3),
                  pl.BlockSpec((d, d), const2), pl.BlockSpec((d, d), const2), pl.BlockSpec((d, d), const2),
                  pl.BlockSpec((None, 1, d), per_batch),
                  pl.BlockSpec((1, d), const2),
                  pl.BlockSpec((None, 1, d), per_batch), pl.BlockSpec((None, 1, d), per_batch),
                  pl.BlockSpec((d, LANES), const2), pl.BlockSpec((1, LANES), const2)],
        out_specs=[pl.BlockSpec((None, tm, d), row),
                   pl.BlockSpec((tm, d), flat),
                   pl.BlockSpec((tm, LANES), flat), pl.BlockSpec((tm, LANES), flat),
                   pl.BlockSpec((tm, LANES), flat),
                   pl.BlockSpec((1, LANES), const2)],
        out_shape=[jax.ShapeDtypeStruct((bsz, t, d), F32),
                   jax.ShapeDtypeStruct((n, d), F32),
                   jax.ShapeDtypeStruct((n, LANES), jnp.int32),
                   jax.ShapeDtypeStruct((n, LANES), F32),
                   jax.ShapeDtypeStruct((n, LANES), jnp.int32),
                   jax.ShapeDtypeStruct((1, LANES), F32)],
        scratch_shapes=[pltpu.VMEM((tm, d), BF16), pltpu.VMEM((1, LANES), F32)],
        compiler_params=_cparams(("arbitrary", "arbitrary")),
        name="mix",
    )(x, z, z, z, z, ya, ws_bf, bs_full, wa_bf, wb_bf, wo_bf, g1, n2, sh2, sc2, wr_pad, br_pad)
    return outs


def _row_copy(src_ref, src_row, dst_ref, dst_row, sem):
    return pltpu.make_async_copy(src_ref.at[pl.ds(src_row, 1)], dst_ref.at[pl.ds(dst_row, 1)], sem)


def _dispatch_kernel(dest_ref, h_ref, xin_ref, xb_ref, sem):
    del xin_ref
    tm = h_ref.shape[0]

    def issue(r, carry):
        for k in range(TOP_K):
            _row_copy(h_ref, r, xb_ref, dest_ref[0, 0, r * TOP_K + k], sem).start()
        return carry

    lax.fori_loop(0, tm, issue, 0)

    def drain(r, carry):
        for k in range(TOP_K):
            _row_copy(h_ref, r, xb_ref, dest_ref[0, 0, r * TOP_K + k], sem).wait()
        return carry

    lax.fori_loop(0, tm, drain, 0)


def _dispatch_call(dest_tiles, h2, xb_zero, *, tm):
    n, d = h2.shape
    cap = xb_zero.shape[0]
    return pl.pallas_call(
        _dispatch_kernel,
        grid=(n // tm,),
        in_specs=[pl.BlockSpec((1, 1, tm * TOP_K), lambda i: (i, 0, 0), memory_space=pltpu.SMEM),
                  pl.BlockSpec((tm, d), lambda i: (i, 0)),
                  pl.BlockSpec(memory_space=pl.ANY)],
        out_specs=pl.BlockSpec(memory_space=pl.ANY),
        out_shape=jax.ShapeDtypeStruct((cap, d), F32),
        scratch_shapes=[pltpu.SemaphoreType.DMA],
        input_output_aliases={2: 0},
        compiler_params=_cparams(("arbitrary",)),
        name="dispatch",
    )(dest_tiles, h2, xb_zero)


def _expert_kernel(be_ref, nu_ref, x_ref, w1_ref, b1_ref, w2_ref, b2_ref, y_ref):
    del be_ref
    f = w2_ref.shape[1]

    @pl.when(pl.program_id(0) < nu_ref[0])
    def _():
        z = jnp.dot(x_ref[...].astype(BF16), w1_ref[0], preferred_element_type=F32) + b1_ref[0]
        gate = jnp.minimum(z[:, :f], SWIGLU_LIMIT)
        lin = jnp.clip(z[:, f:], -SWIGLU_LIMIT, SWIGLU_LIMIT)
        act = gate * jax.nn.sigmoid(SWIGLU_ALPHA * gate) * (lin + 1.0)
        y_ref[...] = jnp.dot(act.astype(BF16), w2_ref[0], preferred_element_type=F32) + b2_ref[0]


def _expert_call(block_e, n_used, xb, w1_bf, b1, w2_bf, b2):
    cap, d = xb.shape
    n_blocks = cap // MOE_BLOCK
    f2 = w1_bf.shape[2]
    f = w2_bf.shape[1]

    def blk(i, be, nu):
        return jnp.minimum(i, nu[0] - 1)

    grid_spec = pltpu.PrefetchScalarGridSpec(
        num_scalar_prefetch=2,
        grid=(n_blocks,),
        in_specs=[pl.BlockSpec((MOE_BLOCK, d), lambda i, be, nu: (blk(i, be, nu), 0)),
                  pl.BlockSpec((1, d, f2), lambda i, be, nu: (be[blk(i, be, nu)], 0, 0)),
                  pl.BlockSpec((1, 1, f2), lambda i, be, nu: (be[blk(i, be, nu)], 0, 0)),
                  pl.BlockSpec((1, f, d), lambda i, be, nu: (be[blk(i, be, nu)], 0, 0)),
                  pl.BlockSpec((1, 1, d), lambda i, be, nu: (be[blk(i, be, nu)], 0, 0))],
        out_specs=pl.BlockSpec((MOE_BLOCK, d), lambda i, be, nu: (blk(i, be, nu), 0)),
    )
    return pl.pallas_call(
        _expert_kernel,
        grid_spec=grid_spec,
        out_shape=jax.ShapeDtypeStruct((cap, d), F32),
        compiler_params=_cparams(("arbitrary",)),
        name="experts",
    )(block_e, n_used, xb, w1_bf, b1, w2_bf, b2)


def _combine_kernel(dest_ref, x1_ref, w_ref, g2_ref, fg_ref, y_ref, o_ref, buf_ref, sem):
    tm = x1_ref.shape[0]

    def issue(r, carry):
        for k in range(TOP_K):
            _row_copy(y_ref, dest_ref[0, 0, r * TOP_K + k], buf_ref.at[k], r, sem).start()
        return carry

    lax.fori_loop(0, tm, issue, 0)

    def drain(r, carry):
        for k in range(TOP_K):
            _row_copy(y_ref, dest_ref[0, 0, r * TOP_K + k], buf_ref.at[k], r, sem).wait()
        return carry

    lax.fori_loop(0, tm, drain, 0)

    w = w_ref[...]
    moe = w[:, 0:1] * buf_ref[0]
    for k in range(1, TOP_K):
        moe = moe + w[:, k:k + 1] * buf_ref[k]
    x2 = x1_ref[...] + g2_ref[...] * moe
    o_ref[...] = x2 * lax.rsqrt(jnp.mean(x2 * x2, axis=-1, keepdims=True) + EPS) * fg_ref[...]


def _combine_call(dest_tiles, x1, w_lanes, g2, final_g, y, *, tm):
    bsz, t, d = x1.shape
    nt = t // tm
    return pl.pallas_call(
        _combine_kernel,
        grid=(bsz, nt),
        in_specs=[pl.BlockSpec((1, 1, tm * TOP_K), lambda b, i: (b * nt + i, 0, 0), memory_space=pltpu.SMEM),
                  pl.BlockSpec((None, tm, d), lambda b, i: (b, i, 0)),
                  pl.BlockSpec((tm, LANES), lambda b, i: (b * nt + i, 0)),
                  pl.BlockSpec((None, 1, d), lambda b, i: (b, 0, 0)),
                  pl.BlockSpec((1, d), lambda b, i: (0, 0)),
                  pl.BlockSpec(memory_space=pl.ANY)],
        out_specs=pl.BlockSpec((None, tm, d), lambda b, i: (b, i, 0)),
        out_shape=jax.ShapeDtypeStruct((bsz, t, d), F32),
        scratch_shapes=[pltpu.VMEM((TOP_K, tm, d), F32), pltpu.SemaphoreType.DMA],
        compiler_params=_cparams(("arbitrary", "arbitrary")),
        name="combine",
    )(dest_tiles, x1, w_lanes, g2, final_g, y)


def kernel(x, c, ctx, c_ctx, norm1_g, norm2_g, w_mod, b_mod, w_in, lb_fwd, lb_bwd, gnorm_g, w_s, b_s,
           w_branch_a, w_branch_b, w_out, w_router, b_router, w1, b1, w2, b2, final_g):
    bsz, t, d = x.shape
    assert w_in.shape[0] == 1 and lb_fwd.shape[0] == 2, "single-layer block"
    assert w_in.shape[2] == N_SPLITS * d and d % HEAD_DIM == 0
    assert t % MLP_CHUNK == 0 and ctx.shape[1] % GLA_CHUNK == 0
    n = bsz * t

    pad_rows = (-(bsz + 1)) % 8
    cc = jnp.concatenate([c, c_ctx[None, :], jnp.zeros((pad_rows, d), F32)], axis=0)
    mod = _mod_call(cc, w_mod[0], b_mod[0][None, :])
    lat = mod[:bsz].reshape(bsz, N_MOD, 1, d)
    sh1, sc1, g1, sh2, sc2, g2 = (lat[:, m] for m in range(N_MOD))
    cmod = mod[bsz].reshape(N_MOD, 1, 1, d)
    csh1, csc1 = cmod[0], cmod[1]

    w_in_bf = w_in[0].astype(BF16)
    n1 = norm1_g[0][None, :]
    tm_proj = min(1024, t)
    z = _proj_call(x, n1, sh1, sc1, w_in_bf, lb_fwd, lb_bwd,
                   split0=0, n_splits=N_SPLITS, tm=tm_proj, per_batch_mod=True)
    zc = _proj_call(ctx, n1, csh1, csc1, w_in_bf, lb_fwd, lb_bwd,
                    split0=1, n_splits=3, tm=ctx.shape[1], per_batch_mod=False)

    ya = _gla_call(z, zc, gnorm_g[0][None, :])

    n_exp = w_router.shape[2]
    wr_pad = jnp.zeros((d, LANES), BF16).at[:, :n_exp].set(w_router[0].astype(BF16))
    br_pad = jnp.full((1, LANES), NEG_BIG, F32).at[0, :n_exp].set(b_router[0])
    bs_full = jnp.broadcast_to(b_s[0][:, :, None], b_s.shape[1:] + (HEAD_DIM,))
    tm_mix = min(256, t)
    x1, h2, e_l, w_l, r_l, cnt = _mix_call(
        x, z, ya, w_s[0].astype(BF16), bs_full,
        w_branch_a[0].astype(BF16), w_branch_b[0].astype(BF16), w_out[0].astype(BF16),
        g1, norm2_g[0][None, :], sh2, sc2, wr_pad, br_pad, tm=tm_mix)

    counts = cnt[0, :n_exp].astype(jnp.int32)
    padded = (counts + MOE_BLOCK - 1) // MOE_BLOCK * MOE_BLOCK
    pad_ends = jnp.cumsum(padded)
    pad_starts = pad_ends - padded
    n_blocks = -(-(n * TOP_K) // MOE_BLOCK) + n_exp
    cap = n_blocks * MOE_BLOCK
    dest = pad_starts[e_l[:, :TOP_K]] + r_l[:, :TOP_K]
    block_e = jnp.clip(jnp.searchsorted(pad_ends, jnp.arange(n_blocks, dtype=jnp.int32) * MOE_BLOCK,
                                        side='right'), 0, n_exp - 1).astype(jnp.int32)
    n_used = (pad_ends[-1:] // MOE_BLOCK).astype(jnp.int32)

    tm_io = min(256, t)
    dest_tiles = dest.reshape(n // tm_io, 1, tm_io * TOP_K)
    xb = _dispatch_call(dest_tiles, h2, jnp.zeros((cap, d), F32), tm=tm_io)
    y = _expert_call(block_e, n_used, xb, w1[0].astype(BF16), b1[0][:, None, :],
                     w2[0].astype(BF16), b2[0][:, None, :])
    return _combine_call(dest_tiles, x1, w_l, g2, final_g[None, :], y, tm=tm_io)
```

```python
import functools

import jax
import jax.numpy as jnp
from jax import lax
from jax.experimental import pallas as pl
from jax.experimental.pallas import tpu as pltpu

F32 = jnp.float32
BF16 = jnp.bfloat16

EPS = 1e-6
N_MOD = 6
S_Q, S_I, S_FF, S_FB, S_OG, S_U, S_V, S_GA, S_GB = range(9)
N_SPLITS = 9
SPLIT_DTYPE = {S_FF: F32, S_FB: F32}
HEAD_DIM = 128
GLA_CHUNK = 64
MLP_CHUNK = 128
TOP_K = 4
MOE_BLOCK = 256
SWIGLU_LIMIT = 7.0
SWIGLU_ALPHA = 1.702
LANES = 128
NEG_BIG = -1e30
VMEM_LIMIT = 56 * 1024 * 1024


def _cparams(sem):
    return pltpu.CompilerParams(dimension_semantics=sem, vmem_limit_bytes=VMEM_LIMIT)


def _resident(shape):
    zeros = (0,) * len(shape)
    return pl.BlockSpec(shape, lambda *_: zeros, pipeline_mode=pl.Buffered(1))


def _mod_kernel(c_ref, w_ref, b_ref, o_ref):
    c = c_ref[...]
    h = (c * jax.nn.sigmoid(c)).astype(BF16)
    o_ref[...] = jnp.dot(h, w_ref[...].astype(BF16), preferred_element_type=F32) + b_ref[...]


def _mod_call(cc, w_mod, b_mod):
    rows, d = cc.shape
    width = w_mod.shape[1]
    return pl.pallas_call(
        _mod_kernel,
        grid=(width // d,),
        in_specs=[pl.BlockSpec((rows, d), lambda j: (0, 0)),
                  pl.BlockSpec((d, d), lambda j: (0, j)),
                  pl.BlockSpec((1, d), lambda j: (0, j))],
        out_specs=pl.BlockSpec((rows, d), lambda j: (0, j)),
        out_shape=jax.ShapeDtypeStruct((rows, width), F32),
        compiler_params=_cparams(("arbitrary",)),
        name="mod",
    )(cc, w_mod, b_mod)


def _lower_bound(lb_ref):
    a = lb_ref[0:1, :]
    b = lb_ref[1:2, :]
    m = jnp.maximum(a, b)
    ea = jnp.exp(a - m)
    eb = jnp.exp(b - m)
    return ea / (ea + eb)


def _gelu(z):
    return 0.5 * z * (1.0 + lax.erf(z * (2.0 ** -0.5)))


def _split_activation(s, z, lbf_ref, lbb_ref):
    if s in (S_Q, S_OG):
        return z * jax.nn.sigmoid(z)
    if s == S_I:
        return z
    if s in (S_FF, S_FB):
        lb = _lower_bound(lbf_ref if s == S_FF else lbb_ref)
        return lb + (1.0 - lb) * jax.nn.sigmoid(z)
    if s in (S_U, S_V):
        return _gelu(z)
    return jax.nn.sigmoid(z)


def _proj_kernel(x_ref, g_ref, sh_ref, sc_ref, w_ref, lbf_ref, lbb_ref, *o_refs, splits):
    d = x_ref.shape[1]
    x = x_ref[...]
    y = x * lax.rsqrt(jnp.mean(x * x, axis=-1, keepdims=True) + EPS) * g_ref[...]
    h = (y * (1.0 + sc_ref[...]) + sh_ref[...]).astype(BF16)
    for s, o_ref in zip(splits, o_refs):
        z = jnp.dot(h, w_ref[:, s * d:(s + 1) * d], preferred_element_type=F32)
        o_ref[...] = _split_activation(s, z, lbf_ref, lbb_ref).astype(o_ref.dtype)


def _proj_call(x, g, sh, sc, w_bf, lbf, lbb, *, splits, tm, per_batch_mod):
    bsz, t, d = x.shape
    mod_map = (lambda b, i: (b, 0, 0)) if per_batch_mod else (lambda b, i: (0, 0, 0))
    row = pl.BlockSpec((None, tm, d), lambda b, i: (b, i, 0))
    return pl.pallas_call(
        functools.partial(_proj_kernel, splits=splits),
        grid=(bsz, t // tm),
        in_specs=[row,
                  pl.BlockSpec((1, d), lambda b, i: (0, 0)),
                  pl.BlockSpec((None, 1, d), mod_map),
                  pl.BlockSpec((None, 1, d), mod_map),
                  _resident(w_bf.shape),
                  pl.BlockSpec((2, d), lambda b, i: (0, 0)),
                  pl.BlockSpec((2, d), lambda b, i: (0, 0))],
        out_specs=[row for _ in splits],
        out_shape=[jax.ShapeDtypeStruct((bsz, t, d), SPLIT_DTYPE.get(s, BF16)) for s in splits],
        compiler_params=_cparams(("arbitrary", "arbitrary")),
        name="proj",
    )(x, g, sh, sc, w_bf, lbf, lbb)


def _prefix_sum(x, reverse):
    n = x.shape[0]
    row = lax.broadcasted_iota(jnp.int32, x.shape, 0)
    s = 1
    while s < n:
        if reverse:
            x = x + jnp.where(row < n - s, pltpu.roll(x, n - s, 0), 0.0)
        else:
            x = x + jnp.where(row >= s, pltpu.roll(x, s, 0), 0.0)
        s *= 2
    return x


def _dot_nt(a, b):
    return lax.dot_general(a, b, (((1,), (1,)), ((), ())), preferred_element_type=F32)


def _dot_tn(a, b):
    return lax.dot_general(a, b, (((0,), (0,)), ((), ())), preferred_element_type=F32)


def _gla_chunk(q, vb, f, st, reverse):
    c = q.shape[0]
    g = jnp.log(f)
    b = _prefix_sum(g, reverse)
    if reverse:
        b_mid = b[c // 2:c // 2 + 1, :]
        b_end = b[0:1, :]
    else:
        b_mid = b[c // 2 - 1:c // 2, :]
        b_end = b[c - 1:c, :]
    k = 1.0 - f
    qe = (q * jnp.exp(b - b_mid)).astype(BF16)
    ke = (k * jnp.exp(b_mid - b)).astype(BF16)
    scores = _dot_nt(qe, ke)
    ti = lax.broadcasted_iota(jnp.int32, (c, c), 0)
    si = lax.broadcasted_iota(jnp.int32, (c, c), 1)
    keep = (si >= ti) if reverse else (si <= ti)
    scores = jnp.where(keep, scores, 0.0)
    o = (jnp.dot(scores.astype(BF16), vb, preferred_element_type=F32)
         + _dot_nt((q * jnp.exp(b)).astype(BF16), st.astype(BF16)))
    kd = (k * jnp.exp(b_end - b)).astype(BF16)
    st_new = st * jnp.exp(b_end) + _dot_tn(vb, kd)
    return o, st_new


def _context_state(f, vb, reverse_decay):
    g = jnp.log(f)
    tail = _prefix_sum(g, reverse_decay) - g
    kt = ((1.0 - f) * jnp.exp(tail)).astype(BF16)
    return _dot_tn(vb, kt)


def _gla_kernel(q_ref, i_ref, ff_ref, fb_ref, og_ref, ci_ref, cff_ref, cfb_ref, gn_ref,
                o_ref, of_ref, ob_ref, sf_ref, sb_ref):
    t = q_ref.shape[0]
    c = GLA_CHUNK
    n = t // c
    ci = ci_ref[...]
    sf_ref[...] = _context_state(cff_ref[...], ci, True)
    sb_ref[...] = _context_state(cfb_ref[...], ci, False)

    def body(step, carry):
        r0 = pl.multiple_of(step * c, c)
        o, st = _gla_chunk(q_ref[pl.ds(r0, c), :].astype(F32), i_ref[pl.ds(r0, c), :],
                           ff_ref[pl.ds(r0, c), :], sf_ref[...], False)
        of_ref[pl.ds(r0, c), :] = o
        sf_ref[...] = st
        r1 = pl.multiple_of((n - 1 - step) * c, c)
        o, st = _gla_chunk(q_ref[pl.ds(r1, c), :].astype(F32), i_ref[pl.ds(r1, c), :],
                           fb_ref[pl.ds(r1, c), :], sb_ref[...], True)
        ob_ref[pl.ds(r1, c), :] = o
        sb_ref[...] = st
        return carry

    lax.fori_loop(0, n, body, 0)
    o = of_ref[...] + ob_ref[...]
    y = o * lax.rsqrt(jnp.mean(o * o, axis=-1, keepdims=True) + EPS) * gn_ref[...]
    o_ref[...] = (y * og_ref[...].astype(F32)).astype(o_ref.dtype)


def _gla_call(q, i, ff, fb, og, ci, cff, cfb, gnorm):
    bsz, t, d = q.shape
    nh = d // HEAD_DIM
    tc = ci.shape[1]
    lat = pl.BlockSpec((None, t, HEAD_DIM), lambda b, h: (b, 0, h))
    ctx = pl.BlockSpec((None, tc, HEAD_DIM), lambda b, h: (b, 0, h))
    return pl.pallas_call(
        _gla_kernel,
        grid=(bsz, nh),
        in_specs=[lat, lat, lat, lat, lat, ctx, ctx, ctx,
                  pl.BlockSpec((1, HEAD_DIM), lambda b, h: (0, 0))],
        out_specs=lat,
        out_shape=jax.ShapeDtypeStruct((bsz, t, d), BF16),
        scratch_shapes=[pltpu.VMEM((t, HEAD_DIM), F32), pltpu.VMEM((t, HEAD_DIM), F32),
                        pltpu.VMEM((HEAD_DIM, HEAD_DIM), F32), pltpu.VMEM((HEAD_DIM, HEAD_DIM), F32)],
        compiler_params=_cparams(("arbitrary", "arbitrary")),
        name="gla",
    )(q, i, ff, fb, og, ci, cff, cfb, gnorm)


def _mix_kernel(x_ref, u_ref, v_ref, ga_ref, gb_ref, ya_ref, ws_ref, bs_ref, wa_ref, wb_ref, wo_ref,
                g1_ref, n2_ref, sh2_ref, sc2_ref, wr_ref, br_ref,
                x1_ref, h2_ref, e_ref, w_ref, rank_ref, cnt_ref, yb_ref, carry_ref):
    tm, d = x_ref.shape
    first = (pl.program_id(0) == 0) & (pl.program_id(1) == 0)

    @pl.when(first)
    def _():
        carry_ref[...] = jnp.zeros_like(carry_ref)

    v = v_ref[...].astype(F32)
    mu = jnp.mean(v, axis=-1, keepdims=True)
    vc = v - mu
    var = jnp.mean(vc * vc, axis=-1, keepdims=True)
    vn = (vc * lax.rsqrt(var + EPS)).astype(BF16)
    for ck in range(tm // MLP_CHUNK):
        rows = slice(ck * MLP_CHUNK, (ck + 1) * MLP_CHUNK)
        for g in range(d // HEAD_DIM):
            cols = slice(g * HEAD_DIM, (g + 1) * HEAD_DIM)
            s = jnp.dot(ws_ref[g], vn[rows, cols], preferred_element_type=F32) + bs_ref[g]
            yb_ref[rows, cols] = (u_ref[rows, cols].astype(F32) * s).astype(BF16)

    pa = jnp.dot(ya_ref[...], wa_ref[...], preferred_element_type=F32)
    pb = jnp.dot(yb_ref[...], wb_ref[...], preferred_element_type=F32)
    merged = ga_ref[...].astype(F32) * pa + gb_ref[...].astype(F32) * pb
    mixed = jnp.dot(merged.astype(BF16), wo_ref[...], preferred_element_type=F32)
    x1 = x_ref[...] + g1_ref[...] * mixed
    x1_ref[...] = x1

    y = x1 * lax.rsqrt(jnp.mean(x1 * x1, axis=-1, keepdims=True) + EPS) * n2_ref[...]
    h2 = y * (1.0 + sc2_ref[...]) + sh2_ref[...]
    h2_ref[...] = h2

    logits = jnp.dot(h2.astype(BF16), wr_ref[...], preferred_element_type=F32) + br_ref[...]
    lane = lax.broadcasted_iota(jnp.int32, (tm, LANES), 1)
    work = logits
    vals, idxs = [], []
    for _ in range(TOP_K):
        m = jnp.max(work, axis=-1, keepdims=True)
        idx = jnp.min(jnp.where(work == m, lane, LANES), axis=-1, keepdims=True)
        vals.append(m)
        idxs.append(idx)
        work = jnp.where(lane == idx, -jnp.inf, work)
    exps = [jnp.exp(m - vals[0]) for m in vals]
    denom = exps[0] + exps[1] + exps[2] + exps[3]

    sel = jnp.zeros((tm, LANES), F32)
    for idx in idxs:
        sel = sel + jnp.where(lane == idx, 1.0, 0.0)
    ri = lax.broadcasted_iota(jnp.int32, (tm, tm), 0)
    cj = lax.broadcasted_iota(jnp.int32, (tm, tm), 1)
    tri = jnp.where(cj < ri, 1.0, 0.0).astype(BF16)
    before = carry_ref[...] + jnp.dot(tri, sel.astype(BF16), preferred_element_type=F32)

    e_out = jnp.zeros((tm, LANES), jnp.int32)
    w_out = jnp.zeros((tm, LANES), F32)
    r_out = jnp.zeros((tm, LANES), F32)
    for k in range(TOP_K):
        rk = jnp.sum(jnp.where(lane == idxs[k], before, 0.0), axis=-1, keepdims=True)
        e_out = jnp.where(lane == k, idxs[k], e_out)
        w_out = jnp.where(lane == k, exps[k] / denom, w_out)
        r_out = jnp.where(lane == k, rk, r_out)
    e_ref[...] = e_out
    w_ref[...] = w_out
    rank_ref[...] = r_out.astype(jnp.int32)
    carry_ref[...] = carry_ref[...] + jnp.sum(sel, axis=0, keepdims=True)
    cnt_ref[...] = carry_ref[...]


def _mix_call(x, u, v, ga, gb, ya, ws_bf, bs_full, wa_bf, wb_bf, wo_bf, g1, n2, sh2, sc2, wr_pad, br_pad, *, tm):
    bsz, t, d = x.shape
    n = bsz * t
    nt = t // tm
    row = pl.BlockSpec((None, tm, d), lambda b, i: (b, i, 0))
    per_batch = pl.BlockSpec((None, 1, d), lambda b, i: (b, 0, 0))

    def flat(width):
        return pl.BlockSpec((tm, width), lambda b, i: (b * nt + i, 0))

    return pl.pallas_call(
        _mix_kernel,
        grid=(bsz, nt),
        in_specs=[row, row, row, row, row, row,
                  _resident(ws_bf.shape), _resident(bs_full.shape),
                  _resident((d, d)), _resident((d, d)), _resident((d, d)),
                  per_batch, _resident((1, d)), per_batch, per_batch,
                  _resident((d, LANES)), _resident((1, LANES))],
        out_specs=[row, flat(d), flat(LANES), flat(LANES), flat(LANES),
                   pl.BlockSpec((1, LANES), lambda b, i: (0, 0))],
        out_shape=[jax.ShapeDtypeStruct((bsz, t, d), F32),
                   jax.ShapeDtypeStruct((n, d), F32),
                   jax.ShapeDtypeStruct((n, LANES), jnp.int32),
                   jax.ShapeDtypeStruct((n, LANES), F32),
                   jax.ShapeDtypeStruct((n, LANES), jnp.int32),
                   jax.ShapeDtypeStruct((1, LANES), F32)],
        scratch_shapes=[pltpu.VMEM((tm, d), BF16), pltpu.VMEM((1, LANES), F32)],
        compiler_params=_cparams(("arbitrary", "arbitrary")),
        name="mix",
    )(x, u, v, ga, gb, ya, ws_bf, bs_full, wa_bf, wb_bf, wo_bf, g1, n2, sh2, sc2, wr_pad, br_pad)


def _padfill_kernel(ends_ref, padded_ref, xb_ref, zero_ref, sem):
    zero_ref[...] = jnp.zeros_like(zero_ref)
    n_exp = ends_ref.shape[0]

    def tail_copy(e):
        start = pl.multiple_of(ends_ref[e] - MOE_BLOCK, MOE_BLOCK)
        return pltpu.make_async_copy(zero_ref, xb_ref.at[pl.ds(start, MOE_BLOCK)], sem)

    for e in range(n_exp):
        @pl.when(padded_ref[e] > 0)
        def _():
            tail_copy(e).start()
    for e in range(n_exp):
        @pl.when(padded_ref[e] > 0)
        def _():
            tail_copy(e).wait()


def _padfill_call(pad_ends, padded, cap, d):
    return pl.pallas_call(
        _padfill_kernel,
        grid_spec=pltpu.PrefetchScalarGridSpec(
            num_scalar_prefetch=2, grid=(1,), in_specs=[],
            out_specs=pl.BlockSpec(memory_space=pl.ANY),
            scratch_shapes=[pltpu.VMEM((MOE_BLOCK, d), F32), pltpu.SemaphoreType.DMA]),
        out_shape=jax.ShapeDtypeStruct((cap, d), F32),
        compiler_params=_cparams(("arbitrary",)),
        name="padfill",
    )(pad_ends, padded)


def _row_copy(src_ref, src_row, dst_ref, dst_row, sem):
    return pltpu.make_async_copy(src_ref.at[pl.ds(src_row, 1)], dst_ref.at[pl.ds(dst_row, 1)], sem)


def _dispatch_kernel(dest_ref, h_ref, xin_ref, xb_ref, sem):
    del xin_ref
    tm = h_ref.shape[0]

    def issue(r, carry):
        for k in range(TOP_K):
            _row_copy(h_ref, r, xb_ref, dest_ref[0, 0, r * TOP_K + k], sem).start()
        return carry

    lax.fori_loop(0, tm, issue, 0)

    def drain(r, carry):
        for k in range(TOP_K):
            _row_copy(h_ref, r, xb_ref, dest_ref[0, 0, r * TOP_K + k], sem).wait()
        return carry

    lax.fori_loop(0, tm, drain, 0)


def _dispatch_call(dest_tiles, h2, xb_init, *, tm):
    n, d = h2.shape
    return pl.pallas_call(
        _dispatch_kernel,
        grid=(n // tm,),
        in_specs=[pl.BlockSpec((1, 1, tm * TOP_K), lambda i: (i, 0, 0), memory_space=pltpu.SMEM),
                  pl.BlockSpec((tm, d), lambda i: (i, 0)),
                  pl.BlockSpec(memory_space=pl.ANY)],
        out_specs=pl.BlockSpec(memory_space=pl.ANY),
        out_shape=jax.ShapeDtypeStruct(xb_init.shape, F32),
        scratch_shapes=[pltpu.SemaphoreType.DMA],
        input_output_aliases={2: 0},
        compiler_params=_cparams(("arbitrary",)),
        name="dispatch",
    )(dest_tiles, h2, xb_init)


def _expert_kernel(be_ref, nu_ref, x_ref, w1_ref, b1_ref, w2_ref, b2_ref, y_ref):
    del be_ref
    f = w2_ref.shape[1]

    @pl.when(pl.program_id(0) < nu_ref[0])
    def _():
        z = jnp.dot(x_ref[...].astype(BF16), w1_ref[0], preferred_element_type=F32) + b1_ref[0]
        gate = jnp.minimum(z[:, :f], SWIGLU_LIMIT)
        lin = jnp.clip(z[:, f:], -SWIGLU_LIMIT, SWIGLU_LIMIT)
        act = gate * jax.nn.sigmoid(SWIGLU_ALPHA * gate) * (lin + 1.0)
        y_ref[...] = jnp.dot(act.astype(BF16), w2_ref[0], preferred_element_type=F32) + b2_ref[0]


def _expert_call(block_e, n_used, xb, w1_bf, b1, w2_bf, b2):
    cap, d = xb.shape
    n_blocks = cap // MOE_BLOCK
    f2 = w1_bf.shape[2]
    f = w2_bf.shape[1]

    def blk(i, be, nu):
        return jnp.minimum(i, nu[0] - 1)

    grid_spec = pltpu.PrefetchScalarGridSpec(
        num_scalar_prefetch=2,
        grid=(n_blocks,),
        in_specs=[pl.BlockSpec((MOE_BLOCK, d), lambda i, be, nu: (blk(i, be, nu), 0)),
                  pl.BlockSpec((1, d, f2), lambda i, be, nu: (be[blk(i, be, nu)], 0, 0)),
                  pl.BlockSpec((1, 1, f2), lambda i, be, nu: (be[blk(i, be, nu)], 0, 0)),
                  pl.BlockSpec((1, f, d), lambda i, be, nu: (be[blk(i, be, nu)], 0, 0)),
                  pl.BlockSpec((1, 1, d), lambda i, be, nu: (be[blk(i, be, nu)], 0, 0))],
        out_specs=pl.BlockSpec((MOE_BLOCK, d), lambda i, be, nu: (blk(i, be, nu), 0)),
    )
    return pl.pallas_call(
        _expert_kernel,
        grid_spec=grid_spec,
        out_shape=jax.ShapeDtypeStruct((cap, d), F32),
        compiler_params=_cparams(("arbitrary",)),
        name="experts",
    )(block_e, n_used, xb, w1_bf, b1, w2_bf, b2)


def _combine_kernel(dest_ref, x1_ref, w_ref, g2_ref, fg_ref, y_ref, o_ref, buf_ref, sem):
    tm = x1_ref.shape[0]

    def issue(r, carry):
        for k in range(TOP_K):
            _row_copy(y_ref, dest_ref[0, 0, r * TOP_K + k], buf_ref.at[k], r, sem).start()
        return carry

    lax.fori_loop(0, tm, issue, 0)

    def drain(r, carry):
        for k in range(TOP_K):
            _row_copy(y_ref, dest_ref[0, 0, r * TOP_K + k], buf_ref.at[k], r, sem).wait()
        return carry

    lax.fori_loop(0, tm, drain, 0)

    w = w_ref[...]
    moe = w[:, 0:1] * buf_ref[0]
    for k in range(1, TOP_K):
        moe = moe + w[:, k:k + 1] * buf_ref[k]
    x2 = x1_ref[...] + g2_ref[...] * moe
    o_ref[...] = x2 * lax.rsqrt(jnp.mean(x2 * x2, axis=-1, keepdims=True) + EPS) * fg_ref[...]


def _combine_call(dest_tiles, x1, w_lanes, g2, final_g, y, *, tm):
    bsz, t, d = x1.shape
    nt = t // tm
    return pl.pallas_call(
        _combine_kernel,
        grid=(bsz, nt),
        in_specs=[pl.BlockSpec((1, 1, tm * TOP_K), lambda b, i: (b * nt + i, 0, 0), memory_space=pltpu.SMEM),
                  pl.BlockSpec((None, tm, d), lambda b, i: (b, i, 0)),
                  pl.BlockSpec((tm, LANES), lambda b, i: (b * nt + i, 0)),
                  pl.BlockSpec((None, 1, d), lambda b, i: (b, 0, 0)),
                  pl.BlockSpec((1, d), lambda b, i: (0, 0)),
                  pl.BlockSpec(memory_space=pl.ANY)],
        out_specs=pl.BlockSpec((None, tm, d), lambda b, i: (b, i, 0)),
        out_shape=jax.ShapeDtypeStruct((bsz, t, d), F32),
        scratch_shapes=[pltpu.VMEM((TOP_K, tm, d), F32), pltpu.SemaphoreType.DMA],
        compiler_params=_cparams(("arbitrary", "arbitrary")),
        name="combine",
    )(dest_tiles, x1, w_lanes, g2, final_g, y)


def kernel(x, c, ctx, c_ctx, norm1_g, norm2_g, w_mod, b_mod, w_in, lb_fwd, lb_bwd, gnorm_g, w_s, b_s,
           w_branch_a, w_branch_b, w_out, w_router, b_router, w1, b1, w2, b2, final_g):
    bsz, t, d = x.shape
    assert w_in.shape[0] == 1 and lb_fwd.shape[0] == 2, "single-layer block"
    assert w_in.shape[2] == N_SPLITS * d and d % HEAD_DIM == 0
    assert t % MLP_CHUNK == 0 and ctx.shape[1] % GLA_CHUNK == 0
    n = bsz * t

    pad_rows = (-(bsz + 1)) % 8
    cc = jnp.concatenate([c, c_ctx[None, :], jnp.zeros((pad_rows, d), F32)], axis=0)
    mod = _mod_call(cc, w_mod[0], b_mod[0][None, :])
    lat = mod[:bsz].reshape(bsz, N_MOD, 1, d)
    sh1, sc1, g1, sh2, sc2, g2 = (lat[:, m] for m in range(N_MOD))
    cmod = mod[bsz].reshape(N_MOD, 1, 1, d)
    csh1, csc1 = cmod[0], cmod[1]

    w_in_bf = w_in[0].astype(BF16)
    n1 = norm1_g[0][None, :]
    q, i_, ff, fb, og, u, v, ga, gb = _proj_call(
        x, n1, sh1, sc1, w_in_bf, lb_fwd, lb_bwd,
        splits=tuple(range(N_SPLITS)), tm=min(256, t), per_batch_mod=True)
    ci, cff, cfb = _proj_call(
        ctx, n1, csh1, csc1, w_in_bf, lb_fwd, lb_bwd,
        splits=(S_I, S_FF, S_FB), tm=ctx.shape[1], per_batch_mod=False)

    ya = _gla_call(q, i_, ff, fb, og, ci, cff, cfb, gnorm_g[0][None, :])

    n_exp = w_router.shape[2]
    wr_pad = jnp.zeros((d, LANES), BF16).at[:, :n_exp].set(w_router[0].astype(BF16))
    br_pad = jnp.full((1, LANES), NEG_BIG, F32).at[0, :n_exp].set(b_router[0])
    bs_full = jnp.broadcast_to(b_s[0][:, :, None], b_s.shape[1:] + (HEAD_DIM,))
    x1, h2, e_l, w_l, r_l, cnt = _mix_call(
        x, u, v, ga, gb, ya, w_s[0].astype(BF16), bs_full,
        w_branch_a[0].astype(BF16), w_branch_b[0].astype(BF16), w_out[0].astype(BF16),
        g1, norm2_g[0][None, :], sh2, sc2, wr_pad, br_pad, tm=min(256, t))

    counts = cnt[0, :n_exp].astype(jnp.int32)
    padded = (counts + MOE_BLOCK - 1) // MOE_BLOCK * MOE_BLOCK
    pad_ends = jnp.cumsum(padded)
    pad_starts = pad_ends - padded
    n_blocks = -(-(n * TOP_K) // MOE_BLOCK) + n_exp
    cap = n_blocks * MOE_BLOCK
    dest = pad_starts[e_l[:, :TOP_K]] + r_l[:, :TOP_K]
    blk_start = jnp.arange(n_blocks, dtype=jnp.int32) * MOE_BLOCK
    block_e = jnp.minimum(jnp.sum((pad_ends[None, :] <= blk_start[:, None]).astype(jnp.int32), axis=1),
                          n_exp - 1)
    n_used = (pad_ends[-1:] // MOE_BLOCK).astype(jnp.int32)

    tm_io = min(256, t)
    dest_tiles = dest.reshape(n // tm_io, 1, tm_io * TOP_K)
    xb = _dispatch_call(dest_tiles, h2, _padfill_call(pad_ends, padded, cap, d), tm=tm_io)
    y = _expert_call(block_e, n_used, xb, w1[0].astype(BF16), b1[0][:, None, :],
                     w2[0].astype(BF16), b2[0][:, None, :])
    return _combine_call(dest_tiles, x1, w_l, g2, final_g[None, :], y, tm=tm_io)
```

```python
import functools

import jax
import jax.numpy as jnp
from jax import lax
from jax.experimental import pallas as pl
from jax.experimental.pallas import tpu as pltpu

F32 = jnp.float32
BF16 = jnp.bfloat16

EPS = 1e-6
N_MOD = 6
S_Q, S_I, S_FF, S_FB, S_OG, S_U, S_V, S_GA, S_GB = range(9)
N_SPLITS = 9
SPLIT_DTYPE = {S_FF: F32, S_FB: F32}
HEAD_DIM = 128
GLA_CHUNK = 64
GLA_HEADS_PER_STEP = 2
MLP_CHUNK = 128
TOP_K = 4
MOE_BLOCK = 256
SWIGLU_LIMIT = 7.0
SWIGLU_ALPHA = 1.702
LANES = 128
NEG_BIG = -1e30
VMEM_LIMIT = 56 * 1024 * 1024


def _cparams(sem):
    return pltpu.CompilerParams(dimension_semantics=sem, vmem_limit_bytes=VMEM_LIMIT)


def _resident(shape):
    zeros = (0,) * len(shape)
    return pl.BlockSpec(shape, lambda *_: zeros, pipeline_mode=pl.Buffered(1))


def _mod_kernel(c_ref, w_ref, b_ref, o_ref):
    c = c_ref[...]
    h = (c * jax.nn.sigmoid(c)).astype(BF16)
    o_ref[...] = jnp.dot(h, w_ref[...].astype(BF16), preferred_element_type=F32) + b_ref[...]


def _mod_call(cc, w_mod, b_mod):
    rows, d = cc.shape
    width = w_mod.shape[1]
    return pl.pallas_call(
        _mod_kernel,
        grid=(width // d,),
        in_specs=[pl.BlockSpec((rows, d), lambda j: (0, 0)),
                  pl.BlockSpec((d, d), lambda j: (0, j)),
                  pl.BlockSpec((1, d), lambda j: (0, j))],
        out_specs=pl.BlockSpec((rows, d), lambda j: (0, j)),
        out_shape=jax.ShapeDtypeStruct((rows, width), F32),
        compiler_params=_cparams(("arbitrary",)),
        name="mod",
    )(cc, w_mod, b_mod)


def _lower_bound(lb_ref):
    a = lb_ref[0:1, :]
    b = lb_ref[1:2, :]
    m = jnp.maximum(a, b)
    ea = jnp.exp(a - m)
    eb = jnp.exp(b - m)
    return ea / (ea + eb)


def _gelu(z):
    return 0.5 * z * (1.0 + lax.erf(z * (2.0 ** -0.5)))


def _split_activation(s, z, lbf_ref, lbb_ref):
    if s in (S_Q, S_OG):
        return z * jax.nn.sigmoid(z)
    if s == S_I:
        return z
    if s in (S_FF, S_FB):
        lb = _lower_bound(lbf_ref if s == S_FF else lbb_ref)
        return lb + (1.0 - lb) * jax.nn.sigmoid(z)
    if s in (S_U, S_V):
        return _gelu(z)
    return jax.nn.sigmoid(z)


def _proj_kernel(x_ref, g_ref, sh_ref, sc_ref, w_ref, lbf_ref, lbb_ref, *o_refs, splits):
    d = x_ref.shape[1]
    x = x_ref[...]
    y = x * lax.rsqrt(jnp.mean(x * x, axis=-1, keepdims=True) + EPS) * g_ref[...]
    h = (y * (1.0 + sc_ref[...]) + sh_ref[...]).astype(BF16)
    for s, o_ref in zip(splits, o_refs):
        z = jnp.dot(h, w_ref[:, s * d:(s + 1) * d], preferred_element_type=F32)
        o_ref[...] = _split_activation(s, z, lbf_ref, lbb_ref).astype(o_ref.dtype)


def _proj_call(x, g, sh, sc, w_bf, lbf, lbb, *, splits, tm, per_batch_mod):
    bsz, t, d = x.shape
    mod_map = (lambda b, i: (b, 0, 0)) if per_batch_mod else (lambda b, i: (0, 0, 0))
    row = pl.BlockSpec((None, tm, d), lambda b, i: (b, i, 0))
    return pl.pallas_call(
        functools.partial(_proj_kernel, splits=splits),
        grid=(bsz, t // tm),
        in_specs=[row,
                  pl.BlockSpec((1, d), lambda b, i: (0, 0)),
                  pl.BlockSpec((None, 1, d), mod_map),
                  pl.BlockSpec((None, 1, d), mod_map),
                  _resident(w_bf.shape),
                  pl.BlockSpec((2, d), lambda b, i: (0, 0)),
                  pl.BlockSpec((2, d), lambda b, i: (0, 0))],
        out_specs=[row for _ in splits],
        out_shape=[jax.ShapeDtypeStruct((bsz, t, d), SPLIT_DTYPE.get(s, BF16)) for s in splits],
        compiler_params=_cparams(("arbitrary", "arbitrary")),
        name="proj",
    )(x, g, sh, sc, w_bf, lbf, lbb)


def _prefix_sum(x, reverse):
    n = x.shape[0]
    row = lax.broadcasted_iota(jnp.int32, x.shape, 0)
    s = 1
    while s < n:
        if reverse:
            x = x + jnp.where(row < n - s, pltpu.roll(x, n - s, 0), 0.0)
        else:
            x = x + jnp.where(row >= s, pltpu.roll(x, s, 0), 0.0)
        s *= 2
    return x


def _dot_nt(a, b):
    return lax.dot_general(a, b, (((1,), (1,)), ((), ())), preferred_element_type=F32)


def _dot_tn(a, b):
    return lax.dot_general(a, b, (((0,), (0,)), ((), ())), preferred_element_type=F32)


def _gla_prepare(q, f, reverse):
    c = q.shape[0]
    b = _prefix_sum(jnp.log(f), reverse)
    if reverse:
        b_mid = b[c // 2:c // 2 + 1, :]
        b_end = b[0:1, :]
    else:
        b_mid = b[c // 2 - 1:c // 2, :]
        b_end = b[c - 1:c, :]
    qe = q * jnp.exp(b - b_mid)
    ke = (1.0 - f) * jnp.exp(b_mid - b)
    kd = ke * jnp.exp(b_end - b_mid)
    return qe.astype(BF16), ke.astype(BF16), kd.astype(BF16), jnp.exp(b_mid), jnp.exp(b_end)


def _gla_scores(qe, ke, st, e_mid):
    rhs = jnp.concatenate([(st * e_mid).astype(BF16), ke], axis=0)
    return _dot_nt(qe, rhs)


def _gla_finish(both, kd, vb, st, e_end, reverse):
    c = both.shape[0]
    dv = st.shape[0]
    ti = lax.broadcasted_iota(jnp.int32, (c, c), 0)
    si = lax.broadcasted_iota(jnp.int32, (c, c), 1)
    keep = (si >= ti) if reverse else (si <= ti)
    scores = jnp.where(keep, both[:, dv:], 0.0)
    o = both[:, :dv] + jnp.dot(scores.astype(BF16), vb, preferred_element_type=F32)
    return o, st * e_end + _dot_tn(vb, kd)


def _context_state(f, vb, reverse_decay):
    g = jnp.log(f)
    tail = _prefix_sum(g, reverse_decay) - g
    kt = ((1.0 - f) * jnp.exp(tail)).astype(BF16)
    return _dot_tn(vb, kt)


def _gla_kernel(q_ref, i_ref, ff_ref, fb_ref, og_ref, ci_ref, cff_ref, cfb_ref, gn_ref,
                o_ref, of_ref, ob_ref, sf_ref, sb_ref, ops_f, ops_b, rows_f, rows_b):
    t = q_ref.shape[0]
    c = GLA_CHUNK
    n = t // c
    heads = [slice(h * HEAD_DIM, (h + 1) * HEAD_DIM) for h in range(q_ref.shape[1] // HEAD_DIM)]
    for h, cols in enumerate(heads):
        ci = ci_ref[:, cols]
        sf_ref[h] = _context_state(cff_ref[:, cols], ci, True)
        sb_ref[h] = _context_state(cfb_ref[:, cols], ci, False)

    def prepare(step, carry):
        r = pl.ds(pl.multiple_of(step * c, c), c)
        for cols in heads:
            q = q_ref[r, cols].astype(F32)
            for f_ref, ops, rows, reverse in ((ff_ref, ops_f, rows_f, False), (fb_ref, ops_b, rows_b, True)):
                qe, ke, kd, e_mid, e_end = _gla_prepare(q, f_ref[r, cols], reverse)
                ops[0, r, cols] = qe
                ops[1, r, cols] = ke
                ops[2, r, cols] = kd
                rows[0, step, :, cols] = jnp.broadcast_to(e_mid, (8, HEAD_DIM))
                rows[1, step, :, cols] = jnp.broadcast_to(e_end, (8, HEAD_DIM))
        return carry

    lax.fori_loop(0, n, prepare, 0)

    def scan(step, carry):
        chains = []
        for h, cols in enumerate(heads):
            for ops, rows, s_ref, acc, reverse in ((ops_f, rows_f, sf_ref, of_ref, False),
                                                   (ops_b, rows_b, sb_ref, ob_ref, True)):
                ck = (n - 1 - step) if reverse else step
                chains.append((h, cols, ops, rows, s_ref, acc, reverse, ck,
                               pl.ds(pl.multiple_of(ck * c, c), c)))
        firsts = []
        for h, cols, ops, rows, s_ref, acc, reverse, ck, r in chains:
            st = s_ref[h]
            firsts.append((st, _gla_scores(ops[0, r, cols], ops[1, r, cols], st, rows[0, ck, 0:1, cols])))
        for (h, cols, ops, rows, s_ref, acc, reverse, ck, r), (st, both) in zip(chains, firsts):
            o, st = _gla_finish(both, ops[2, r, cols], i_ref[r, cols], st, rows[1, ck, 0:1, cols], reverse)
            acc[r, cols] = o
            s_ref[h] = st
        return carry

    lax.fori_loop(0, n, scan, 0)
    for h, cols in enumerate(heads):
        o = of_ref[:, cols] + ob_ref[:, cols]
        y = o * lax.rsqrt(jnp.mean(o * o, axis=-1, keepdims=True) + EPS) * gn_ref[...]
        o_ref[:, cols] = (y * og_ref[:, cols].astype(F32)).astype(o_ref.dtype)


def _gla_call(q, i, ff, fb, og, ci, cff, cfb, gnorm):
    bsz, t, d = q.shape
    width = GLA_HEADS_PER_STEP * HEAD_DIM
    tc = ci.shape[1]
    lat = pl.BlockSpec((None, t, width), lambda b, h: (b, 0, h))
    ctx = pl.BlockSpec((None, tc, width), lambda b, h: (b, 0, h))
    out_acc = pltpu.VMEM((t, width), F32)
    state = pltpu.VMEM((GLA_HEADS_PER_STEP, HEAD_DIM, HEAD_DIM), F32)
    operands = pltpu.VMEM((3, t, width), BF16)
    decay_rows = pltpu.VMEM((2, t // GLA_CHUNK, 8, width), F32)
    return pl.pallas_call(
        _gla_kernel,
        grid=(bsz, d // width),
        in_specs=[lat, lat, lat, lat, lat, ctx, ctx, ctx,
                  pl.BlockSpec((1, HEAD_DIM), lambda b, h: (0, 0))],
        out_specs=lat,
        out_shape=jax.ShapeDtypeStruct((bsz, t, d), BF16),
        scratch_shapes=[out_acc, out_acc, state, state, operands, operands, decay_rows, decay_rows],
        compiler_params=_cparams(("arbitrary", "arbitrary")),
        name="gla",
    )(q, i, ff, fb, og, ci, cff, cfb, gnorm)


def _mix_kernel(x_ref, u_ref, v_ref, ga_ref, gb_ref, ya_ref, ws_ref, bs_ref, wa_ref, wb_ref, wo_ref,
                g1_ref, n2_ref, sh2_ref, sc2_ref, wr_ref, br_ref,
                x1_ref, h2_ref, e_ref, w_ref, rank_ref, cnt_ref, yb_ref, carry_ref):
    tm, d = x_ref.shape
    first = (pl.program_id(0) == 0) & (pl.program_id(1) == 0)

    @pl.when(first)
    def _():
        carry_ref[...] = jnp.zeros_like(carry_ref)

    v = v_ref[...].astype(F32)
    mu = jnp.mean(v, axis=-1, keepdims=True)
    vc = v - mu
    var = jnp.mean(vc * vc, axis=-1, keepdims=True)
    vn = (vc * lax.rsqrt(var + EPS)).astype(BF16)
    for ck in range(tm // MLP_CHUNK):
        rows = slice(ck * MLP_CHUNK, (ck + 1) * MLP_CHUNK)
        for g in range(d // HEAD_DIM):
            cols = slice(g * HEAD_DIM, (g + 1) * HEAD_DIM)
            s = jnp.dot(ws_ref[g], vn[rows, cols], preferred_element_type=F32) + bs_ref[g]
            yb_ref[rows, cols] = (u_ref[rows, cols].astype(F32) * s).astype(BF16)

    pa = jnp.dot(ya_ref[...], wa_ref[...], preferred_element_type=F32)
    pb = jnp.dot(yb_ref[...], wb_ref[...], preferred_element_type=F32)
    merged = ga_ref[...].astype(F32) * pa + gb_ref[...].astype(F32) * pb
    mixed = jnp.dot(merged.astype(BF16), wo_ref[...], preferred_element_type=F32)
    x1 = x_ref[...] + g1_ref[...] * mixed
    x1_ref[...] = x1

    y = x1 * lax.rsqrt(jnp.mean(x1 * x1, axis=-1, keepdims=True) + EPS) * n2_ref[...]
    h2 = y * (1.0 + sc2_ref[...]) + sh2_ref[...]
    h2_ref[...] = h2

    logits = jnp.dot(h2.astype(BF16), wr_ref[...], preferred_element_type=F32) + br_ref[...]
    lane = lax.broadcasted_iota(jnp.int32, (tm, LANES), 1)
    work = logits
    vals, idxs = [], []
    for _ in range(TOP_K):
        m = jnp.max(work, axis=-1, keepdims=True)
        idx = jnp.min(jnp.where(work == m, lane, LANES), axis=-1, keepdims=True)
        vals.append(m)
        idxs.append(idx)
        work = jnp.where(lane == idx, -jnp.inf, work)
    exps = [jnp.exp(m - vals[0]) for m in vals]
    denom = exps[0] + exps[1] + exps[2] + exps[3]

    sel = jnp.zeros((tm, LANES), F32)
    for idx in idxs:
        sel = sel + jnp.where(lane == idx, 1.0, 0.0)
    ri = lax.broadcasted_iota(jnp.int32, (tm, tm), 0)
    cj = lax.broadcasted_iota(jnp.int32, (tm, tm), 1)
    tri = jnp.where(cj < ri, 1.0, 0.0).astype(BF16)
    before = carry_ref[...] + jnp.dot(tri, sel.astype(BF16), preferred_element_type=F32)

    e_out = jnp.zeros((tm, LANES), jnp.int32)
    w_out = jnp.zeros((tm, LANES), F32)
    r_out = jnp.zeros((tm, LANES), F32)
    for k in range(TOP_K):
        rk = jnp.sum(jnp.where(lane == idxs[k], before, 0.0), axis=-1, keepdims=True)
        e_out = jnp.where(lane == k, idxs[k], e_out)
        w_out = jnp.where(lane == k, exps[k] / denom, w_out)
        r_out = jnp.where(lane == k, rk, r_out)
    e_ref[...] = e_out
    w_ref[...] = w_out
    rank_ref[...] = r_out.astype(jnp.int32)
    carry_ref[...] = carry_ref[...] + jnp.sum(sel, axis=0, keepdims=True)
    cnt_ref[...] = carry_ref[...]


def _mix_call(x, u, v, ga, gb, ya, ws_bf, bs_full, wa_bf, wb_bf, wo_bf, g1, n2, sh2, sc2, wr_pad, br_pad, *, tm):
    bsz, t, d = x.shape
    n = bsz * t
    nt = t // tm
    row = pl.BlockSpec((None, tm, d), lambda b, i: (b, i, 0))
    per_batch = pl.BlockSpec((None, 1, d), lambda b, i: (b, 0, 0))

    def flat(width):
        return pl.BlockSpec((tm, width), lambda b, i: (b * nt + i, 0))

    return pl.pallas_call(
        _mix_kernel,
        grid=(bsz, nt),
        in_specs=[row, row, row, row, row, row,
                  _resident(ws_bf.shape), _resident(bs_full.shape),
                  _resident((d, d)), _resident((d, d)), _resident((d, d)),
                  per_batch, _resident((1, d)), per_batch, per_batch,
                  _resident((d, LANES)), _resident((1, LANES))],
        out_specs=[row, flat(d), flat(LANES), flat(LANES), flat(LANES),
                   pl.BlockSpec((1, LANES), lambda b, i: (0, 0))],
        out_shape=[jax.ShapeDtypeStruct((bsz, t, d), F32),
                   jax.ShapeDtypeStruct((n, d), F32),
                   jax.ShapeDtypeStruct((n, LANES), jnp.int32),
                   jax.ShapeDtypeStruct((n, LANES), F32),
                   jax.ShapeDtypeStruct((n, LANES), jnp.int32),
                   jax.ShapeDtypeStruct((1, LANES), F32)],
        scratch_shapes=[pltpu.VMEM((tm, d), BF16), pltpu.VMEM((1, LANES), F32)],
        compiler_params=_cparams(("arbitrary", "arbitrary")),
        name="mix",
    )(x, u, v, ga, gb, ya, ws_bf, bs_full, wa_bf, wb_bf, wo_bf, g1, n2, sh2, sc2, wr_pad, br_pad)


def _padfill_kernel(ends_ref, padded_ref, xb_ref, zero_ref, sem):
    zero_ref[...] = jnp.zeros_like(zero_ref)
    n_exp = ends_ref.shape[0]

    def tail_copy(e):
        start = pl.multiple_of(ends_ref[e] - MOE_BLOCK, MOE_BLOCK)
        return pltpu.make_async_copy(zero_ref, xb_ref.at[pl.ds(start, MOE_BLOCK)], sem)

    for e in range(n_exp):
        @pl.when(padded_ref[e] > 0)
        def _():
            tail_copy(e).start()
    for e in range(n_exp):
        @pl.when(padded_ref[e] > 0)
        def _():
            tail_copy(e).wait()


def _padfill_call(pad_ends, padded, cap, d):
    return pl.pallas_call(
        _padfill_kernel,
        grid_spec=pltpu.PrefetchScalarGridSpec(
            num_scalar_prefetch=2, grid=(1,), in_specs=[],
            out_specs=pl.BlockSpec(memory_space=pl.ANY),
            scratch_shapes=[pltpu.VMEM((MOE_BLOCK, d), F32), pltpu.SemaphoreType.DMA]),
        out_shape=jax.ShapeDtypeStruct((cap, d), F32),
        compiler_params=_cparams(("arbitrary",)),
        name="padfill",
    )(pad_ends, padded)


def _row_copy(src_ref, src_row, dst_ref, dst_row, sem):
    return pltpu.make_async_copy(src_ref.at[pl.ds(src_row, 1)], dst_ref.at[pl.ds(dst_row, 1)], sem)


def _dispatch_kernel(dest_ref, h_ref, xin_ref, xb_ref, sem):
    del xin_ref
    tm = h_ref.shape[0]

    def issue(r, carry):
        for k in range(TOP_K):
            _row_copy(h_ref, r, xb_ref, dest_ref[0, 0, r * TOP_K + k], sem).start()
        return carry

    lax.fori_loop(0, tm, issue, 0)

    def drain(r, carry):
        for k in range(TOP_K):
            _row_copy(h_ref, r, xb_ref, dest_ref[0, 0, r * TOP_K + k], sem).wait()
        return carry

    lax.fori_loop(0, tm, drain, 0)


def _dispatch_call(dest_tiles, h2, xb_init, *, tm):
    n, d = h2.shape
    return pl.pallas_call(
        _dispatch_kernel,
        grid=(n // tm,),
        in_specs=[pl.BlockSpec((1, 1, tm * TOP_K), lambda i: (i, 0, 0), memory_space=pltpu.SMEM),
                  pl.BlockSpec((tm, d), lambda i: (i, 0)),
                  pl.BlockSpec(memory_space=pl.ANY)],
        out_specs=pl.BlockSpec(memory_space=pl.ANY),
        out_shape=jax.ShapeDtypeStruct(xb_init.shape, F32),
        scratch_shapes=[pltpu.SemaphoreType.DMA],
        input_output_aliases={2: 0},
        compiler_params=_cparams(("arbitrary",)),
        name="dispatch",
    )(dest_tiles, h2, xb_init)


def _expert_kernel(be_ref, nu_ref, x_ref, w1_ref, b1_ref, w2_ref, b2_ref, y_ref):
    del be_ref
    f = w2_ref.shape[1]

    @pl.when(pl.program_id(0) < nu_ref[0])
    def _():
        z = jnp.dot(x_ref[...].astype(BF16), w1_ref[0], preferred_element_type=F32) + b1_ref[0]
        gate = jnp.minimum(z[:, :f], SWIGLU_LIMIT)
        lin = jnp.clip(z[:, f:], -SWIGLU_LIMIT, SWIGLU_LIMIT)
        act = gate * jax.nn.sigmoid(SWIGLU_ALPHA * gate) * (lin + 1.0)
        y_ref[...] = jnp.dot(act.astype(BF16), w2_ref[0], preferred_element_type=F32) + b2_ref[0]


def _expert_call(block_e, n_used, xb, w1_bf, b1, w2_bf, b2):
    cap, d = xb.shape
    n_blocks = cap // MOE_BLOCK
    f2 = w1_bf.shape[2]
    f = w2_bf.shape[1]

    def blk(i, be, nu):
        return jnp.minimum(i, nu[0] - 1)

    grid_spec = pltpu.PrefetchScalarGridSpec(
        num_scalar_prefetch=2,
        grid=(n_blocks,),
        in_specs=[pl.BlockSpec((MOE_BLOCK, d), lambda i, be, nu: (blk(i, be, nu), 0)),
                  pl.BlockSpec((1, d, f2), lambda i, be, nu: (be[blk(i, be, nu)], 0, 0)),
                  pl.BlockSpec((1, 1, f2), lambda i, be, nu: (be[blk(i, be, nu)], 0, 0)),
                  pl.BlockSpec((1, f, d), lambda i, be, nu: (be[blk(i, be, nu)], 0, 0)),
                  pl.BlockSpec((1, 1, d), lambda i, be, nu: (be[blk(i, be, nu)], 0, 0))],
        out_specs=pl.BlockSpec((MOE_BLOCK, d), lambda i, be, nu: (blk(i, be, nu), 0)),
    )
    return pl.pallas_call(
        _expert_kernel,
        grid_spec=grid_spec,
        out_shape=jax.ShapeDtypeStruct((cap, d), F32),
        compiler_params=_cparams(("arbitrary",)),
        name="experts",
    )(block_e, n_used, xb, w1_bf, b1, w2_bf, b2)


def _combine_kernel(dest_ref, x1_ref, w_ref, g2_ref, fg_ref, y_ref, o_ref, buf_ref, sem):
    tm = x1_ref.shape[0]

    def issue(r, carry):
        for k in range(TOP_K):
            _row_copy(y_ref, dest_ref[0, 0, r * TOP_K + k], buf_ref.at[k], r, sem).start()
        return carry

    lax.fori_loop(0, tm, issue, 0)

    def drain(r, carry):
        for k in range(TOP_K):
            _row_copy(y_ref, dest_ref[0, 0, r * TOP_K + k], buf_ref.at[k], r, sem).wait()
        return carry

    lax.fori_loop(0, tm, drain, 0)

    w = w_ref[...]
    moe = w[:, 0:1] * buf_ref[0]
    for k in range(1, TOP_K):
        moe = moe + w[:, k:k + 1] * buf_ref[k]
    x2 = x1_ref[...] + g2_ref[...] * moe
    o_ref[...] = x2 * lax.rsqrt(jnp.mean(x2 * x2, axis=-1, keepdims=True) + EPS) * fg_ref[...]


def _combine_call(dest_tiles, x1, w_lanes, g2, final_g, y, *, tm):
    bsz, t, d = x1.shape
    nt = t // tm
    return pl.pallas_call(
        _combine_kernel,
        grid=(bsz, nt),
        in_specs=[pl.BlockSpec((1, 1, tm * TOP_K), lambda b, i: (b * nt + i, 0, 0), memory_space=pltpu.SMEM),
                  pl.BlockSpec((None, tm, d), lambda b, i: (b, i, 0)),
                  pl.BlockSpec((tm, LANES), lambda b, i: (b * nt + i, 0)),
                  pl.BlockSpec((None, 1, d), lambda b, i: (b, 0, 0)),
                  pl.BlockSpec((1, d), lambda b, i: (0, 0)),
                  pl.BlockSpec(memory_space=pl.ANY)],
        out_specs=pl.BlockSpec((None, tm, d), lambda b, i: (b, i, 0)),
        out_shape=jax.ShapeDtypeStruct((bsz, t, d), F32),
        scratch_shapes=[pltpu.VMEM((TOP_K, tm, d), F32), pltpu.SemaphoreType.DMA],
        compiler_params=_cparams(("arbitrary", "arbitrary")),
        name="combine",
    )(dest_tiles, x1, w_lanes, g2, final_g, y)


def kernel(x, c, ctx, c_ctx, norm1_g, norm2_g, w_mod, b_mod, w_in, lb_fwd, lb_bwd, gnorm_g, w_s, b_s,
           w_branch_a, w_branch_b, w_out, w_router, b_router, w1, b1, w2, b2, final_g):
    bsz, t, d = x.shape
    assert w_in.shape[0] == 1 and lb_fwd.shape[0] == 2, "single-layer block"
    assert w_in.shape[2] == N_SPLITS * d and d % HEAD_DIM == 0
    assert t % MLP_CHUNK == 0 and ctx.shape[1] % GLA_CHUNK == 0
    n = bsz * t

    pad_rows = (-(bsz + 1)) % 8
    cc = jnp.concatenate([c, c_ctx[None, :], jnp.zeros((pad_rows, d), F32)], axis=0)
    mod = _mod_call(cc, w_mod[0], b_mod[0][None, :])
    lat = mod[:bsz].reshape(bsz, N_MOD, 1, d)
    sh1, sc1, g1, sh2, sc2, g2 = (lat[:, m] for m in range(N_MOD))
    cmod = mod[bsz].reshape(N_MOD, 1, 1, d)
    csh1, csc1 = cmod[0], cmod[1]

    w_in_bf = w_in[0].astype(BF16)
    n1 = norm1_g[0][None, :]
    q, i_, ff, fb, og, u, v, ga, gb = _proj_call(
        x, n1, sh1, sc1, w_in_bf, lb_fwd, lb_bwd,
        splits=tuple(range(N_SPLITS)), tm=min(256, t), per_batch_mod=True)
    ci, cff, cfb = _proj_call(
        ctx, n1, csh1, csc1, w_in_bf, lb_fwd, lb_bwd,
        splits=(S_I, S_FF, S_FB), tm=ctx.shape[1], per_batch_mod=False)

    ya = _gla_call(q, i_, ff, fb, og, ci, cff, cfb, gnorm_g[0][None, :])

    n_exp = w_router.shape[2]
    wr_pad = jnp.zeros((d, LANES), BF16).at[:, :n_exp].set(w_router[0].astype(BF16))
    br_pad = jnp.full((1, LANES), NEG_BIG, F32).at[0, :n_exp].set(b_router[0])
    bs_full = jnp.broadcast_to(b_s[0][:, :, None], b_s.shape[1:] + (HEAD_DIM,))
    x1, h2, e_l, w_l, r_l, cnt = _mix_call(
        x, u, v, ga, gb, ya, w_s[0].astype(BF16), bs_full,
        w_branch_a[0].astype(BF16), w_branch_b[0].astype(BF16), w_out[0].astype(BF16),
        g1, norm2_g[0][None, :], sh2, sc2, wr_pad, br_pad, tm=min(256, t))

    counts = cnt[0, :n_exp].astype(jnp.int32)
    padded = (counts + MOE_BLOCK - 1) // MOE_BLOCK * MOE_BLOCK
    pad_ends = jnp.cumsum(padded)
    pad_starts = pad_ends - padded
    n_blocks = -(-(n * TOP_K) // MOE_BLOCK) + n_exp
    cap = n_blocks * MOE_BLOCK
    dest = pad_starts[e_l[:, :TOP_K]] + r_l[:, :TOP_K]
    blk_start = jnp.arange(n_blocks, dtype=jnp.int32) * MOE_BLOCK
    block_e = jnp.minimum(jnp.sum((pad_ends[None, :] <= blk_start[:, None]).astype(jnp.int32), axis=1),
                          n_exp - 1)
    n_used = (pad_ends[-1:] // MOE_BLOCK).astype(jnp.int32)

    tm_io = min(256, t)
    dest_tiles = dest.reshape(n // tm_io, 1, tm_io * TOP_K)
    xb = _dispatch_call(dest_tiles, h2, _padfill_call(pad_ends, padded, cap, d), tm=tm_io)
    y = _expert_call(block_e, n_used, xb, w1[0].astype(BF16), b1[0][:, None, :],
                     w2[0].astype(BF16), b2[0][:, None, :])
    return _combine_call(dest_tiles, x1, w_l, g2, final_g[None, :], y, tm=tm_io)
```

```python
import functools

import jax
import jax.numpy as jnp
from jax import lax
from jax.experimental import pallas as pl
from jax.experimental.pallas import tpu as pltpu

F32 = jnp.float32
BF16 = jnp.bfloat16

EPS = 1e-6
N_MOD = 6
S_Q, S_I, S_FF, S_FB, S_OG, S_U, S_V, S_GA, S_GB = range(9)
N_SPLITS = 9
SPLIT_DTYPE = {S_FF: F32, S_FB: F32}
HEAD_DIM = 128
GLA_CHUNK = 64
GLA_HEADS_PER_STEP = 2
MLP_CHUNK = 128
TOP_K = 4
MOE_BLOCK = 256
TRASH_ROWS = 2 * MOE_BLOCK
SWIGLU_LIMIT = 7.0
SWIGLU_ALPHA = 1.702
LANES = 128
NEG_BIG = -1e30
VMEM_LIMIT = 56 * 1024 * 1024


def _cparams(sem):
    return pltpu.CompilerParams(dimension_semantics=sem, vmem_limit_bytes=VMEM_LIMIT)


def _resident(shape):
    zeros = (0,) * len(shape)
    return pl.BlockSpec(shape, lambda *_: zeros, pipeline_mode=pl.Buffered(1))


def _mod_kernel(c_ref, w_ref, b_ref, o_ref):
    c = c_ref[...]
    h = (c * jax.nn.sigmoid(c)).astype(BF16)
    o_ref[...] = jnp.dot(h, w_ref[...].astype(BF16), preferred_element_type=F32) + b_ref[...]


def _mod_call(cc, w_mod, b_mod):
    rows, d = cc.shape
    width = w_mod.shape[1]
    return pl.pallas_call(
        _mod_kernel,
        grid=(width // d,),
        in_specs=[pl.BlockSpec((rows, d), lambda j: (0, 0)),
                  pl.BlockSpec((d, d), lambda j: (0, j)),
                  pl.BlockSpec((1, d), lambda j: (0, j))],
        out_specs=pl.BlockSpec((rows, d), lambda j: (0, j)),
        out_shape=jax.ShapeDtypeStruct((rows, width), F32),
        compiler_params=_cparams(("arbitrary",)),
        name="mod",
    )(cc, w_mod, b_mod)


def _lower_bound(lb_ref):
    a = lb_ref[0:1, :]
    b = lb_ref[1:2, :]
    m = jnp.maximum(a, b)
    ea = jnp.exp(a - m)
    eb = jnp.exp(b - m)
    return ea / (ea + eb)


def _gelu(z):
    return 0.5 * z * (1.0 + lax.erf(z * (2.0 ** -0.5)))


def _split_activation(s, z, lbf_ref, lbb_ref):
    if s in (S_Q, S_OG):
        return z * jax.nn.sigmoid(z)
    if s == S_I:
        return z
    if s in (S_FF, S_FB):
        lb = _lower_bound(lbf_ref if s == S_FF else lbb_ref)
        return lb + (1.0 - lb) * jax.nn.sigmoid(z)
    if s in (S_U, S_V):
        return _gelu(z)
    return jax.nn.sigmoid(z)


def _proj_kernel(x_ref, g_ref, sh_ref, sc_ref, w_ref, lbf_ref, lbb_ref, *o_refs, splits):
    d = x_ref.shape[1]
    x = x_ref[...]
    y = x * lax.rsqrt(jnp.mean(x * x, axis=-1, keepdims=True) + EPS) * g_ref[...]
    h = (y * (1.0 + sc_ref[...]) + sh_ref[...]).astype(BF16)
    for s, o_ref in zip(splits, o_refs):
        z = jnp.dot(h, w_ref[:, s * d:(s + 1) * d], preferred_element_type=F32)
        o_ref[...] = _split_activation(s, z, lbf_ref, lbb_ref).astype(o_ref.dtype)


def _proj_call(x, g, sh, sc, w_bf, lbf, lbb, *, splits, tm, per_batch_mod):
    bsz, t, d = x.shape
    mod_map = (lambda b, i: (b, 0, 0)) if per_batch_mod else (lambda b, i: (0, 0, 0))
    row = pl.BlockSpec((None, tm, d), lambda b, i: (b, i, 0))
    return pl.pallas_call(
        functools.partial(_proj_kernel, splits=splits),
        grid=(bsz, t // tm),
        in_specs=[row,
                  pl.BlockSpec((1, d), lambda b, i: (0, 0)),
                  pl.BlockSpec((None, 1, d), mod_map),
                  pl.BlockSpec((None, 1, d), mod_map),
                  _resident(w_bf.shape),
                  pl.BlockSpec((2, d), lambda b, i: (0, 0)),
                  pl.BlockSpec((2, d), lambda b, i: (0, 0))],
        out_specs=[row for _ in splits],
        out_shape=[jax.ShapeDtypeStruct((bsz, t, d), SPLIT_DTYPE.get(s, BF16)) for s in splits],
        compiler_params=_cparams(("arbitrary", "arbitrary")),
        name="proj",
    )(x, g, sh, sc, w_bf, lbf, lbb)


def _prefix_sum(x, reverse):
    n = x.shape[0]
    row = lax.broadcasted_iota(jnp.int32, x.shape, 0)
    s = 1
    while s < n:
        if reverse:
            x = x + jnp.where(row < n - s, pltpu.roll(x, n - s, 0), 0.0)
        else:
            x = x + jnp.where(row >= s, pltpu.roll(x, s, 0), 0.0)
        s *= 2
    return x


def _dot_nt(a, b):
    return lax.dot_general(a, b, (((1,), (1,)), ((), ())), preferred_element_type=F32)


def _dot_tn(a, b):
    return lax.dot_general(a, b, (((0,), (0,)), ((), ())), preferred_element_type=F32)


def _gla_prepare(q, f, reverse):
    c = q.shape[0]
    b = _prefix_sum(jnp.log(f), reverse)
    if reverse:
        b_mid = b[c // 2:c // 2 + 1, :]
        b_end = b[0:1, :]
    else:
        b_mid = b[c // 2 - 1:c // 2, :]
        b_end = b[c - 1:c, :]
    qe = q * jnp.exp(b - b_mid)
    ke = (1.0 - f) * jnp.exp(b_mid - b)
    kd = ke * jnp.exp(b_end - b_mid)
    return qe.astype(BF16), ke.astype(BF16), kd.astype(BF16), jnp.exp(b_mid), jnp.exp(b_end)


def _gla_scores(qe, ke, st, e_mid):
    rhs = jnp.concatenate([(st * e_mid).astype(BF16), ke], axis=0)
    return _dot_nt(qe, rhs)


def _gla_finish(both, kd, vb, st, e_end, reverse):
    c = both.shape[0]
    dv = st.shape[0]
    ti = lax.broadcasted_iota(jnp.int32, (c, c), 0)
    si = lax.broadcasted_iota(jnp.int32, (c, c), 1)
    keep = (si >= ti) if reverse else (si <= ti)
    scores = jnp.where(keep, both[:, dv:], 0.0)
    o = both[:, :dv] + jnp.dot(scores.astype(BF16), vb, preferred_element_type=F32)
    return o, st * e_end + _dot_tn(vb, kd)


def _context_state(f, vb, reverse_decay):
    g = jnp.log(f)
    tail = _prefix_sum(g, reverse_decay) - g
    kt = ((1.0 - f) * jnp.exp(tail)).astype(BF16)
    return _dot_tn(vb, kt)


def _gla_kernel(q_ref, i_ref, ff_ref, fb_ref, og_ref, ci_ref, cff_ref, cfb_ref, gn_ref,
                o_ref, of_ref, ob_ref, sf_ref, sb_ref, ops_f, ops_b, rows_f, rows_b):
    t = q_ref.shape[0]
    c = GLA_CHUNK
    n = t // c
    heads = [slice(h * HEAD_DIM, (h + 1) * HEAD_DIM) for h in range(q_ref.shape[1] // HEAD_DIM)]
    for h, cols in enumerate(heads):
        ci = ci_ref[:, cols]
        sf_ref[h] = _context_state(cff_ref[:, cols], ci, True)
        sb_ref[h] = _context_state(cfb_ref[:, cols], ci, False)

    def prepare(step, carry):
        r = pl.ds(pl.multiple_of(step * c, c), c)
        for cols in heads:
            q = q_ref[r, cols].astype(F32)
            for f_ref, ops, rows, reverse in ((ff_ref, ops_f, rows_f, False), (fb_ref, ops_b, rows_b, True)):
                qe, ke, kd, e_mid, e_end = _gla_prepare(q, f_ref[r, cols], reverse)
                ops[0, r, cols] = qe
                ops[1, r, cols] = ke
                ops[2, r, cols] = kd
                rows[0, step, :, cols] = jnp.broadcast_to(e_mid, (8, HEAD_DIM))
                rows[1, step, :, cols] = jnp.broadcast_to(e_end, (8, HEAD_DIM))
        return carry

    lax.fori_loop(0, n, prepare, 0)

    def scan(step, carry):
        chains = []
        for h, cols in enumerate(heads):
            for ops, rows, s_ref, acc, reverse in ((ops_f, rows_f, sf_ref, of_ref, False),
                                                   (ops_b, rows_b, sb_ref, ob_ref, True)):
                ck = (n - 1 - step) if reverse else step
                chains.append((h, cols, ops, rows, s_ref, acc, reverse, ck,
                               pl.ds(pl.multiple_of(ck * c, c), c)))
        firsts = []
        for h, cols, ops, rows, s_ref, acc, reverse, ck, r in chains:
            st = s_ref[h]
            firsts.append((st, _gla_scores(ops[0, r, cols], ops[1, r, cols], st, rows[0, ck, 0:1, cols])))
        for (h, cols, ops, rows, s_ref, acc, reverse, ck, r), (st, both) in zip(chains, firsts):
            o, st = _gla_finish(both, ops[2, r, cols], i_ref[r, cols], st, rows[1, ck, 0:1, cols], reverse)
            acc[r, cols] = o
            s_ref[h] = st
        return carry

    lax.fori_loop(0, n, scan, 0)
    for h, cols in enumerate(heads):
        o = of_ref[:, cols] + ob_ref[:, cols]
        y = o * lax.rsqrt(jnp.mean(o * o, axis=-1, keepdims=True) + EPS) * gn_ref[...]
        o_ref[:, cols] = (y * og_ref[:, cols].astype(F32)).astype(o_ref.dtype)


def _gla_call(q, i, ff, fb, og, ci, cff, cfb, gnorm):
    bsz, t, d = q.shape
    width = GLA_HEADS_PER_STEP * HEAD_DIM
    tc = ci.shape[1]
    lat = pl.BlockSpec((None, t, width), lambda b, h: (b, 0, h))
    ctx = pl.BlockSpec((None, tc, width), lambda b, h: (b, 0, h))
    out_acc = pltpu.VMEM((t, width), F32)
    state = pltpu.VMEM((GLA_HEADS_PER_STEP, HEAD_DIM, HEAD_DIM), F32)
    operands = pltpu.VMEM((3, t, width), BF16)
    decay_rows = pltpu.VMEM((2, t // GLA_CHUNK, 8, width), F32)
    return pl.pallas_call(
        _gla_kernel,
        grid=(bsz, d // width),
        in_specs=[lat, lat, lat, lat, lat, ctx, ctx, ctx,
                  pl.BlockSpec((1, HEAD_DIM), lambda b, h: (0, 0))],
        out_specs=lat,
        out_shape=jax.ShapeDtypeStruct((bsz, t, d), BF16),
        scratch_shapes=[out_acc, out_acc, state, state, operands, operands, decay_rows, decay_rows],
        compiler_params=_cparams(("arbitrary", "arbitrary")),
        name="gla",
    )(q, i, ff, fb, og, ci, cff, cfb, gnorm)


def _mix_kernel(x_ref, u_ref, v_ref, ga_ref, gb_ref, ya_ref, ws_ref, bs_ref, wa_ref, wb_ref, wo_ref,
                g1_ref, n2_ref, sh2_ref, sc2_ref, wr_ref, br_ref,
                x1_ref, h2_ref, e_ref, w_ref, rank_ref, cnt_ref, yb_ref, carry_ref):
    tm, d = x_ref.shape
    first = (pl.program_id(0) == 0) & (pl.program_id(1) == 0)

    @pl.when(first)
    def _():
        carry_ref[...] = jnp.zeros_like(carry_ref)

    v = v_ref[...].astype(F32)
    mu = jnp.mean(v, axis=-1, keepdims=True)
    vc = v - mu
    var = jnp.mean(vc * vc, axis=-1, keepdims=True)
    vn = (vc * lax.rsqrt(var + EPS)).astype(BF16)
    for ck in range(tm // MLP_CHUNK):
        rows = slice(ck * MLP_CHUNK, (ck + 1) * MLP_CHUNK)
        for g in range(d // HEAD_DIM):
            cols = slice(g * HEAD_DIM, (g + 1) * HEAD_DIM)
            s = jnp.dot(ws_ref[g], vn[rows, cols], preferred_element_type=F32) + bs_ref[g]
            yb_ref[rows, cols] = (u_ref[rows, cols].astype(F32) * s).astype(BF16)

    pa = jnp.dot(ya_ref[...], wa_ref[...], preferred_element_type=F32)
    pb = jnp.dot(yb_ref[...], wb_ref[...], preferred_element_type=F32)
    merged = ga_ref[...].astype(F32) * pa + gb_ref[...].astype(F32) * pb
    mixed = jnp.dot(merged.astype(BF16), wo_ref[...], preferred_element_type=F32)
    x1 = x_ref[...] + g1_ref[...] * mixed
    x1_ref[...] = x1

    y = x1 * lax.rsqrt(jnp.mean(x1 * x1, axis=-1, keepdims=True) + EPS) * n2_ref[...]
    h2 = y * (1.0 + sc2_ref[...]) + sh2_ref[...]
    h2_ref[...] = h2

    logits = jnp.dot(h2.astype(BF16), wr_ref[...], preferred_element_type=F32) + br_ref[...]
    lane = lax.broadcasted_iota(jnp.int32, (tm, LANES), 1)
    work = logits
    vals, idxs = [], []
    for _ in range(TOP_K):
        m = jnp.max(work, axis=-1, keepdims=True)
        idx = jnp.min(jnp.where(work == m, lane, LANES), axis=-1, keepdims=True)
        vals.append(m)
        idxs.append(idx)
        work = jnp.where(lane == idx, -jnp.inf, work)
    exps = [jnp.exp(m - vals[0]) for m in vals]
    denom = exps[0] + exps[1] + exps[2] + exps[3]

    sel = jnp.zeros((tm, LANES), F32)
    for idx in idxs:
        sel = sel + jnp.where(lane == idx, 1.0, 0.0)
    ri = lax.broadcasted_iota(jnp.int32, (tm, tm), 0)
    cj = lax.broadcasted_iota(jnp.int32, (tm, tm), 1)
    tri = jnp.where(cj < ri, 1.0, 0.0).astype(BF16)
    before = carry_ref[...] + jnp.dot(tri, sel.astype(BF16), preferred_element_type=F32)

    e_out = jnp.zeros((tm, LANES), jnp.int32)
    w_out = jnp.zeros((tm, LANES), F32)
    r_out = jnp.zeros((tm, LANES), F32)
    for k in range(TOP_K):
        rk = jnp.sum(jnp.where(lane == idxs[k], before, 0.0), axis=-1, keepdims=True)
        e_out = jnp.where(lane == k, idxs[k], e_out)
        w_out = jnp.where(lane == k, exps[k] / denom, w_out)
        r_out = jnp.where(lane == k, rk, r_out)
    e_ref[...] = e_out
    w_ref[...] = w_out
    rank_ref[...] = r_out.astype(jnp.int32)
    carry_ref[...] = carry_ref[...] + jnp.sum(sel, axis=0, keepdims=True)
    cnt_ref[...] = carry_ref[...]


def _mix_call(x, u, v, ga, gb, ya, ws_bf, bs_full, wa_bf, wb_bf, wo_bf, g1, n2, sh2, sc2, wr_pad, br_pad, *, tm):
    bsz, t, d = x.shape
    n = bsz * t
    nt = t // tm
    row = pl.BlockSpec((None, tm, d), lambda b, i: (b, i, 0))
    per_batch = pl.BlockSpec((None, 1, d), lambda b, i: (b, 0, 0))

    def flat(width):
        return pl.BlockSpec((tm, width), lambda b, i: (b * nt + i, 0))

    return pl.pallas_call(
        _mix_kernel,
        grid=(bsz, nt),
        in_specs=[row, row, row, row, row, row,
                  _resident(ws_bf.shape), _resident(bs_full.shape),
                  _resident((d, d)), _resident((d, d)), _resident((d, d)),
                  per_batch, _resident((1, d)), per_batch, per_batch,
                  _resident((d, LANES)), _resident((1, LANES))],
        out_specs=[row, flat(d), flat(LANES), flat(LANES), flat(LANES),
                   pl.BlockSpec((1, LANES), lambda b, i: (0, 0))],
        out_shape=[jax.ShapeDtypeStruct((bsz, t, d), F32),
                   jax.ShapeDtypeStruct((n, d), F32),
                   jax.ShapeDtypeStruct((n, LANES), jnp.int32),
                   jax.ShapeDtypeStruct((n, LANES), F32),
                   jax.ShapeDtypeStruct((n, LANES), jnp.int32),
                   jax.ShapeDtypeStruct((1, LANES), F32)],
        scratch_shapes=[pltpu.VMEM((tm, d), BF16), pltpu.VMEM((1, LANES), F32)],
        compiler_params=_cparams(("arbitrary", "arbitrary")),
        name="mix",
    )(x, u, v, ga, gb, ya, ws_bf, bs_full, wa_bf, wb_bf, wo_bf, g1, n2, sh2, sc2, wr_pad, br_pad)


def _slot_table(dest, n_slots, n_assign):
    filler = n_assign + jnp.arange(n_slots, dtype=jnp.int32) % TRASH_ROWS
    ids = jnp.arange(n_assign, dtype=jnp.int32)
    return filler.at[dest.T.reshape(-1)].set(ids)


def _expert_kernel(be_ref, nu_ref, nxt_ref, cur_ref, prv_ref, h_ref, w1_ref, b1_ref, w2_ref, b2_ref,
                   yg_ref, xbuf0, xbuf1, ybuf0, ybuf1, gsem, ssem):
    del be_ref
    i = pl.program_id(0)
    n_used = nu_ref[0]
    f = w2_ref.shape[1]
    xbufs, ybufs = (xbuf0, xbuf1), (ybuf0, ybuf1)
    rows = xbuf0.shape[0]

    def gather(tab_ref, r, slot):
        return pltpu.make_async_copy(h_ref.at[pl.ds(tab_ref[0, 0, r], 1)], xbufs[slot].at[pl.ds(r, 1)],
                                     gsem.at[slot])

    def scatter(tab_ref, r, slot):
        return pltpu.make_async_copy(ybufs[slot].at[pl.ds(r, 1)], yg_ref.at[pl.ds(tab_ref[0, 0, r], 1)],
                                     ssem.at[slot])

    def for_rows(fn):
        def body(r, carry):
            fn(r)
            return carry
        lax.fori_loop(0, rows, body, 0)

    @pl.when(i == 0)
    def _():
        for_rows(lambda r: gather(cur_ref, r, 0).start())
        ybuf0[...] = jnp.zeros_like(ybuf0)
        ybuf1[...] = jnp.zeros_like(ybuf1)

    for cur in (0, 1):
        oth = 1 - cur

        @pl.when((i % 2 == cur) & (i <= n_used))
        def _():
            for_rows(lambda r: gather(cur_ref, r, cur).wait())

        @pl.when((i % 2 == cur) & (i >= 1) & (i <= n_used))
        def _():
            for_rows(lambda r: scatter(prv_ref, r, cur).wait())

        @pl.when((i % 2 == cur) & (i < n_used))
        def _():
            for r in range(rows):
                gather(nxt_ref, r, oth).start()
                scatter(prv_ref, r, oth).start()
            z = jnp.dot(xbufs[cur][...].astype(BF16), w1_ref[0], preferred_element_type=F32) + b1_ref[0]
            gate = jnp.minimum(z[:, :f], SWIGLU_LIMIT)
            lin = jnp.clip(z[:, f:], -SWIGLU_LIMIT, SWIGLU_LIMIT)
            act = gate * jax.nn.sigmoid(SWIGLU_ALPHA * gate) * (lin + 1.0)
            ybufs[cur][...] = jnp.dot(act.astype(BF16), w2_ref[0], preferred_element_type=F32) + b2_ref[0]

        @pl.when((i % 2 == cur) & (i == n_used))
        def _():
            for_rows(lambda r: scatter(prv_ref, r, oth).start())
            for_rows(lambda r: scatter(prv_ref, r, oth).wait())


def _expert_call(block_e, n_used, slot_tok, slot_row, h2, w1_bf, b1, w2_bf, b2, n_out_rows):
    d = h2.shape[1]
    n_blocks = slot_row.shape[0] - 1
    f2 = w1_bf.shape[2]
    f = w2_bf.shape[1]

    def wmap(i, be, nu):
        return (be[jnp.minimum(i, nu[0] - 1)], 0, 0)

    def tab(index):
        return pl.BlockSpec((1, 1, MOE_BLOCK), lambda i, be, nu: (index(i), 0, 0), memory_space=pltpu.SMEM)

    grid_spec = pltpu.PrefetchScalarGridSpec(
        num_scalar_prefetch=2,
        grid=(n_blocks + 1,),
        in_specs=[tab(lambda i: jnp.minimum(i + 1, n_blocks - 1)),
                  tab(lambda i: jnp.minimum(i, n_blocks - 1)),
                  tab(lambda i: jnp.where(i == 0, n_blocks, i - 1)),
                  pl.BlockSpec(memory_space=pl.ANY),
                  pl.BlockSpec((1, d, f2), wmap), pl.BlockSpec((1, 1, f2), wmap),
                  pl.BlockSpec((1, f, d), wmap), pl.BlockSpec((1, 1, d), wmap)],
        out_specs=pl.BlockSpec(memory_space=pl.ANY),
        scratch_shapes=[pltpu.VMEM((MOE_BLOCK, d), F32)] * 4
        + [pltpu.SemaphoreType.DMA((2,)), pltpu.SemaphoreType.DMA((2,))],
    )
    return pl.pallas_call(
        _expert_kernel,
        grid_spec=grid_spec,
        out_shape=jax.ShapeDtypeStruct((n_out_rows, d), F32),
        compiler_params=_cparams(("arbitrary",)),
        name="experts",
    )(block_e, n_used, slot_tok, slot_tok, slot_row, h2, w1_bf, b1, w2_bf, b2)


def _combine_kernel(x1_ref, w_ref, g2_ref, fg_ref, y0_ref, y1_ref, y2_ref, y3_ref, o_ref):
    w = w_ref[...]
    moe = w[:, 0:1] * y0_ref[...]
    for k, y_ref in enumerate((y1_ref, y2_ref, y3_ref), start=1):
        moe = moe + w[:, k:k + 1] * y_ref[...]
    x2 = x1_ref[...] + g2_ref[...] * moe
    o_ref[...] = x2 * lax.rsqrt(jnp.mean(x2 * x2, axis=-1, keepdims=True) + EPS) * fg_ref[...]


def _combine_call(x1, w_lanes, g2, final_g, yg, *, tm):
    bsz, t, d = x1.shape
    nt = t // tm
    tiles = bsz * nt

    def expert_rows(k):
        return pl.BlockSpec((tm, d), lambda b, i: (k * tiles + b * nt + i, 0))

    return pl.pallas_call(
        _combine_kernel,
        grid=(bsz, nt),
        in_specs=[pl.BlockSpec((None, tm, d), lambda b, i: (b, i, 0)),
                  pl.BlockSpec((tm, LANES), lambda b, i: (b * nt + i, 0)),
                  pl.BlockSpec((None, 1, d), lambda b, i: (b, 0, 0)),
                  pl.BlockSpec((1, d), lambda b, i: (0, 0)),
                  expert_rows(0), expert_rows(1), expert_rows(2), expert_rows(3)],
        out_specs=pl.BlockSpec((None, tm, d), lambda b, i: (b, i, 0)),
        out_shape=jax.ShapeDtypeStruct((bsz, t, d), F32),
        compiler_params=_cparams(("arbitrary", "arbitrary")),
        name="combine",
    )(x1, w_lanes, g2, final_g, yg, yg, yg, yg)


def kernel(x, c, ctx, c_ctx, norm1_g, norm2_g, w_mod, b_mod, w_in, lb_fwd, lb_bwd, gnorm_g, w_s, b_s,
           w_branch_a, w_branch_b, w_out, w_router, b_router, w1, b1, w2, b2, final_g):
    bsz, t, d = x.shape
    assert w_in.shape[0] == 1 and lb_fwd.shape[0] == 2, "single-layer block"
    assert w_in.shape[2] == N_SPLITS * d and d % HEAD_DIM == 0
    assert t % MLP_CHUNK == 0 and ctx.shape[1] % GLA_CHUNK == 0
    n = bsz * t

    pad_rows = (-(bsz + 1)) % 8
    cc = jnp.concatenate([c, c_ctx[None, :], jnp.zeros((pad_rows, d), F32)], axis=0)
    mod = _mod_call(cc, w_mod[0], b_mod[0][None, :])
    lat = mod[:bsz].reshape(bsz, N_MOD, 1, d)
    sh1, sc1, g1, sh2, sc2, g2 = (lat[:, m] for m in range(N_MOD))
    cmod = mod[bsz].reshape(N_MOD, 1, 1, d)
    csh1, csc1 = cmod[0], cmod[1]

    w_in_bf = w_in[0].astype(BF16)
    n1 = norm1_g[0][None, :]
    q, i_, ff, fb, og, u, v, ga, gb = _proj_call(
        x, n1, sh1, sc1, w_in_bf, lb_fwd, lb_bwd,
        splits=tuple(range(N_SPLITS)), tm=min(256, t), per_batch_mod=True)
    ci, cff, cfb = _proj_call(
        ctx, n1, csh1, csc1, w_in_bf, lb_fwd, lb_bwd,
        splits=(S_I, S_FF, S_FB), tm=ctx.shape[1], per_batch_mod=False)

    ya = _gla_call(q, i_, ff, fb, og, ci, cff, cfb, gnorm_g[0][None, :])

    n_exp = w_router.shape[2]
    wr_pad = jnp.zeros((d, LANES), BF16).at[:, :n_exp].set(w_router[0].astype(BF16))
    br_pad = jnp.full((1, LANES), NEG_BIG, F32).at[0, :n_exp].set(b_router[0])
    bs_full = jnp.broadcast_to(b_s[0][:, :, None], b_s.shape[1:] + (HEAD_DIM,))
    x1, h2, e_l, w_l, r_l, cnt = _mix_call(
        x, u, v, ga, gb, ya, w_s[0].astype(BF16), bs_full,
        w_branch_a[0].astype(BF16), w_branch_b[0].astype(BF16), w_out[0].astype(BF16),
        g1, norm2_g[0][None, :], sh2, sc2, wr_pad, br_pad, tm=min(256, t))

    counts = cnt[0, :n_exp].astype(jnp.int32)
    padded = (counts + MOE_BLOCK - 1) // MOE_BLOCK * MOE_BLOCK
    pad_ends = jnp.cumsum(padded)
    pad_starts = pad_ends - padded
    n_assign = n * TOP_K
    n_blocks = -(-n_assign // MOE_BLOCK) + n_exp
    dest = pad_starts[e_l[:, :TOP_K]] + r_l[:, :TOP_K]
    blk_start = jnp.arange(n_blocks, dtype=jnp.int32) * MOE_BLOCK
    block_e = jnp.minimum(jnp.sum((pad_ends[None, :] <= blk_start[:, None]).astype(jnp.int32), axis=1),
                          n_exp - 1)
    n_used = (pad_ends[-1:] // MOE_BLOCK).astype(jnp.int32)

    slot_row = _slot_table(dest, (n_blocks + 1) * MOE_BLOCK, n_assign).reshape(n_blocks + 1, 1, MOE_BLOCK)
    yg = _expert_call(block_e, n_used, slot_row % n, slot_row, h2, w1[0].astype(BF16), b1[0][:, None, :],
                      w2[0].astype(BF16), b2[0][:, None, :], n_assign + TRASH_ROWS)
    return _combine_call(x1, w_l, g2, final_g[None, :], yg, tm=min(256, t))
```

```python
import functools

import jax
import jax.numpy as jnp
from jax import lax
from jax.experimental import pallas as pl
from jax.experimental.pallas import tpu as pltpu

F32 = jnp.float32
BF16 = jnp.bfloat16

EPS = 1e-6
N_MOD = 6
S_Q, S_I, S_FF, S_FB, S_OG, S_U, S_V, S_GA, S_GB = range(9)
N_SPLITS = 9
SPLIT_DTYPE = {S_FF: F32, S_FB: F32}
HEAD_DIM = 128
GLA_CHUNK = 64
GLA_HEADS_PER_STEP = 2
MLP_CHUNK = 128
TOP_K = 4
MOE_BLOCK = 256
ROW_GROUP = 8
SWIGLU_LIMIT = 7.0
SWIGLU_ALPHA = 1.702
LANES = 128
NEG_BIG = -1e30
VMEM_LIMIT = 56 * 1024 * 1024


def _cparams(sem):
    return pltpu.CompilerParams(dimension_semantics=sem, vmem_limit_bytes=VMEM_LIMIT)


def _resident(shape):
    zeros = (0,) * len(shape)
    return pl.BlockSpec(shape, lambda *_: zeros, pipeline_mode=pl.Buffered(1))


def _mod_kernel(c_ref, w_ref, b_ref, o_ref):
    c = c_ref[...]
    h = (c * jax.nn.sigmoid(c)).astype(BF16)
    o_ref[...] = jnp.dot(h, w_ref[...].astype(BF16), preferred_element_type=F32) + b_ref[...]


def _mod_call(cc, w_mod, b_mod):
    rows, d = cc.shape
    width = w_mod.shape[1]
    return pl.pallas_call(
        _mod_kernel,
        grid=(width // d,),
        in_specs=[pl.BlockSpec((rows, d), lambda j: (0, 0)),
                  pl.BlockSpec((d, d), lambda j: (0, j)),
                  pl.BlockSpec((1, d), lambda j: (0, j))],
        out_specs=pl.BlockSpec((rows, d), lambda j: (0, j)),
        out_shape=jax.ShapeDtypeStruct((rows, width), F32),
        compiler_params=_cparams(("arbitrary",)),
        name="mod",
    )(cc, w_mod, b_mod)


def _lower_bound(lb_ref):
    a = lb_ref[0:1, :]
    b = lb_ref[1:2, :]
    m = jnp.maximum(a, b)
    ea = jnp.exp(a - m)
    eb = jnp.exp(b - m)
    return ea / (ea + eb)


def _gelu(z):
    return 0.5 * z * (1.0 + lax.erf(z * (2.0 ** -0.5)))


def _split_activation(s, z, lbf_ref, lbb_ref):
    if s in (S_Q, S_OG):
        return z * jax.nn.sigmoid(z)
    if s == S_I:
        return z
    if s in (S_FF, S_FB):
        lb = _lower_bound(lbf_ref if s == S_FF else lbb_ref)
        return lb + (1.0 - lb) * jax.nn.sigmoid(z)
    if s in (S_U, S_V):
        return _gelu(z)
    return jax.nn.sigmoid(z)


def _proj_kernel(x_ref, g_ref, sh_ref, sc_ref, w_ref, lbf_ref, lbb_ref, *o_refs, splits):
    d = x_ref.shape[1]
    x = x_ref[...]
    y = x * lax.rsqrt(jnp.mean(x * x, axis=-1, keepdims=True) + EPS) * g_ref[...]
    h = (y * (1.0 + sc_ref[...]) + sh_ref[...]).astype(BF16)
    for s, o_ref in zip(splits, o_refs):
        z = jnp.dot(h, w_ref[:, s * d:(s + 1) * d], preferred_element_type=F32)
        o_ref[...] = _split_activation(s, z, lbf_ref, lbb_ref).astype(o_ref.dtype)


def _proj_call(x, g, sh, sc, w_bf, lbf, lbb, *, splits, tm, per_batch_mod):
    bsz, t, d = x.shape
    mod_map = (lambda b, i: (b, 0, 0)) if per_batch_mod else (lambda b, i: (0, 0, 0))
    row = pl.BlockSpec((None, tm, d), lambda b, i: (b, i, 0))
    return pl.pallas_call(
        functools.partial(_proj_kernel, splits=splits),
        grid=(bsz, t // tm),
        in_specs=[row,
                  pl.BlockSpec((1, d), lambda b, i: (0, 0)),
                  pl.BlockSpec((None, 1, d), mod_map),
                  pl.BlockSpec((None, 1, d), mod_map),
                  _resident(w_bf.shape),
                  pl.BlockSpec((2, d), lambda b, i: (0, 0)),
                  pl.BlockSpec((2, d), lambda b, i: (0, 0))],
        out_specs=[row for _ in splits],
        out_shape=[jax.ShapeDtypeStruct((bsz, t, d), SPLIT_DTYPE.get(s, BF16)) for s in splits],
        compiler_params=_cparams(("arbitrary", "arbitrary")),
        name="proj",
    )(x, g, sh, sc, w_bf, lbf, lbb)


def _prefix_sum(x, reverse):
    n = x.shape[0]
    row = lax.broadcasted_iota(jnp.int32, x.shape, 0)
    s = 1
    while s < n:
        if reverse:
            x = x + jnp.where(row < n - s, pltpu.roll(x, n - s, 0), 0.0)
        else:
            x = x + jnp.where(row >= s, pltpu.roll(x, s, 0), 0.0)
        s *= 2
    return x


def _dot_nt(a, b):
    return lax.dot_general(a, b, (((1,), (1,)), ((), ())), preferred_element_type=F32)


def _dot_tn(a, b):
    return lax.dot_general(a, b, (((0,), (0,)), ((), ())), preferred_element_type=F32)


def _gla_prepare(q, f, reverse):
    c = q.shape[0]
    b = _prefix_sum(jnp.log(f), reverse)
    if reverse:
        b_mid = b[c // 2:c // 2 + 1, :]
        b_end = b[0:1, :]
    else:
        b_mid = b[c // 2 - 1:c // 2, :]
        b_end = b[c - 1:c, :]
    qe = q * jnp.exp(b - b_mid)
    ke = (1.0 - f) * jnp.exp(b_mid - b)
    kd = ke * jnp.exp(b_end - b_mid)
    return qe.astype(BF16), ke.astype(BF16), kd.astype(BF16), jnp.exp(b_mid), jnp.exp(b_end)


def _gla_scores(qe, ke, st, e_mid):
    rhs = jnp.concatenate([(st * e_mid).astype(BF16), ke], axis=0)
    return _dot_nt(qe, rhs)


def _gla_finish(both, kd, vb, st, e_end, reverse):
    c = both.shape[0]
    dv = st.shape[0]
    ti = lax.broadcasted_iota(jnp.int32, (c, c), 0)
    si = lax.broadcasted_iota(jnp.int32, (c, c), 1)
    keep = (si >= ti) if reverse else (si <= ti)
    scores = jnp.where(keep, both[:, dv:], 0.0)
    o = both[:, :dv] + jnp.dot(scores.astype(BF16), vb, preferred_element_type=F32)
    return o, st * e_end + _dot_tn(vb, kd)


def _context_state(f, vb, reverse_decay):
    g = jnp.log(f)
    tail = _prefix_sum(g, reverse_decay) - g
    kt = ((1.0 - f) * jnp.exp(tail)).astype(BF16)
    return _dot_tn(vb, kt)


def _gla_kernel(q_ref, i_ref, ff_ref, fb_ref, og_ref, ci_ref, cff_ref, cfb_ref, gn_ref,
                o_ref, of_ref, ob_ref, sf_ref, sb_ref, ops_f, ops_b, rows_f, rows_b):
    t = q_ref.shape[0]
    c = GLA_CHUNK
    n = t // c
    heads = [slice(h * HEAD_DIM, (h + 1) * HEAD_DIM) for h in range(q_ref.shape[1] // HEAD_DIM)]
    for h, cols in enumerate(heads):
        ci = ci_ref[:, cols]
        sf_ref[h] = _context_state(cff_ref[:, cols], ci, True)
        sb_ref[h] = _context_state(cfb_ref[:, cols], ci, False)

    def prepare(step, carry):
        r = pl.ds(pl.multiple_of(step * c, c), c)
        for cols in heads:
            q = q_ref[r, cols].astype(F32)
            for f_ref, ops, rows, reverse in ((ff_ref, ops_f, rows_f, False), (fb_ref, ops_b, rows_b, True)):
                qe, ke, kd, e_mid, e_end = _gla_prepare(q, f_ref[r, cols], reverse)
                ops[0, r, cols] = qe
                ops[1, r, cols] = ke
                ops[2, r, cols] = kd
                rows[0, step, :, cols] = jnp.broadcast_to(e_mid, (8, HEAD_DIM))
                rows[1, step, :, cols] = jnp.broadcast_to(e_end, (8, HEAD_DIM))
        return carry

    lax.fori_loop(0, n, prepare, 0)

    def scan(step, carry):
        chains = []
        for h, cols in enumerate(heads):
            for ops, rows, s_ref, acc, reverse in ((ops_f, rows_f, sf_ref, of_ref, False),
                                                   (ops_b, rows_b, sb_ref, ob_ref, True)):
                ck = (n - 1 - step) if reverse else step
                chains.append((h, cols, ops, rows, s_ref, acc, reverse, ck,
                               pl.ds(pl.multiple_of(ck * c, c), c)))
        firsts = []
        for h, cols, ops, rows, s_ref, acc, reverse, ck, r in chains:
            st = s_ref[h]
            firsts.append((st, _gla_scores(ops[0, r, cols], ops[1, r, cols], st, rows[0, ck, 0:1, cols])))
        for (h, cols, ops, rows, s_ref, acc, reverse, ck, r), (st, both) in zip(chains, firsts):
            o, st = _gla_finish(both, ops[2, r, cols], i_ref[r, cols], st, rows[1, ck, 0:1, cols], reverse)
            acc[r, cols] = o
            s_ref[h] = st
        return carry

    lax.fori_loop(0, n, scan, 0)
    for h, cols in enumerate(heads):
        o = of_ref[:, cols] + ob_ref[:, cols]
        y = o * lax.rsqrt(jnp.mean(o * o, axis=-1, keepdims=True) + EPS) * gn_ref[...]
        o_ref[:, cols] = (y * og_ref[:, cols].astype(F32)).astype(o_ref.dtype)


def _gla_call(q, i, ff, fb, og, ci, cff, cfb, gnorm):
    bsz, t, d = q.shape
    width = GLA_HEADS_PER_STEP * HEAD_DIM
    tc = ci.shape[1]
    lat = pl.BlockSpec((None, t, width), lambda b, h: (b, 0, h))
    ctx = pl.BlockSpec((None, tc, width), lambda b, h: (b, 0, h))
    out_acc = pltpu.VMEM((t, width), F32)
    state = pltpu.VMEM((GLA_HEADS_PER_STEP, HEAD_DIM, HEAD_DIM), F32)
    operands = pltpu.VMEM((3, t, width), BF16)
    decay_rows = pltpu.VMEM((2, t // GLA_CHUNK, 8, width), F32)
    return pl.pallas_call(
        _gla_kernel,
        grid=(bsz, d // width),
        in_specs=[lat, lat, lat, lat, lat, ctx, ctx, ctx,
                  pl.BlockSpec((1, HEAD_DIM), lambda b, h: (0, 0))],
        out_specs=lat,
        out_shape=jax.ShapeDtypeStruct((bsz, t, d), BF16),
        scratch_shapes=[out_acc, out_acc, state, state, operands, operands, decay_rows, decay_rows],
        compiler_params=_cparams(("arbitrary", "arbitrary")),
        name="gla",
    )(q, i, ff, fb, og, ci, cff, cfb, gnorm)


def _mix_kernel(x_ref, u_ref, v_ref, ga_ref, gb_ref, ya_ref, ws_ref, bs_ref, wa_ref, wb_ref, wo_ref,
                g1_ref, n2_ref, sh2_ref, sc2_ref, wr_ref, br_ref,
                x1_ref, h2_ref, e_ref, w_ref, rank_ref, cnt_ref, yb_ref, carry_ref):
    tm, d = x_ref.shape
    first = (pl.program_id(0) == 0) & (pl.program_id(1) == 0)

    @pl.when(first)
    def _():
        carry_ref[...] = jnp.zeros_like(carry_ref)

    v = v_ref[...].astype(F32)
    mu = jnp.mean(v, axis=-1, keepdims=True)
    vc = v - mu
    var = jnp.mean(vc * vc, axis=-1, keepdims=True)
    vn = (vc * lax.rsqrt(var + EPS)).astype(BF16)
    for ck in range(tm // MLP_CHUNK):
        rows = slice(ck * MLP_CHUNK, (ck + 1) * MLP_CHUNK)
        for g in range(d // HEAD_DIM):
            cols = slice(g * HEAD_DIM, (g + 1) * HEAD_DIM)
            s = jnp.dot(ws_ref[g], vn[rows, cols], preferred_element_type=F32) + bs_ref[g]
            yb_ref[rows, cols] = (u_ref[rows, cols].astype(F32) * s).astype(BF16)

    pa = jnp.dot(ya_ref[...], wa_ref[...], preferred_element_type=F32)
    pb = jnp.dot(yb_ref[...], wb_ref[...], preferred_element_type=F32)
    merged = ga_ref[...].astype(F32) * pa + gb_ref[...].astype(F32) * pb
    mixed = jnp.dot(merged.astype(BF16), wo_ref[...], preferred_element_type=F32)
    x1 = x_ref[...] + g1_ref[...] * mixed
    x1_ref[...] = x1

    y = x1 * lax.rsqrt(jnp.mean(x1 * x1, axis=-1, keepdims=True) + EPS) * n2_ref[...]
    h2 = y * (1.0 + sc2_ref[...]) + sh2_ref[...]
    h2_ref[...] = h2

    logits = jnp.dot(h2.astype(BF16), wr_ref[...], preferred_element_type=F32) + br_ref[...]
    lane = lax.broadcasted_iota(jnp.int32, (tm, LANES), 1)
    work = logits
    vals, idxs = [], []
    for _ in range(TOP_K):
        m = jnp.max(work, axis=-1, keepdims=True)
        idx = jnp.min(jnp.where(work == m, lane, LANES), axis=-1, keepdims=True)
        vals.append(m)
        idxs.append(idx)
        work = jnp.where(lane == idx, -jnp.inf, work)
    exps = [jnp.exp(m - vals[0]) for m in vals]
    denom = exps[0] + exps[1] + exps[2] + exps[3]

    sel = jnp.zeros((tm, LANES), F32)
    for idx in idxs:
        sel = sel + jnp.where(lane == idx, 1.0, 0.0)
    ri = lax.broadcasted_iota(jnp.int32, (tm, tm), 0)
    cj = lax.broadcasted_iota(jnp.int32, (tm, tm), 1)
    tri = jnp.where(cj < ri, 1.0, 0.0).astype(BF16)
    before = carry_ref[...] + jnp.dot(tri, sel.astype(BF16), preferred_element_type=F32)

    e_out = jnp.zeros((tm, LANES), jnp.int32)
    w_out = jnp.zeros((tm, LANES), F32)
    r_out = jnp.zeros((tm, LANES), F32)
    for k in range(TOP_K):
        rk = jnp.sum(jnp.where(lane == idxs[k], before, 0.0), axis=-1, keepdims=True)
        e_out = jnp.where(lane == k, idxs[k], e_out)
        w_out = jnp.where(lane == k, exps[k] / denom, w_out)
        r_out = jnp.where(lane == k, rk, r_out)
    e_ref[...] = e_out
    w_ref[...] = w_out
    rank_ref[...] = r_out.astype(jnp.int32)
    carry_ref[...] = carry_ref[...] + jnp.sum(sel, axis=0, keepdims=True)
    cnt_ref[...] = carry_ref[...]


def _mix_call(x, u, v, ga, gb, ya, ws_bf, bs_full, wa_bf, wb_bf, wo_bf, g1, n2, sh2, sc2, wr_pad, br_pad, *, tm):
    bsz, t, d = x.shape
    n = bsz * t
    nt = t // tm
    row = pl.BlockSpec((None, tm, d), lambda b, i: (b, i, 0))
    per_batch = pl.BlockSpec((None, 1, d), lambda b, i: (b, 0, 0))

    def flat(width):
        return pl.BlockSpec((tm, width), lambda b, i: (b * nt + i, 0))

    return pl.pallas_call(
        _mix_kernel,
        grid=(bsz, nt),
        in_specs=[row, row, row, row, row, row,
                  _resident(ws_bf.shape), _resident(bs_full.shape),
                  _resident((d, d)), _resident((d, d)), _resident((d, d)),
                  per_batch, _resident((1, d)), per_batch, per_batch,
                  _resident((d, LANES)), _resident((1, LANES))],
        out_specs=[row, flat(d), flat(LANES), flat(LANES), flat(LANES),
                   pl.BlockSpec((1, LANES), lambda b, i: (0, 0))],
        out_shape=[jax.ShapeDtypeStruct((bsz, t, d), F32),
                   jax.ShapeDtypeStruct((n, d), F32),
                   jax.ShapeDtypeStruct((n, LANES), jnp.int32),
                   jax.ShapeDtypeStruct((n, LANES), F32),
                   jax.ShapeDtypeStruct((n, LANES), jnp.int32),
                   jax.ShapeDtypeStruct((1, LANES), F32)],
        scratch_shapes=[pltpu.VMEM((tm, d), BF16), pltpu.VMEM((1, LANES), F32)],
        compiler_params=_cparams(("arbitrary", "arbitrary")),
        name="mix",
    )(x, u, v, ga, gb, ya, ws_bf, bs_full, wa_bf, wb_bf, wo_bf, g1, n2, sh2, sc2, wr_pad, br_pad)


def _padfill_kernel(ends_ref, padded_ref, xb_ref, zero_ref, sem):
    zero_ref[...] = jnp.zeros_like(zero_ref)
    n_exp = ends_ref.shape[0]

    def tail_copy(e):
        start = pl.multiple_of(ends_ref[e] - MOE_BLOCK, MOE_BLOCK)
        return pltpu.make_async_copy(zero_ref, xb_ref.at[pl.ds(start, MOE_BLOCK)], sem)

    for e in range(n_exp):
        @pl.when(padded_ref[e] > 0)
        def _():
            tail_copy(e).start()
    for e in range(n_exp):
        @pl.when(padded_ref[e] > 0)
        def _():
            tail_copy(e).wait()


def _padfill_call(pad_ends, padded, cap, d):
    return pl.pallas_call(
        _padfill_kernel,
        grid_spec=pltpu.PrefetchScalarGridSpec(
            num_scalar_prefetch=2, grid=(1,), in_specs=[],
            out_specs=pl.BlockSpec(memory_space=pl.ANY),
            scratch_shapes=[pltpu.VMEM((MOE_BLOCK, d), F32), pltpu.SemaphoreType.DMA]),
        out_shape=jax.ShapeDtypeStruct((cap, d), F32),
        compiler_params=_cparams(("arbitrary",)),
        name="padfill",
    )(pad_ends, padded)


def _row_copy(src_ref, src_row, dst_ref, dst_row, sem):
    return pltpu.make_async_copy(src_ref.at[pl.ds(src_row, 1)], dst_ref.at[pl.ds(dst_row, 1)], sem)


def _for_row_groups(n_rows, fn):
    def body(g, carry):
        base = pl.multiple_of(g * ROW_GROUP, ROW_GROUP)
        for rr in range(ROW_GROUP):
            fn(base + rr)
        return carry
    lax.fori_loop(0, n_rows // ROW_GROUP, body, 0)


DISPATCH_RING = 3


def _dispatch_kernel(dest_ref, h_ref, xin_ref, xb_ref, hbuf, load_sems, row_sems):
    del xin_ref
    tm = hbuf.shape[1]
    i = pl.program_id(0)
    last = pl.num_programs(0) - 1
    slot = i % DISPATCH_RING

    def load(tile, s):
        start = pl.multiple_of(tile * tm, tm)
        return pltpu.make_async_copy(h_ref.at[pl.ds(start, tm)], hbuf.at[s], load_sems.at[s])

    def drain(s):
        _for_row_groups(tm * TOP_K, lambda r: _row_copy(hbuf.at[s], 0, xb_ref, 0, row_sems.at[s]).wait())

    @pl.when(i == 0)
    def _():
        load(0, 0).start()

    load(i, slot).wait()

    @pl.when(i < last)
    def _():
        load(i + 1, (i + 1) % DISPATCH_RING).start()

    def issue(r):
        for k in range(TOP_K):
            _row_copy(hbuf.at[slot], r, xb_ref, dest_ref[0, 0, r * TOP_K + k], row_sems.at[slot]).start()

    _for_row_groups(tm, issue)

    @pl.when(i >= 1)
    def _():
        drain((i + DISPATCH_RING - 1) % DISPATCH_RING)

    @pl.when(i == last)
    def _():
        drain(slot)


def _dispatch_call(dest_tiles, h2, xb_init, *, tm):
    n, d = h2.shape
    return pl.pallas_call(
        _dispatch_kernel,
        grid=(n // tm,),
        in_specs=[pl.BlockSpec((1, 1, tm * TOP_K), lambda i: (i, 0, 0), memory_space=pltpu.SMEM),
                  pl.BlockSpec(memory_space=pl.ANY),
                  pl.BlockSpec(memory_space=pl.ANY)],
        out_specs=pl.BlockSpec(memory_space=pl.ANY),
        out_shape=jax.ShapeDtypeStruct(xb_init.shape, F32),
        scratch_shapes=[pltpu.VMEM((DISPATCH_RING, tm, d), F32),
                        pltpu.SemaphoreType.DMA((DISPATCH_RING,)), pltpu.SemaphoreType.DMA((DISPATCH_RING,))],
        input_output_aliases={2: 0},
        compiler_params=_cparams(("arbitrary",)),
        name="dispatch",
    )(dest_tiles, h2, xb_init)


def _expert_kernel(be_ref, nu_ref, x_ref, w1_ref, b1_ref, w2_ref, b2_ref, y_ref):
    del be_ref
    f = w2_ref.shape[1]

    @pl.when(pl.program_id(0) < nu_ref[0])
    def _():
        z = jnp.dot(x_ref[...].astype(BF16), w1_ref[0], preferred_element_type=F32) + b1_ref[0]
        gate = jnp.minimum(z[:, :f], SWIGLU_LIMIT)
        lin = jnp.clip(z[:, f:], -SWIGLU_LIMIT, SWIGLU_LIMIT)
        act = gate * jax.nn.sigmoid(SWIGLU_ALPHA * gate) * (lin + 1.0)
        y_ref[...] = jnp.dot(act.astype(BF16), w2_ref[0], preferred_element_type=F32) + b2_ref[0]


def _expert_call(block_e, n_used, xb, w1_bf, b1, w2_bf, b2):
    cap, d = xb.shape
    n_blocks = cap // MOE_BLOCK
    f2 = w1_bf.shape[2]
    f = w2_bf.shape[1]

    def blk(i, be, nu):
        return jnp.minimum(i, nu[0] - 1)

    grid_spec = pltpu.PrefetchScalarGridSpec(
        num_scalar_prefetch=2,
        grid=(n_blocks,),
        in_specs=[pl.BlockSpec((MOE_BLOCK, d), lambda i, be, nu: (blk(i, be, nu), 0)),
                  pl.BlockSpec((1, d, f2), lambda i, be, nu: (be[blk(i, be, nu)], 0, 0)),
                  pl.BlockSpec((1, 1, f2), lambda i, be, nu: (be[blk(i, be, nu)], 0, 0)),
                  pl.BlockSpec((1, f, d), lambda i, be, nu: (be[blk(i, be, nu)], 0, 0)),
                  pl.BlockSpec((1, 1, d), lambda i, be, nu: (be[blk(i, be, nu)], 0, 0))],
        out_specs=pl.BlockSpec((MOE_BLOCK, d), lambda i, be, nu: (blk(i, be, nu), 0)),
    )
    return pl.pallas_call(
        _expert_kernel,
        grid_spec=grid_spec,
        out_shape=jax.ShapeDtypeStruct((cap, d), F32),
        compiler_params=_cparams(("arbitrary",)),
        name="experts",
    )(block_e, n_used, xb, w1_bf, b1, w2_bf, b2)


def _combine_kernel(dest_ref, nxt_ref, x1_ref, w_ref, g2_ref, fg_ref, y_ref, o_ref, buf_ref, sems):
    tm = x1_ref.shape[0]
    nt = pl.num_programs(1)
    i = pl.program_id(0) * nt + pl.program_id(1)
    last = pl.num_programs(0) * nt - 1
    cur = i % 2

    def gather(tab_ref, slot):
        def issue(r):
            for k in range(TOP_K):
                _row_copy(y_ref, tab_ref[0, 0, r * TOP_K + k], buf_ref.at[slot, k], r, sems.at[slot]).start()
        _for_row_groups(tm, issue)

    @pl.when(i == 0)
    def _():
        gather(dest_ref, 0)

    @pl.when(i < last)
    def _():
        gather(nxt_ref, 1 - cur)

    _for_row_groups(tm * TOP_K, lambda r: _row_copy(y_ref, 0, buf_ref.at[cur, 0], 0, sems.at[cur]).wait())

    w = w_ref[...]
    moe = w[:, 0:1] * buf_ref[cur, 0]
    for k in range(1, TOP_K):
        moe = moe + w[:, k:k + 1] * buf_ref[cur, k]
    x2 = x1_ref[...] + g2_ref[...] * moe
    o_ref[...] = x2 * lax.rsqrt(jnp.mean(x2 * x2, axis=-1, keepdims=True) + EPS) * fg_ref[...]


def _combine_call(dest_tiles, x1, w_lanes, g2, final_g, y, *, tm):
    bsz, t, d = x1.shape
    nt = t // tm
    tiles = bsz * nt

    def table(shift):
        return pl.BlockSpec((1, 1, tm * TOP_K), lambda b, i: (jnp.minimum(b * nt + i + shift, tiles - 1), 0, 0),
                            memory_space=pltpu.SMEM)

    return pl.pallas_call(
        _combine_kernel,
        grid=(bsz, nt),
        in_specs=[table(0), table(1),
                  pl.BlockSpec((None, tm, d), lambda b, i: (b, i, 0)),
                  pl.BlockSpec((tm, LANES), lambda b, i: (b * nt + i, 0)),
                  pl.BlockSpec((None, 1, d), lambda b, i: (b, 0, 0)),
                  pl.BlockSpec((1, d), lambda b, i: (0, 0)),
                  pl.BlockSpec(memory_space=pl.ANY)],
        out_specs=pl.BlockSpec((None, tm, d), lambda b, i: (b, i, 0)),
        out_shape=jax.ShapeDtypeStruct((bsz, t, d), F32),
        scratch_shapes=[pltpu.VMEM((2, TOP_K, tm, d), F32), pltpu.SemaphoreType.DMA((2,))],
        compiler_params=_cparams(("arbitrary", "arbitrary")),
        name="combine",
    )(dest_tiles, dest_tiles, x1, w_lanes, g2, final_g, y)


def kernel(x, c, ctx, c_ctx, norm1_g, norm2_g, w_mod, b_mod, w_in, lb_fwd, lb_bwd, gnorm_g, w_s, b_s,
           w_branch_a, w_branch_b, w_out, w_router, b_router, w1, b1, w2, b2, final_g):
    bsz, t, d = x.shape
    assert w_in.shape[0] == 1 and lb_fwd.shape[0] == 2, "single-layer block"
    assert w_in.shape[2] == N_SPLITS * d and d % HEAD_DIM == 0
    assert t % MLP_CHUNK == 0 and ctx.shape[1] % GLA_CHUNK == 0
    n = bsz * t

    pad_rows = (-(bsz + 1)) % 8
    cc = jnp.concatenate([c, c_ctx[None, :], jnp.zeros((pad_rows, d), F32)], axis=0)
    mod = _mod_call(cc, w_mod[0], b_mod[0][None, :])
    lat = mod[:bsz].reshape(bsz, N_MOD, 1, d)
    sh1, sc1, g1, sh2, sc2, g2 = (lat[:, m] for m in range(N_MOD))
    cmod = mod[bsz].reshape(N_MOD, 1, 1, d)
    csh1, csc1 = cmod[0], cmod[1]

    w_in_bf = w_in[0].astype(BF16)
    n1 = norm1_g[0][None, :]
    q, i_, ff, fb, og, u, v, ga, gb = _proj_call(
        x, n1, sh1, sc1, w_in_bf, lb_fwd, lb_bwd,
        splits=tuple(range(N_SPLITS)), tm=min(256, t), per_batch_mod=True)
    ci, cff, cfb = _proj_call(
        ctx, n1, csh1, csc1, w_in_bf, lb_fwd, lb_bwd,
        splits=(S_I, S_FF, S_FB), tm=ctx.shape[1], per_batch_mod=False)

    ya = _gla_call(q, i_, ff, fb, og, ci, cff, cfb, gnorm_g[0][None, :])

    n_exp = w_router.shape[2]
    wr_pad = jnp.zeros((d, LANES), BF16).at[:, :n_exp].set(w_router[0].astype(BF16))
    br_pad = jnp.full((1, LANES), NEG_BIG, F32).at[0, :n_exp].set(b_router[0])
    bs_full = jnp.broadcast_to(b_s[0][:, :, None], b_s.shape[1:] + (HEAD_DIM,))
    x1, h2, e_l, w_l, r_l, cnt = _mix_call(
        x, u, v, ga, gb, ya, w_s[0].astype(BF16), bs_full,
        w_branch_a[0].astype(BF16), w_branch_b[0].astype(BF16), w_out[0].astype(BF16),
        g1, norm2_g[0][None, :], sh2, sc2, wr_pad, br_pad, tm=min(256, t))

    counts = cnt[0, :n_exp].astype(jnp.int32)
    padded = (counts + MOE_BLOCK - 1) // MOE_BLOCK * MOE_BLOCK
    pad_ends = jnp.cumsum(padded)
    pad_starts = pad_ends - padded
    n_blocks = -(-(n * TOP_K) // MOE_BLOCK) + n_exp
    cap = n_blocks * MOE_BLOCK
    dest = pad_starts[e_l[:, :TOP_K]] + r_l[:, :TOP_K]
    blk_start = jnp.arange(n_blocks, dtype=jnp.int32) * MOE_BLOCK
    block_e = jnp.minimum(jnp.sum((pad_ends[None, :] <= blk_start[:, None]).astype(jnp.int32), axis=1),
                          n_exp - 1)
    n_used = (pad_ends[-1:] // MOE_BLOCK).astype(jnp.int32)

    tm_io = min(256, t)
    dest_tiles = dest.reshape(n // tm_io, 1, tm_io * TOP_K)
    xb = _dispatch_call(dest_tiles, h2, _padfill_call(pad_ends, padded, cap, d), tm=tm_io)
    y = _expert_call(block_e, n_used, xb, w1[0].astype(BF16), b1[0][:, None, :],
                     w2[0].astype(BF16), b2[0][:, None, :])
    return _combine_call(dest_tiles, x1, w_l, g2, final_g[None, :], y, tm=tm_io)
```

```python
import functools

import jax
import jax.numpy as jnp
from jax import lax
from jax.experimental import pallas as pl
from jax.experimental.pallas import tpu as pltpu

F32 = jnp.float32
BF16 = jnp.bfloat16

EPS = 1e-6
N_MOD = 6
S_Q, S_I, S_FF, S_FB, S_OG, S_U, S_V, S_GA, S_GB = range(9)
N_SPLITS = 9
SPLIT_DTYPE = {S_FF: F32, S_FB: F32}
HEAD_DIM = 128
GLA_CHUNK = 64
GLA_HEADS_PER_STEP = 2
MLP_CHUNK = 128
TOP_K = 4
MOE_BLOCK = 256
ROW_GROUP = 8
SWIGLU_LIMIT = 7.0
SWIGLU_ALPHA = 1.702
LANES = 128
NEG_BIG = -1e30
VMEM_LIMIT = 56 * 1024 * 1024


def _cparams(sem):
    return pltpu.CompilerParams(dimension_semantics=sem, vmem_limit_bytes=VMEM_LIMIT)


def _resident(shape):
    zeros = (0,) * len(shape)
    return pl.BlockSpec(shape, lambda *_: zeros, pipeline_mode=pl.Buffered(1))


def _mod_kernel(c_ref, w_ref, b_ref, o_ref):
    c = c_ref[...]
    h = (c * jax.nn.sigmoid(c)).astype(BF16)
    o_ref[...] = jnp.dot(h, w_ref[...].astype(BF16), preferred_element_type=F32) + b_ref[...]


def _mod_call(cc, w_mod, b_mod):
    rows, d = cc.shape
    width = w_mod.shape[1]
    return pl.pallas_call(
        _mod_kernel,
        grid=(width // d,),
        in_specs=[pl.BlockSpec((rows, d), lambda j: (0, 0)),
                  pl.BlockSpec((d, d), lambda j: (0, j)),
                  pl.BlockSpec((1, d), lambda j: (0, j))],
        out_specs=pl.BlockSpec((rows, d), lambda j: (0, j)),
        out_shape=jax.ShapeDtypeStruct((rows, width), F32),
        compiler_params=_cparams(("arbitrary",)),
        name="mod",
    )(cc, w_mod, b_mod)


def _lower_bound(lb_ref):
    a = lb_ref[0:1, :]
    b = lb_ref[1:2, :]
    m = jnp.maximum(a, b)
    ea = jnp.exp(a - m)
    eb = jnp.exp(b - m)
    return ea / (ea + eb)


def _gelu(z):
    return 0.5 * z * (1.0 + lax.erf(z * (2.0 ** -0.5)))


def _split_activation(s, z, lbf_ref, lbb_ref):
    if s in (S_Q, S_OG):
        return z * jax.nn.sigmoid(z)
    if s == S_I:
        return z
    if s in (S_FF, S_FB):
        lb = _lower_bound(lbf_ref if s == S_FF else lbb_ref)
        return lb + (1.0 - lb) * jax.nn.sigmoid(z)
    if s in (S_U, S_V):
        return _gelu(z)
    return jax.nn.sigmoid(z)


def _proj_kernel(x_ref, g_ref, sh_ref, sc_ref, w_ref, lbf_ref, lbb_ref, *o_refs, splits):
    d = x_ref.shape[1]
    x = x_ref[...]
    y = x * lax.rsqrt(jnp.mean(x * x, axis=-1, keepdims=True) + EPS) * g_ref[...]
    h = (y * (1.0 + sc_ref[...]) + sh_ref[...]).astype(BF16)
    for s, o_ref in zip(splits, o_refs):
        z = jnp.dot(h, w_ref[:, s * d:(s + 1) * d], preferred_element_type=F32)
        o_ref[...] = _split_activation(s, z, lbf_ref, lbb_ref).astype(o_ref.dtype)


def _proj_call(x, g, sh, sc, w_bf, lbf, lbb, *, splits, tm, per_batch_mod):
    bsz, t, d = x.shape
    mod_map = (lambda b, i: (b, 0, 0)) if per_batch_mod else (lambda b, i: (0, 0, 0))
    row = pl.BlockSpec((None, tm, d), lambda b, i: (b, i, 0))
    return pl.pallas_call(
        functools.partial(_proj_kernel, splits=splits),
        grid=(bsz, t // tm),
        in_specs=[row,
                  pl.BlockSpec((1, d), lambda b, i: (0, 0)),
                  pl.BlockSpec((None, 1, d), mod_map),
                  pl.BlockSpec((None, 1, d), mod_map),
                  _resident(w_bf.shape),
                  pl.BlockSpec((2, d), lambda b, i: (0, 0)),
                  pl.BlockSpec((2, d), lambda b, i: (0, 0))],
        out_specs=[row for _ in splits],
        out_shape=[jax.ShapeDtypeStruct((bsz, t, d), SPLIT_DTYPE.get(s, BF16)) for s in splits],
        compiler_params=_cparams(("arbitrary", "arbitrary")),
        name="proj",
    )(x, g, sh, sc, w_bf, lbf, lbb)


def _prefix_sum(x, reverse):
    n = x.shape[0]
    row = lax.broadcasted_iota(jnp.int32, x.shape, 0)
    s = 1
    while s < n:
        if reverse:
            x = x + jnp.where(row < n - s, pltpu.roll(x, n - s, 0), 0.0)
        else:
            x = x + jnp.where(row >= s, pltpu.roll(x, s, 0), 0.0)
        s *= 2
    return x


def _dot_nt(a, b):
    return lax.dot_general(a, b, (((1,), (1,)), ((), ())), preferred_element_type=F32)


def _dot_tn(a, b):
    return lax.dot_general(a, b, (((0,), (0,)), ((), ())), preferred_element_type=F32)


def _gla_prepare(q, f, reverse):
    c = q.shape[0]
    b = _prefix_sum(jnp.log(f), reverse)
    if reverse:
        b_mid = b[c // 2:c // 2 + 1, :]
        b_end = b[0:1, :]
    else:
        b_mid = b[c // 2 - 1:c // 2, :]
        b_end = b[c - 1:c, :]
    qe = q * jnp.exp(b - b_mid)
    ke = (1.0 - f) * jnp.exp(b_mid - b)
    kd = ke * jnp.exp(b_end - b_mid)
    return qe.astype(BF16), ke.astype(BF16), kd.astype(BF16), jnp.exp(b_mid), jnp.exp(b_end)


def _gla_scores(qe, ke, st, e_mid):
    rhs = jnp.concatenate([(st * e_mid).astype(BF16), ke], axis=0)
    return _dot_nt(qe, rhs)


def _gla_finish(both, kd, vb, st, e_end, reverse):
    c = both.shape[0]
    dv = st.shape[0]
    ti = lax.broadcasted_iota(jnp.int32, (c, c), 0)
    si = lax.broadcasted_iota(jnp.int32, (c, c), 1)
    keep = (si >= ti) if reverse else (si <= ti)
    scores = jnp.where(keep, both[:, dv:], 0.0)
    o = both[:, :dv] + jnp.dot(scores.astype(BF16), vb, preferred_element_type=F32)
    return o, st * e_end + _dot_tn(vb, kd)


def _context_state(f, vb, reverse_decay):
    g = jnp.log(f)
    tail = _prefix_sum(g, reverse_decay) - g
    kt = ((1.0 - f) * jnp.exp(tail)).astype(BF16)
    return _dot_tn(vb, kt)


def _gla_kernel(q_ref, i_ref, ff_ref, fb_ref, og_ref, ci_ref, cff_ref, cfb_ref, gn_ref,
                o_ref, of_ref, ob_ref, sf_ref, sb_ref, ops_f, ops_b, rows_f, rows_b):
    t = q_ref.shape[0]
    c = GLA_CHUNK
    n = t // c
    heads = [slice(h * HEAD_DIM, (h + 1) * HEAD_DIM) for h in range(q_ref.shape[1] // HEAD_DIM)]
    for h, cols in enumerate(heads):
        ci = ci_ref[:, cols]
        sf_ref[h] = _context_state(cff_ref[:, cols], ci, True)
        sb_ref[h] = _context_state(cfb_ref[:, cols], ci, False)

    def prepare(step, carry):
        r = pl.ds(pl.multiple_of(step * c, c), c)
        for cols in heads:
            q = q_ref[r, cols].astype(F32)
            for f_ref, ops, rows, reverse in ((ff_ref, ops_f, rows_f, False), (fb_ref, ops_b, rows_b, True)):
                qe, ke, kd, e_mid, e_end = _gla_prepare(q, f_ref[r, cols], reverse)
                ops[0, r, cols] = qe
                ops[1, r, cols] = ke
                ops[2, r, cols] = kd
                rows[0, step, :, cols] = jnp.broadcast_to(e_mid, (8, HEAD_DIM))
                rows[1, step, :, cols] = jnp.broadcast_to(e_end, (8, HEAD_DIM))
        return carry

    lax.fori_loop(0, n, prepare, 0)

    def scan(step, carry):
        chains = []
        for h, cols in enumerate(heads):
            for ops, rows, s_ref, acc, reverse in ((ops_f, rows_f, sf_ref, of_ref, False),
                                                   (ops_b, rows_b, sb_ref, ob_ref, True)):
                ck = (n - 1 - step) if reverse else step
                chains.append((h, cols, ops, rows, s_ref, acc, reverse, ck,
                               pl.ds(pl.multiple_of(ck * c, c), c)))
        firsts = []
        for h, cols, ops, rows, s_ref, acc, reverse, ck, r in chains:
            st = s_ref[h]
            firsts.append((st, _gla_scores(ops[0, r, cols], ops[1, r, cols], st, rows[0, ck, 0:1, cols])))
        for (h, cols, ops, rows, s_ref, acc, reverse, ck, r), (st, both) in zip(chains, firsts):
            o, st = _gla_finish(both, ops[2, r, cols], i_ref[r, cols], st, rows[1, ck, 0:1, cols], reverse)
            acc[r, cols] = o
            s_ref[h] = st
        return carry

    lax.fori_loop(0, n, scan, 0)
    for h, cols in enumerate(heads):
        o = of_ref[:, cols] + ob_ref[:, cols]
        y = o * lax.rsqrt(jnp.mean(o * o, axis=-1, keepdims=True) + EPS) * gn_ref[...]
        o_ref[:, cols] = (y * og_ref[:, cols].astype(F32)).astype(o_ref.dtype)


def _gla_call(q, i, ff, fb, og, ci, cff, cfb, gnorm):
    bsz, t, d = q.shape
    width = GLA_HEADS_PER_STEP * HEAD_DIM
    tc = ci.shape[1]
    lat = pl.BlockSpec((None, t, width), lambda b, h: (b, 0, h))
    ctx = pl.BlockSpec((None, tc, width), lambda b, h: (b, 0, h))
    out_acc = pltpu.VMEM((t, width), F32)
    state = pltpu.VMEM((GLA_HEADS_PER_STEP, HEAD_DIM, HEAD_DIM), F32)
    operands = pltpu.VMEM((3, t, width), BF16)
    decay_rows = pltpu.VMEM((2, t // GLA_CHUNK, 8, width), F32)
    return pl.pallas_call(
        _gla_kernel,
        grid=(bsz, d // width),
        in_specs=[lat, lat, lat, lat, lat, ctx, ctx, ctx,
                  pl.BlockSpec((1, HEAD_DIM), lambda b, h: (0, 0))],
        out_specs=lat,
        out_shape=jax.ShapeDtypeStruct((bsz, t, d), BF16),
        scratch_shapes=[out_acc, out_acc, state, state, operands, operands, decay_rows, decay_rows],
        compiler_params=_cparams(("arbitrary", "arbitrary")),
        name="gla",
    )(q, i, ff, fb, og, ci, cff, cfb, gnorm)


def _mix_kernel(x_ref, u_ref, v_ref, ga_ref, gb_ref, ya_ref, ws_ref, bs_ref, wa_ref, wb_ref, wo_ref,
                g1_ref, n2_ref, sh2_ref, sc2_ref, wr_ref, br_ref,
                x1_ref, h2_ref, e_ref, w_ref, rank_ref, cnt_ref, yb_ref, carry_ref):
    tm, d = x_ref.shape
    first = (pl.program_id(0) == 0) & (pl.program_id(1) == 0)

    @pl.when(first)
    def _():
        carry_ref[...] = jnp.zeros_like(carry_ref)

    v = v_ref[...].astype(F32)
    mu = jnp.mean(v, axis=-1, keepdims=True)
    vc = v - mu
    var = jnp.mean(vc * vc, axis=-1, keepdims=True)
    vn = (vc * lax.rsqrt(var + EPS)).astype(BF16)
    for ck in range(tm // MLP_CHUNK):
        rows = slice(ck * MLP_CHUNK, (ck + 1) * MLP_CHUNK)
        for g in range(d // HEAD_DIM):
            cols = slice(g * HEAD_DIM, (g + 1) * HEAD_DIM)
            s = jnp.dot(ws_ref[g], vn[rows, cols], preferred_element_type=F32) + bs_ref[g]
            yb_ref[rows, cols] = (u_ref[rows, cols].astype(F32) * s).astype(BF16)

    pa = jnp.dot(ya_ref[...], wa_ref[...], preferred_element_type=F32)
    pb = jnp.dot(yb_ref[...], wb_ref[...], preferred_element_type=F32)
    merged = ga_ref[...].astype(F32) * pa + gb_ref[...].astype(F32) * pb
    mixed = jnp.dot(merged.astype(BF16), wo_ref[...], preferred_element_type=F32)
    x1 = x_ref[...] + g1_ref[...] * mixed
    x1_ref[...] = x1

    y = x1 * lax.rsqrt(jnp.mean(x1 * x1, axis=-1, keepdims=True) + EPS) * n2_ref[...]
    h2 = y * (1.0 + sc2_ref[...]) + sh2_ref[...]
    h2_ref[...] = h2

    logits = jnp.dot(h2.astype(BF16), wr_ref[...], preferred_element_type=F32) + br_ref[...]
    lane = lax.broadcasted_iota(jnp.int32, (tm, LANES), 1)
    work = logits
    vals, idxs = [], []
    for _ in range(TOP_K):
        m = jnp.max(work, axis=-1, keepdims=True)
        idx = jnp.min(jnp.where(work == m, lane, LANES), axis=-1, keepdims=True)
        vals.append(m)
        idxs.append(idx)
        work = jnp.where(lane == idx, -jnp.inf, work)
    exps = [jnp.exp(m - vals[0]) for m in vals]
    denom = exps[0] + exps[1] + exps[2] + exps[3]

    sel = jnp.zeros((tm, LANES), F32)
    for idx in idxs:
        sel = sel + jnp.where(lane == idx, 1.0, 0.0)
    ri = lax.broadcasted_iota(jnp.int32, (tm, tm), 0)
    cj = lax.broadcasted_iota(jnp.int32, (tm, tm), 1)
    tri = jnp.where(cj < ri, 1.0, 0.0).astype(BF16)
    before = carry_ref[...] + jnp.dot(tri, sel.astype(BF16), preferred_element_type=F32)

    e_out = jnp.zeros((tm, LANES), jnp.int32)
    w_out = jnp.zeros((tm, LANES), F32)
    r_out = jnp.zeros((tm, LANES), F32)
    for k in range(TOP_K):
        rk = jnp.sum(jnp.where(lane == idxs[k], before, 0.0), axis=-1, keepdims=True)
        e_out = jnp.where(lane == k, idxs[k], e_out)
        w_out = jnp.where(lane == k, exps[k] / denom, w_out)
        r_out = jnp.where(lane == k, rk, r_out)
    e_ref[...] = e_out
    w_ref[...] = w_out
    rank_ref[...] = r_out.astype(jnp.int32)
    carry_ref[...] = carry_ref[...] + jnp.sum(sel, axis=0, keepdims=True)
    cnt_ref[...] = carry_ref[...]


def _mix_call(x, u, v, ga, gb, ya, ws_bf, bs_full, wa_bf, wb_bf, wo_bf, g1, n2, sh2, sc2, wr_pad, br_pad, *, tm):
    bsz, t, d = x.shape
    n = bsz * t
    nt = t // tm
    row = pl.BlockSpec((None, tm, d), lambda b, i: (b, i, 0))
    per_batch = pl.BlockSpec((None, 1, d), lambda b, i: (b, 0, 0))

    def flat(width):
        return pl.BlockSpec((tm, width), lambda b, i: (b * nt + i, 0))

    return pl.pallas_call(
        _mix_kernel,
        grid=(bsz, nt),
        in_specs=[row, row, row, row, row, row,
                  _resident(ws_bf.shape), _resident(bs_full.shape),
                  _resident((d, d)), _resident((d, d)), _resident((d, d)),
                  per_batch, _resident((1, d)), per_batch, per_batch,
                  _resident((d, LANES)), _resident((1, LANES))],
        out_specs=[row, flat(d), flat(LANES), flat(LANES), flat(LANES),
                   pl.BlockSpec((1, LANES), lambda b, i: (0, 0))],
        out_shape=[jax.ShapeDtypeStruct((bsz, t, d), F32),
                   jax.ShapeDtypeStruct((n, d), F32),
                   jax.ShapeDtypeStruct((n, LANES), jnp.int32),
                   jax.ShapeDtypeStruct((n, LANES), F32),
                   jax.ShapeDtypeStruct((n, LANES), jnp.int32),
                   jax.ShapeDtypeStruct((1, LANES), F32)],
        scratch_shapes=[pltpu.VMEM((tm, d), BF16), pltpu.VMEM((1, LANES), F32)],
        compiler_params=_cparams(("arbitrary", "arbitrary")),
        name="mix",
    )(x, u, v, ga, gb, ya, ws_bf, bs_full, wa_bf, wb_bf, wo_bf, g1, n2, sh2, sc2, wr_pad, br_pad)


def _padfill_kernel(ends_ref, padded_ref, xb_ref, zero_ref, sem):
    zero_ref[...] = jnp.zeros_like(zero_ref)
    n_exp = ends_ref.shape[0]

    def tail_copy(e):
        start = pl.multiple_of(ends_ref[e] - MOE_BLOCK, MOE_BLOCK)
        return pltpu.make_async_copy(zero_ref, xb_ref.at[pl.ds(start, MOE_BLOCK)], sem)

    for e in range(n_exp):
        @pl.when(padded_ref[e] > 0)
        def _():
            tail_copy(e).start()
    for e in range(n_exp):
        @pl.when(padded_ref[e] > 0)
        def _():
            tail_copy(e).wait()


def _padfill_call(pad_ends, padded, cap, d):
    return pl.pallas_call(
        _padfill_kernel,
        grid_spec=pltpu.PrefetchScalarGridSpec(
            num_scalar_prefetch=2, grid=(1,), in_specs=[],
            out_specs=pl.BlockSpec(memory_space=pl.ANY),
            scratch_shapes=[pltpu.VMEM((MOE_BLOCK, d), F32), pltpu.SemaphoreType.DMA]),
        out_shape=jax.ShapeDtypeStruct((cap, d), F32),
        compiler_params=_cparams(("arbitrary",)),
        name="padfill",
    )(pad_ends, padded)


def _row_copy(src_ref, src_row, dst_ref, dst_row, sem):
    return pltpu.make_async_copy(src_ref.at[pl.ds(src_row, 1)], dst_ref.at[pl.ds(dst_row, 1)], sem)


def _for_row_groups(n_rows, fn):
    def body(g, carry):
        base = pl.multiple_of(g * ROW_GROUP, ROW_GROUP)
        for rr in range(ROW_GROUP):
            fn(base + rr)
        return carry
    lax.fori_loop(0, n_rows // ROW_GROUP, body, 0)


DISPATCH_RING = 3


def _dispatch_kernel(dest_ref, h_ref, xin_ref, xb_ref, hbuf, load_sems, row_sems):
    del xin_ref
    tm = hbuf.shape[1]
    i = pl.program_id(0)
    last = pl.num_programs(0) - 1
    slot = i % DISPATCH_RING

    def load(tile, s):
        start = pl.multiple_of(tile * tm, tm)
        return pltpu.make_async_copy(h_ref.at[pl.ds(start, tm)], hbuf.at[s], load_sems.at[s])

    def drain(s):
        for _ in range(TOP_K):
            pltpu.make_async_copy(hbuf.at[s], xb_ref.at[pl.ds(0, tm)], row_sems.at[s]).wait()

    @pl.when(i == 0)
    def _():
        load(0, 0).start()

    load(i, slot).wait()

    @pl.when(i < last)
    def _():
        load(i + 1, (i + 1) % DISPATCH_RING).start()

    def issue(r):
        for k in range(TOP_K):
            _row_copy(hbuf.at[slot], r, xb_ref, dest_ref[0, 0, r * TOP_K + k],
                      row_sems.at[slot]).start(priority=k % 2)

    _for_row_groups(tm, issue)

    @pl.when(i >= 1)
    def _():
        drain((i + DISPATCH_RING - 1) % DISPATCH_RING)

    @pl.when(i == last)
    def _():
        drain(slot)


def _dispatch_call(dest_tiles, h2, xb_init, *, tm):
    n, d = h2.shape
    return pl.pallas_call(
        _dispatch_kernel,
        grid=(n // tm,),
        in_specs=[pl.BlockSpec((1, 1, tm * TOP_K), lambda i: (i, 0, 0), memory_space=pltpu.SMEM),
                  pl.BlockSpec(memory_space=pl.ANY),
                  pl.BlockSpec(memory_space=pl.ANY)],
        out_specs=pl.BlockSpec(memory_space=pl.ANY),
        out_shape=jax.ShapeDtypeStruct(xb_init.shape, F32),
        scratch_shapes=[pltpu.VMEM((DISPATCH_RING, tm, d), F32),
                        pltpu.SemaphoreType.DMA((DISPATCH_RING,)), pltpu.SemaphoreType.DMA((DISPATCH_RING,))],
        input_output_aliases={2: 0},
        compiler_params=_cparams(("arbitrary",)),
        name="dispatch",
    )(dest_tiles, h2, xb_init)


def _expert_kernel(be_ref, nu_ref, x_ref, w1_ref, b1_ref, w2_ref, b2_ref, y_ref):
    del be_ref
    f = w2_ref.shape[1]

    @pl.when(pl.program_id(0) < nu_ref[0])
    def _():
        z = jnp.dot(x_ref[...].astype(BF16), w1_ref[0], preferred_element_type=F32) + b1_ref[0]
        gate = jnp.minimum(z[:, :f], SWIGLU_LIMIT)
        lin = jnp.clip(z[:, f:], -SWIGLU_LIMIT, SWIGLU_LIMIT)
        act = gate * jax.nn.sigmoid(SWIGLU_ALPHA * gate) * (lin + 1.0)
        y_ref[...] = jnp.dot(act.astype(BF16), w2_ref[0], preferred_element_type=F32) + b2_ref[0]


def _expert_call(block_e, n_used, xb, w1_bf, b1, w2_bf, b2):
    cap, d = xb.shape
    n_blocks = cap // MOE_BLOCK
    f2 = w1_bf.shape[2]
    f = w2_bf.shape[1]

    def blk(i, be, nu):
        return jnp.minimum(i, nu[0] - 1)

    grid_spec = pltpu.PrefetchScalarGridSpec(
        num_scalar_prefetch=2,
        grid=(n_blocks,),
        in_specs=[pl.BlockSpec((MOE_BLOCK, d), lambda i, be, nu: (blk(i, be, nu), 0)),
                  pl.BlockSpec((1, d, f2), lambda i, be, nu: (be[blk(i, be, nu)], 0, 0)),
                  pl.BlockSpec((1, 1, f2), lambda i, be, nu: (be[blk(i, be, nu)], 0, 0)),
                  pl.BlockSpec((1, f, d), lambda i, be, nu: (be[blk(i, be, nu)], 0, 0)),
                  pl.BlockSpec((1, 1, d), lambda i, be, nu: (be[blk(i, be, nu)], 0, 0))],
        out_specs=pl.BlockSpec((MOE_BLOCK, d), lambda i, be, nu: (blk(i, be, nu), 0)),
    )
    return pl.pallas_call(
        _expert_kernel,
        grid_spec=grid_spec,
        out_shape=jax.ShapeDtypeStruct((cap, d), F32),
        compiler_params=_cparams(("arbitrary",)),
        name="experts",
    )(block_e, n_used, xb, w1_bf, b1, w2_bf, b2)


def _combine_kernel(dest_ref, nxt_ref, x1_ref, w_ref, g2_ref, fg_ref, y_ref, o_ref, buf_ref, sems):
    tm = x1_ref.shape[0]
    nt = pl.num_programs(1)
    i = pl.program_id(0) * nt + pl.program_id(1)
    last = pl.num_programs(0) * nt - 1
    cur = i % 2

    def gather(tab_ref, slot):
        def issue(r):
            for k in range(TOP_K):
                _row_copy(y_ref, tab_ref[0, 0, r * TOP_K + k], buf_ref.at[slot, k], r,
                          sems.at[slot]).start(priority=k % 2)
        _for_row_groups(tm, issue)

    @pl.when(i == 0)
    def _():
        gather(dest_ref, 0)

    @pl.when(i < last)
    def _():
        gather(nxt_ref, 1 - cur)

    for k in range(TOP_K):
        pltpu.make_async_copy(y_ref.at[pl.ds(0, tm)], buf_ref.at[cur, k], sems.at[cur]).wait()

    w = w_ref[...]
    moe = w[:, 0:1] * buf_ref[cur, 0]
    for k in range(1, TOP_K):
        moe = moe + w[:, k:k + 1] * buf_ref[cur, k]
    x2 = x1_ref[...] + g2_ref[...] * moe
    o_ref[...] = x2 * lax.rsqrt(jnp.mean(x2 * x2, axis=-1, keepdims=True) + EPS) * fg_ref[...]


def _combine_call(dest_tiles, x1, w_lanes, g2, final_g, y, *, tm):
    bsz, t, d = x1.shape
    nt = t // tm
    tiles = bsz * nt

    def table(shift):
        return pl.BlockSpec((1, 1, tm * TOP_K), lambda b, i: (jnp.minimum(b * nt + i + shift, tiles - 1), 0, 0),
                            memory_space=pltpu.SMEM)

    return pl.pallas_call(
        _combine_kernel,
        grid=(bsz, nt),
        in_specs=[table(0), table(1),
                  pl.BlockSpec((None, tm, d), lambda b, i: (b, i, 0)),
                  pl.BlockSpec((tm, LANES), lambda b, i: (b * nt + i, 0)),
                  pl.BlockSpec((None, 1, d), lambda b, i: (b, 0, 0)),
                  pl.BlockSpec((1, d), lambda b, i: (0, 0)),
                  pl.BlockSpec(memory_space=pl.ANY)],
        out_specs=pl.BlockSpec((None, tm, d), lambda b, i: (b, i, 0)),
        out_shape=jax.ShapeDtypeStruct((bsz, t, d), F32),
        scratch_shapes=[pltpu.VMEM((2, TOP_K, tm, d), F32), pltpu.SemaphoreType.DMA((2,))],
        compiler_params=_cparams(("arbitrary", "arbitrary")),
        name="combine",
    )(dest_tiles, dest_tiles, x1, w_lanes, g2, final_g, y)


def kernel(x, c, ctx, c_ctx, norm1_g, norm2_g, w_mod, b_mod, w_in, lb_fwd, lb_bwd, gnorm_g, w_s, b_s,
           w_branch_a, w_branch_b, w_out, w_router, b_router, w1, b1, w2, b2, final_g):
    bsz, t, d = x.shape
    assert w_in.shape[0] == 1 and lb_fwd.shape[0] == 2, "single-layer block"
    assert w_in.shape[2] == N_SPLITS * d and d % HEAD_DIM == 0
    assert t % MLP_CHUNK == 0 and ctx.shape[1] % GLA_CHUNK == 0
    n = bsz * t

    pad_rows = (-(bsz + 1)) % 8
    cc = jnp.concatenate([c, c_ctx[None, :], jnp.zeros((pad_rows, d), F32)], axis=0)
    mod = _mod_call(cc, w_mod[0], b_mod[0][None, :])
    lat = mod[:bsz].reshape(bsz, N_MOD, 1, d)
    sh1, sc1, g1, sh2, sc2, g2 = (lat[:, m] for m in range(N_MOD))
    cmod = mod[bsz].reshape(N_MOD, 1, 1, d)
    csh1, csc1 = cmod[0], cmod[1]

    w_in_bf = w_in[0].astype(BF16)
    n1 = norm1_g[0][None, :]
    q, i_, ff, fb, og, u, v, ga, gb = _proj_call(
        x, n1, sh1, sc1, w_in_bf, lb_fwd, lb_bwd,
        splits=tuple(range(N_SPLITS)), tm=min(256, t), per_batch_mod=True)
    ci, cff, cfb = _proj_call(
        ctx, n1, csh1, csc1, w_in_bf, lb_fwd, lb_bwd,
        splits=(S_I, S_FF, S_FB), tm=ctx.shape[1], per_batch_mod=False)

    ya = _gla_call(q, i_, ff, fb, og, ci, cff, cfb, gnorm_g[0][None, :])

    n_exp = w_router.shape[2]
    wr_pad = jnp.zeros((d, LANES), BF16).at[:, :n_exp].set(w_router[0].astype(BF16))
    br_pad = jnp.full((1, LANES), NEG_BIG, F32).at[0, :n_exp].set(b_router[0])
    bs_full = jnp.broadcast_to(b_s[0][:, :, None], b_s.shape[1:] + (HEAD_DIM,))
    x1, h2, e_l, w_l, r_l, cnt = _mix_call(
        x, u, v, ga, gb, ya, w_s[0].astype(BF16), bs_full,
        w_branch_a[0].astype(BF16), w_branch_b[0].astype(BF16), w_out[0].astype(BF16),
        g1, norm2_g[0][None, :], sh2, sc2, wr_pad, br_pad, tm=min(256, t))

    counts = cnt[0, :n_exp].astype(jnp.int32)
    padded = (counts + MOE_BLOCK - 1) // MOE_BLOCK * MOE_BLOCK
    pad_ends = jnp.cumsum(padded)
    pad_starts = pad_ends - padded
    n_blocks = -(-(n * TOP_K) // MOE_BLOCK) + n_exp
    cap = n_blocks * MOE_BLOCK
    dest = pad_starts[e_l[:, :TOP_K]] + r_l[:, :TOP_K]
    blk_start = jnp.arange(n_blocks, dtype=jnp.int32) * MOE_BLOCK
    block_e = jnp.minimum(jnp.sum((pad_ends[None, :] <= blk_start[:, None]).astype(jnp.int32), axis=1),
                          n_exp - 1)
    n_used = (pad_ends[-1:] // MOE_BLOCK).astype(jnp.int32)

    tm_io = min(256, t)
    dest_tiles = dest.reshape(n // tm_io, 1, tm_io * TOP_K)
    xb = _dispatch_call(dest_tiles, h2, _padfill_call(pad_ends, padded, cap, d), tm=tm_io)
    y = _expert_call(block_e, n_used, xb, w1[0].astype(BF16), b1[0][:, None, :],
                     w2[0].astype(BF16), b2[0][:, None, :])
    return _combine_call(dest_tiles, x1, w_l, g2, final_g[None, :], y, tm=tm_io)
```

```python
import functools

import jax
import jax.numpy as jnp
from jax import lax
from jax.experimental import pallas as pl
from jax.experimental.pallas import tpu as pltpu

F32 = jnp.float32
BF16 = jnp.bfloat16

EPS = 1e-6
N_MOD = 6
S_Q, S_I, S_FF, S_FB, S_OG, S_U, S_V, S_GA, S_GB = range(9)
N_SPLITS = 9
SPLIT_DTYPE = {S_FF: F32, S_FB: F32}
HEAD_DIM = 128
GLA_CHUNK = 64
GLA_HEADS_PER_STEP = 4
MLP_CHUNK = 128
TOP_K = 4
MOE_BLOCK = 256
ROW_GROUP = 8
SWIGLU_LIMIT = 7.0
SWIGLU_ALPHA = 1.702
LANES = 128
NEG_BIG = -1e30
VMEM_LIMIT = 56 * 1024 * 1024


def _cparams(sem):
    return pltpu.CompilerParams(dimension_semantics=sem, vmem_limit_bytes=VMEM_LIMIT)


def _resident(shape):
    zeros = (0,) * len(shape)
    return pl.BlockSpec(shape, lambda *_: zeros, pipeline_mode=pl.Buffered(1))


def _mod_kernel(c_ref, w_ref, b_ref, o_ref):
    c = c_ref[...]
    h = (c * jax.nn.sigmoid(c)).astype(BF16)
    o_ref[...] = jnp.dot(h, w_ref[...].astype(BF16), preferred_element_type=F32) + b_ref[...]


def _mod_call(cc, w_mod, b_mod):
    rows, d = cc.shape
    width = w_mod.shape[1]
    return pl.pallas_call(
        _mod_kernel,
        grid=(width // d,),
        in_specs=[pl.BlockSpec((rows, d), lambda j: (0, 0)),
                  pl.BlockSpec((d, d), lambda j: (0, j)),
                  pl.BlockSpec((1, d), lambda j: (0, j))],
        out_specs=pl.BlockSpec((rows, d), lambda j: (0, j)),
        out_shape=jax.ShapeDtypeStruct((rows, width), F32),
        compiler_params=_cparams(("arbitrary",)),
        name="mod",
    )(cc, w_mod, b_mod)


def _lower_bound(lb_ref):
    a = lb_ref[0:1, :]
    b = lb_ref[1:2, :]
    m = jnp.maximum(a, b)
    ea = jnp.exp(a - m)
    eb = jnp.exp(b - m)
    return ea / (ea + eb)


def _gelu(z):
    return 0.5 * z * (1.0 + lax.erf(z * (2.0 ** -0.5)))


def _split_activation(s, z, lbf_ref, lbb_ref):
    if s in (S_Q, S_OG):
        return z * jax.nn.sigmoid(z)
    if s == S_I:
        return z
    if s in (S_FF, S_FB):
        lb = _lower_bound(lbf_ref if s == S_FF else lbb_ref)
        return lb + (1.0 - lb) * jax.nn.sigmoid(z)
    if s in (S_U, S_V):
        return _gelu(z)
    return jax.nn.sigmoid(z)


def _proj_kernel(x_ref, g_ref, sh_ref, sc_ref, w_ref, lbf_ref, lbb_ref, *o_refs, splits):
    d = x_ref.shape[1]
    x = x_ref[...]
    y = x * lax.rsqrt(jnp.mean(x * x, axis=-1, keepdims=True) + EPS) * g_ref[...]
    h = (y * (1.0 + sc_ref[...]) + sh_ref[...]).astype(BF16)
    for s, o_ref in zip(splits, o_refs):
        z = jnp.dot(h, w_ref[:, s * d:(s + 1) * d], preferred_element_type=F32)
        o_ref[...] = _split_activation(s, z, lbf_ref, lbb_ref).astype(o_ref.dtype)


def _proj_call(x, g, sh, sc, w_bf, lbf, lbb, *, splits, tm, per_batch_mod):
    bsz, t, d = x.shape
    mod_map = (lambda b, i: (b, 0, 0)) if per_batch_mod else (lambda b, i: (0, 0, 0))
    row = pl.BlockSpec((None, tm, d), lambda b, i: (b, i, 0))
    return pl.pallas_call(
        functools.partial(_proj_kernel, splits=splits),
        grid=(bsz, t // tm),
        in_specs=[row,
                  pl.BlockSpec((1, d), lambda b, i: (0, 0)),
                  pl.BlockSpec((None, 1, d), mod_map),
                  pl.BlockSpec((None, 1, d), mod_map),
                  _resident(w_bf.shape),
                  pl.BlockSpec((2, d), lambda b, i: (0, 0)),
                  pl.BlockSpec((2, d), lambda b, i: (0, 0))],
        out_specs=[row for _ in splits],
        out_shape=[jax.ShapeDtypeStruct((bsz, t, d), SPLIT_DTYPE.get(s, BF16)) for s in splits],
        compiler_params=_cparams(("arbitrary", "arbitrary")),
        name="proj",
    )(x, g, sh, sc, w_bf, lbf, lbb)


def _prefix_sum(x, reverse):
    n = x.shape[0]
    row = lax.broadcasted_iota(jnp.int32, x.shape, 0)
    s = 1
    while s < n:
        if reverse:
            x = x + jnp.where(row < n - s, pltpu.roll(x, n - s, 0), 0.0)
        else:
            x = x + jnp.where(row >= s, pltpu.roll(x, s, 0), 0.0)
        s *= 2
    return x


def _dot_nt(a, b):
    return lax.dot_general(a, b, (((1,), (1,)), ((), ())), preferred_element_type=F32)


def _dot_tn(a, b):
    return lax.dot_general(a, b, (((0,), (0,)), ((), ())), preferred_element_type=F32)


def _gla_prepare(q, f, reverse):
    c = q.shape[0]
    b = _prefix_sum(jnp.log(f), reverse)
    if reverse:
        b_mid = b[c // 2:c // 2 + 1, :]
        b_end = b[0:1, :]
    else:
        b_mid = b[c // 2 - 1:c // 2, :]
        b_end = b[c - 1:c, :]
    qe = q * jnp.exp(b - b_mid)
    ke = (1.0 - f) * jnp.exp(b_mid - b)
    kd = ke * jnp.exp(b_end - b_mid)
    return qe.astype(BF16), ke.astype(BF16), kd.astype(BF16), jnp.exp(b_mid), jnp.exp(b_end)


def _gla_scores(qe, ke, st, e_mid):
    rhs = jnp.concatenate([(st * e_mid).astype(BF16), ke], axis=0)
    return _dot_nt(qe, rhs)


def _gla_finish(both, kd, vb, st, e_end, reverse):
    c = both.shape[0]
    dv = st.shape[0]
    ti = lax.broadcasted_iota(jnp.int32, (c, c), 0)
    si = lax.broadcasted_iota(jnp.int32, (c, c), 1)
    keep = (si >= ti) if reverse else (si <= ti)
    scores = jnp.where(keep, both[:, dv:], 0.0)
    o = both[:, :dv] + jnp.dot(scores.astype(BF16), vb, preferred_element_type=F32)
    return o, st * e_end + _dot_tn(vb, kd)


def _context_state(f, vb, reverse_decay):
    g = jnp.log(f)
    tail = _prefix_sum(g, reverse_decay) - g
    kt = ((1.0 - f) * jnp.exp(tail)).astype(BF16)
    return _dot_tn(vb, kt)


def _gla_kernel(q_ref, i_ref, ff_ref, fb_ref, og_ref, ci_ref, cff_ref, cfb_ref, gn_ref,
                o_ref, of_ref, ob_ref, sf_ref, sb_ref, ops_f, ops_b, rows_f, rows_b):
    t = q_ref.shape[0]
    c = GLA_CHUNK
    n = t // c
    heads = [slice(h * HEAD_DIM, (h + 1) * HEAD_DIM) for h in range(q_ref.shape[1] // HEAD_DIM)]
    for h, cols in enumerate(heads):
        ci = ci_ref[:, cols]
        sf_ref[h] = _context_state(cff_ref[:, cols], ci, True)
        sb_ref[h] = _context_state(cfb_ref[:, cols], ci, False)

    directions = ((ff_ref, ops_f, rows_f, sf_ref, of_ref, False), (fb_ref, ops_b, rows_b, sb_ref, ob_ref, True))

    def chunk_rows(ck):
        return pl.ds(pl.multiple_of(ck * c, c), c)

    def prepare(step):
        slot = step % 2
        for cols in heads:
            for f_ref, ops, rows, _, _, reverse in directions:
                r = chunk_rows((n - 1 - step) if reverse else step)
                qe, ke, kd, e_mid, e_end = _gla_prepare(q_ref[r, cols].astype(F32), f_ref[r, cols], reverse)
                ops[slot, 0, :, cols] = qe
                ops[slot, 1, :, cols] = ke
                ops[slot, 2, :, cols] = kd
                rows[slot, 0, :, cols] = jnp.broadcast_to(e_mid, (8, HEAD_DIM))
                rows[slot, 1, :, cols] = jnp.broadcast_to(e_end, (8, HEAD_DIM))

    prepare(0)

    def scan(step, carry):
        slot = step % 2
        chains = []
        for h, cols in enumerate(heads):
            for _, ops, rows, s_ref, acc, reverse in directions:
                chains.append((h, cols, ops, rows, s_ref, acc, reverse,
                               chunk_rows((n - 1 - step) if reverse else step)))
        firsts = []
        for h, cols, ops, rows, s_ref, acc, reverse, r in chains:
            st = s_ref[h]
            firsts.append((st, _gla_scores(ops[slot, 0, :, cols], ops[slot, 1, :, cols], st,
                                           rows[slot, 0, 0:1, cols])))
        for (h, cols, ops, rows, s_ref, acc, reverse, r), (st, both) in zip(chains, firsts):
            o, st = _gla_finish(both, ops[slot, 2, :, cols], i_ref[r, cols], st, rows[slot, 1, 0:1, cols],
                                reverse)
            acc[r, cols] = o
            s_ref[h] = st
        prepare(jnp.minimum(step + 1, n - 1))
        return carry

    lax.fori_loop(0, n, scan, 0)
    for h, cols in enumerate(heads):
        o = of_ref[:, cols] + ob_ref[:, cols]
        y = o * lax.rsqrt(jnp.mean(o * o, axis=-1, keepdims=True) + EPS) * gn_ref[...]
        o_ref[:, cols] = (y * og_ref[:, cols].astype(F32)).astype(o_ref.dtype)


def _gla_call(q, i, ff, fb, og, ci, cff, cfb, gnorm):
    bsz, t, d = q.shape
    width = GLA_HEADS_PER_STEP * HEAD_DIM
    tc = ci.shape[1]
    lat = pl.BlockSpec((None, t, width), lambda b, h: (b, 0, h))
    ctx = pl.BlockSpec((None, tc, width), lambda b, h: (b, 0, h))
    out_acc = pltpu.VMEM((t, width), F32)
    state = pltpu.VMEM((GLA_HEADS_PER_STEP, HEAD_DIM, HEAD_DIM), F32)
    operands = pltpu.VMEM((2, 3, GLA_CHUNK, width), BF16)
    decay_rows = pltpu.VMEM((2, 2, 8, width), F32)
    return pl.pallas_call(
        _gla_kernel,
        grid=(bsz, d // width),
        in_specs=[lat, lat, lat, lat, lat, ctx, ctx, ctx,
                  pl.BlockSpec((1, HEAD_DIM), lambda b, h: (0, 0))],
        out_specs=lat,
        out_shape=jax.ShapeDtypeStruct((bsz, t, d), BF16),
        scratch_shapes=[out_acc, out_acc, state, state, operands, operands, decay_rows, decay_rows],
        compiler_params=_cparams(("arbitrary", "arbitrary")),
        name="gla",
    )(q, i, ff, fb, og, ci, cff, cfb, gnorm)


def _mix_kernel(x_ref, u_ref, v_ref, ga_ref, gb_ref, ya_ref, ws_ref, bs_ref, wa_ref, wb_ref, wo_ref,
                g1_ref, n2_ref, sh2_ref, sc2_ref, wr_ref, br_ref,
                x1_ref, h2_ref, e_ref, w_ref, rank_ref, cnt_ref, yb_ref, carry_ref):
    tm, d = x_ref.shape
    first = (pl.program_id(0) == 0) & (pl.program_id(1) == 0)

    @pl.when(first)
    def _():
        carry_ref[...] = jnp.zeros_like(carry_ref)

    v = v_ref[...].astype(F32)
    mu = jnp.mean(v, axis=-1, keepdims=True)
    vc = v - mu
    var = jnp.mean(vc * vc, axis=-1, keepdims=True)
    vn = (vc * lax.rsqrt(var + EPS)).astype(BF16)
    for ck in range(tm // MLP_CHUNK):
        rows = slice(ck * MLP_CHUNK, (ck + 1) * MLP_CHUNK)
        for g in range(d // HEAD_DIM):
            cols = slice(g * HEAD_DIM, (g + 1) * HEAD_DIM)
            s = jnp.dot(ws_ref[g], vn[rows, cols], preferred_element_type=F32) + bs_ref[g]
            yb_ref[rows, cols] = (u_ref[rows, cols].astype(F32) * s).astype(BF16)

    pa = jnp.dot(ya_ref[...], wa_ref[...], preferred_element_type=F32)
    pb = jnp.dot(yb_ref[...], wb_ref[...], preferred_element_type=F32)
    merged = ga_ref[...].astype(F32) * pa + gb_ref[...].astype(F32) * pb
    mixed = jnp.dot(merged.astype(BF16), wo_ref[...], preferred_element_type=F32)
    x1 = x_ref[...] + g1_ref[...] * mixed
    x1_ref[...] = x1

    y = x1 * lax.rsqrt(jnp.mean(x1 * x1, axis=-1, keepdims=True) + EPS) * n2_ref[...]
    h2 = y * (1.0 + sc2_ref[...]) + sh2_ref[...]
    h2_ref[...] = h2

    logits = jnp.dot(h2.astype(BF16), wr_ref[...], preferred_element_type=F32) + br_ref[...]
    lane = lax.broadcasted_iota(jnp.int32, (tm, LANES), 1)
    work = logits
    vals, idxs = [], []
    for _ in range(TOP_K):
        m = jnp.max(work, axis=-1, keepdims=True)
        idx = jnp.min(jnp.where(work == m, lane, LANES), axis=-1, keepdims=True)
        vals.append(m)
        idxs.append(idx)
        work = jnp.where(lane == idx, -jnp.inf, work)
    exps = [jnp.exp(m - vals[0]) for m in vals]
    denom = exps[0] + exps[1] + exps[2] + exps[3]

    sel = jnp.zeros((tm, LANES), F32)
    for idx in idxs:
        sel = sel + jnp.where(lane == idx, 1.0, 0.0)
    ri = lax.broadcasted_iota(jnp.int32, (tm, tm), 0)
    cj = lax.broadcasted_iota(jnp.int32, (tm, tm), 1)
    tri = jnp.where(cj < ri, 1.0, 0.0).astype(BF16)
    before = carry_ref[...] + jnp.dot(tri, sel.astype(BF16), preferred_element_type=F32)

    e_out = jnp.zeros((tm, LANES), jnp.int32)
    w_out = jnp.zeros((tm, LANES), F32)
    r_out = jnp.zeros((tm, LANES), F32)
    for k in range(TOP_K):
        rk = jnp.sum(jnp.where(lane == idxs[k], before, 0.0), axis=-1, keepdims=True)
        e_out = jnp.where(lane == k, idxs[k], e_out)
        w_out = jnp.where(lane == k, exps[k] / denom, w_out)
        r_out = jnp.where(lane == k, rk, r_out)
    e_ref[...] = e_out
    w_ref[...] = w_out
    rank_ref[...] = r_out.astype(jnp.int32)
    carry_ref[...] = carry_ref[...] + jnp.sum(sel, axis=0, keepdims=True)
    cnt_ref[...] = carry_ref[...]


def _mix_call(x, u, v, ga, gb, ya, ws_bf, bs_full, wa_bf, wb_bf, wo_bf, g1, n2, sh2, sc2, wr_pad, br_pad, *, tm):
    bsz, t, d = x.shape
    n = bsz * t
    nt = t // tm
    row = pl.BlockSpec((None, tm, d), lambda b, i: (b, i, 0))
    per_batch = pl.BlockSpec((None, 1, d), lambda b, i: (b, 0, 0))

    def flat(width):
        return pl.BlockSpec((tm, width), lambda b, i: (b * nt + i, 0))

    return pl.pallas_call(
        _mix_kernel,
        grid=(bsz, nt),
        in_specs=[row, row, row, row, row, row,
                  _resident(ws_bf.shape), _resident(bs_full.shape),
                  _resident((d, d)), _resident((d, d)), _resident((d, d)),
                  per_batch, _resident((1, d)), per_batch, per_batch,
                  _resident((d, LANES)), _resident((1, LANES))],
        out_specs=[row, flat(d), flat(LANES), flat(LANES), flat(LANES),
                   pl.BlockSpec((1, LANES), lambda b, i: (0, 0))],
        out_shape=[jax.ShapeDtypeStruct((bsz, t, d), F32),
                   jax.ShapeDtypeStruct((n, d), F32),
                   jax.ShapeDtypeStruct((n, LANES), jnp.int32),
                   jax.ShapeDtypeStruct((n, LANES), F32),
                   jax.ShapeDtypeStruct((n, LANES), jnp.int32),
                   jax.ShapeDtypeStruct((1, LANES), F32)],
        scratch_shapes=[pltpu.VMEM((tm, d), BF16), pltpu.VMEM((1, LANES), F32)],
        compiler_params=_cparams(("arbitrary", "arbitrary")),
        name="mix",
    )(x, u, v, ga, gb, ya, ws_bf, bs_full, wa_bf, wb_bf, wo_bf, g1, n2, sh2, sc2, wr_pad, br_pad)


def _padfill_kernel(ends_ref, padded_ref, xb_ref, zero_ref, sem):
    zero_ref[...] = jnp.zeros_like(zero_ref)
    n_exp = ends_ref.shape[0]

    def tail_copy(e):
        start = pl.multiple_of(ends_ref[e] - MOE_BLOCK, MOE_BLOCK)
        return pltpu.make_async_copy(zero_ref, xb_ref.at[pl.ds(start, MOE_BLOCK)], sem)

    for e in range(n_exp):
        @pl.when(padded_ref[e] > 0)
        def _():
            tail_copy(e).start()
    for e in range(n_exp):
        @pl.when(padded_ref[e] > 0)
        def _():
            tail_copy(e).wait()


def _padfill_call(pad_ends, padded, cap, d):
    return pl.pallas_call(
        _padfill_kernel,
        grid_spec=pltpu.PrefetchScalarGridSpec(
            num_scalar_prefetch=2, grid=(1,), in_specs=[],
            out_specs=pl.BlockSpec(memory_space=pl.ANY),
            scratch_shapes=[pltpu.VMEM((MOE_BLOCK, d), F32), pltpu.SemaphoreType.DMA]),
        out_shape=jax.ShapeDtypeStruct((cap, d), F32),
        compiler_params=_cparams(("arbitrary",)),
        name="padfill",
    )(pad_ends, padded)


def _row_copy(src_ref, src_row, dst_ref, dst_row, sem):
    return pltpu.make_async_copy(src_ref.at[pl.ds(src_row, 1)], dst_ref.at[pl.ds(dst_row, 1)], sem)


def _for_row_groups(n_rows, fn):
    def body(g, carry):
        base = pl.multiple_of(g * ROW_GROUP, ROW_GROUP)
        for rr in range(ROW_GROUP):
            fn(base + rr)
        return carry
    lax.fori_loop(0, n_rows // ROW_GROUP, body, 0)


DISPATCH_RING = 3


def _dispatch_kernel(dest_ref, h_ref, xin_ref, xb_ref, hbuf, load_sems, row_sems):
    del xin_ref
    tm = hbuf.shape[1]
    i = pl.program_id(0)
    last = pl.num_programs(0) - 1
    slot = i % DISPATCH_RING

    def load(tile, s):
        start = pl.multiple_of(tile * tm, tm)
        return pltpu.make_async_copy(h_ref.at[pl.ds(start, tm)], hbuf.at[s], load_sems.at[s])

    def drain(s):
        for _ in range(TOP_K):
            pltpu.make_async_copy(hbuf.at[s], xb_ref.at[pl.ds(0, tm)], row_sems.at[s]).wait()

    @pl.when(i == 0)
    def _():
        load(0, 0).start()

    load(i, slot).wait()

    @pl.when(i < last)
    def _():
        load(i + 1, (i + 1) % DISPATCH_RING).start()

    def issue(r):
        for k in range(TOP_K):
            _row_copy(hbuf.at[slot], r, xb_ref, dest_ref[0, 0, r * TOP_K + k],
                      row_sems.at[slot]).start(priority=k % 2)

    _for_row_groups(tm, issue)

    @pl.when(i >= 1)
    def _():
        drain((i + DISPATCH_RING - 1) % DISPATCH_RING)

    @pl.when(i == last)
    def _():
        drain(slot)


def _dispatch_call(dest_tiles, h2, xb_init, *, tm):
    n, d = h2.shape
    return pl.pallas_call(
        _dispatch_kernel,
        grid=(n // tm,),
        in_specs=[pl.BlockSpec((1, 1, tm * TOP_K), lambda i: (i, 0, 0), memory_space=pltpu.SMEM),
                  pl.BlockSpec(memory_space=pl.ANY),
                  pl.BlockSpec(memory_space=pl.ANY)],
        out_specs=pl.BlockSpec(memory_space=pl.ANY),
        out_shape=jax.ShapeDtypeStruct(xb_init.shape, F32),
        scratch_shapes=[pltpu.VMEM((DISPATCH_RING, tm, d), F32),
                        pltpu.SemaphoreType.DMA((DISPATCH_RING,)), pltpu.SemaphoreType.DMA((DISPATCH_RING,))],
        input_output_aliases={2: 0},
        compiler_params=_cparams(("arbitrary",)),
        name="dispatch",
    )(dest_tiles, h2, xb_init)


def _expert_kernel(be_ref, nu_ref, x_ref, w1_ref, b1_ref, w2_ref, b2_ref, y_ref):
    del be_ref
    f = w2_ref.shape[1]

    @pl.when(pl.program_id(0) < nu_ref[0])
    def _():
        z = jnp.dot(x_ref[...].astype(BF16), w1_ref[0], preferred_element_type=F32) + b1_ref[0]
        gate = jnp.minimum(z[:, :f], SWIGLU_LIMIT)
        lin = jnp.clip(z[:, f:], -SWIGLU_LIMIT, SWIGLU_LIMIT)
        act = gate * jax.nn.sigmoid(SWIGLU_ALPHA * gate) * (lin + 1.0)
        y_ref[...] = jnp.dot(act.astype(BF16), w2_ref[0], preferred_element_type=F32) + b2_ref[0]


def _expert_call(block_e, n_used, xb, w1_bf, b1, w2_bf, b2):
    cap, d = xb.shape
    n_blocks = cap // MOE_BLOCK
    f2 = w1_bf.shape[2]
    f = w2_bf.shape[1]

    def blk(i, be, nu):
        return jnp.minimum(i, nu[0] - 1)

    grid_spec = pltpu.PrefetchScalarGridSpec(
        num_scalar_prefetch=2,
        grid=(n_blocks,),
        in_specs=[pl.BlockSpec((MOE_BLOCK, d), lambda i, be, nu: (blk(i, be, nu), 0)),
                  pl.BlockSpec((1, d, f2), lambda i, be, nu: (be[blk(i, be, nu)], 0, 0)),
                  pl.BlockSpec((1, 1, f2), lambda i, be, nu: (be[blk(i, be, nu)], 0, 0)),
                  pl.BlockSpec((1, f, d), lambda i, be, nu: (be[blk(i, be, nu)], 0, 0)),
                  pl.BlockSpec((1, 1, d), lambda i, be, nu: (be[blk(i, be, nu)], 0, 0))],
        out_specs=pl.BlockSpec((MOE_BLOCK, d), lambda i, be, nu: (blk(i, be, nu), 0)),
    )
    return pl.pallas_call(
        _expert_kernel,
        grid_spec=grid_spec,
        out_shape=jax.ShapeDtypeStruct((cap, d), F32),
        compiler_params=_cparams(("arbitrary",)),
        name="experts",
    )(block_e, n_used, xb, w1_bf, b1, w2_bf, b2)


def _combine_kernel(dest_ref, nxt_ref, x1_ref, w_ref, g2_ref, fg_ref, y_ref, o_ref, buf_ref, sems):
    tm = x1_ref.shape[0]
    nt = pl.num_programs(1)
    i = pl.program_id(0) * nt + pl.program_id(1)
    last = pl.num_programs(0) * nt - 1
    cur = i % 2

    def gather(tab_ref, slot):
        def issue(r):
            for k in range(TOP_K):
                _row_copy(y_ref, tab_ref[0, 0, r * TOP_K + k], buf_ref.at[slot, k], r,
                          sems.at[slot]).start(priority=k % 2)
        _for_row_groups(tm, issue)

    @pl.when(i == 0)
    def _():
        gather(dest_ref, 0)

    @pl.when(i < last)
    def _():
        gather(nxt_ref, 1 - cur)

    for k in range(TOP_K):
        pltpu.make_async_copy(y_ref.at[pl.ds(0, tm)], buf_ref.at[cur, k], sems.at[cur]).wait()

    w = w_ref[...]
    moe = w[:, 0:1] * buf_ref[cur, 0]
    for k in range(1, TOP_K):
        moe = moe + w[:, k:k + 1] * buf_ref[cur, k]
    x2 = x1_ref[...] + g2_ref[...] * moe
    o_ref[...] = x2 * lax.rsqrt(jnp.mean(x2 * x2, axis=-1, keepdims=True) + EPS) * fg_ref[...]


def _combine_call(dest_tiles, x1, w_lanes, g2, final_g, y, *, tm):
    bsz, t, d = x1.shape
    nt = t // tm
    tiles = bsz * nt

    def table(shift):
        return pl.BlockSpec((1, 1, tm * TOP_K), lambda b, i: (jnp.minimum(b * nt + i + shift, tiles - 1), 0, 0),
                            memory_space=pltpu.SMEM)

    return pl.pallas_call(
        _combine_kernel,
        grid=(bsz, nt),
        in_specs=[table(0), table(1),
                  pl.BlockSpec((None, tm, d), lambda b, i: (b, i, 0)),
                  pl.BlockSpec((tm, LANES), lambda b, i: (b * nt + i, 0)),
                  pl.BlockSpec((None, 1, d), lambda b, i: (b, 0, 0)),
                  pl.BlockSpec((1, d), lambda b, i: (0, 0)),
                  pl.BlockSpec(memory_space=pl.ANY)],
        out_specs=pl.BlockSpec((None, tm, d), lambda b, i: (b, i, 0)),
        out_shape=jax.ShapeDtypeStruct((bsz, t, d), F32),
        scratch_shapes=[pltpu.VMEM((2, TOP_K, tm, d), F32), pltpu.SemaphoreType.DMA((2,))],
        compiler_params=_cparams(("arbitrary", "arbitrary")),
        name="combine",
    )(dest_tiles, dest_tiles, x1, w_lanes, g2, final_g, y)


def kernel(x, c, ctx, c_ctx, norm1_g, norm2_g, w_mod, b_mod, w_in, lb_fwd, lb_bwd, gnorm_g, w_s, b_s,
           w_branch_a, w_branch_b, w_out, w_router, b_router, w1, b1, w2, b2, final_g):
    bsz, t, d = x.shape
    assert w_in.shape[0] == 1 and lb_fwd.shape[0] == 2, "single-layer block"
    assert w_in.shape[2] == N_SPLITS * d and d % HEAD_DIM == 0
    assert t % MLP_CHUNK == 0 and ctx.shape[1] % GLA_CHUNK == 0
    n = bsz * t

    pad_rows = (-(bsz + 1)) % 8
    cc = jnp.concatenate([c, c_ctx[None, :], jnp.zeros((pad_rows, d), F32)], axis=0)
    mod = _mod_call(cc, w_mod[0], b_mod[0][None, :])
    lat = mod[:bsz].reshape(bsz, N_MOD, 1, d)
    sh1, sc1, g1, sh2, sc2, g2 = (lat[:, m] for m in range(N_MOD))
    cmod = mod[bsz].reshape(N_MOD, 1, 1, d)
    csh1, csc1 = cmod[0], cmod[1]

    w_in_bf = w_in[0].astype(BF16)
    n1 = norm1_g[0][None, :]
    q, i_, ff, fb, og, u, v, ga, gb = _proj_call(
        x, n1, sh1, sc1, w_in_bf, lb_fwd, lb_bwd,
        splits=tuple(range(N_SPLITS)), tm=min(256, t), per_batch_mod=True)
    ci, cff, cfb = _proj_call(
        ctx, n1, csh1, csc1, w_in_bf, lb_fwd, lb_bwd,
        splits=(S_I, S_FF, S_FB), tm=ctx.shape[1], per_batch_mod=False)

    ya = _gla_call(q, i_, ff, fb, og, ci, cff, cfb, gnorm_g[0][None, :])

    n_exp = w_router.shape[2]
    wr_pad = jnp.zeros((d, LANES), BF16).at[:, :n_exp].set(w_router[0].astype(BF16))
    br_pad = jnp.full((1, LANES), NEG_BIG, F32).at[0, :n_exp].set(b_router[0])
    bs_full = jnp.broadcast_to(b_s[0][:, :, None], b_s.shape[1:] + (HEAD_DIM,))
    x1, h2, e_l, w_l, r_l, cnt = _mix_call(
        x, u, v, ga, gb, ya, w_s[0].astype(BF16), bs_full,
        w_branch_a[0].astype(BF16), w_branch_b[0].astype(BF16), w_out[0].astype(BF16),
        g1, norm2_g[0][None, :], sh2, sc2, wr_pad, br_pad, tm=min(256, t))

    counts = cnt[0, :n_exp].astype(jnp.int32)
    padded = (counts + MOE_BLOCK - 1) // MOE_BLOCK * MOE_BLOCK
    pad_ends = jnp.cumsum(padded)
    pad_starts = pad_ends - padded
    n_blocks = -(-(n * TOP_K) // MOE_BLOCK) + n_exp
    cap = n_blocks * MOE_BLOCK
    dest = pad_starts[e_l[:, :TOP_K]] + r_l[:, :TOP_K]
    blk_start = jnp.arange(n_blocks, dtype=jnp.int32) * MOE_BLOCK
    block_e = jnp.minimum(jnp.sum((pad_ends[None, :] <= blk_start[:, None]).astype(jnp.int32), axis=1),
                          n_exp - 1)
    n_used = (pad_ends[-1:] // MOE_BLOCK).astype(jnp.int32)

    tm_io = min(256, t)
    dest_tiles = dest.reshape(n // tm_io, 1, tm_io * TOP_K)
    xb = _dispatch_call(dest_tiles, h2, _padfill_call(pad_ends, padded, cap, d), tm=tm_io)
    y = _expert_call(block_e, n_used, xb, w1[0].astype(BF16), b1[0][:, None, :],
                     w2[0].astype(BF16), b2[0][:, None, :])
    return _combine_call(dest_tiles, x1, w_l, g2, final_g[None, :], y, tm=tm_io)
```

```python
import functools

import jax
import jax.numpy as jnp
from jax import lax
from jax.experimental import pallas as pl
from jax.experimental.pallas import tpu as pltpu

F32 = jnp.float32
BF16 = jnp.bfloat16

EPS = 1e-6
N_MOD = 6
S_Q, S_I, S_FF, S_FB, S_OG, S_U, S_V, S_GA, S_GB = range(9)
N_SPLITS = 9
SPLIT_DTYPE = {S_FF: F32, S_FB: F32}
HEAD_DIM = 128
GLA_CHUNK = 64
GLA_HEADS_PER_STEP = 4
MLP_CHUNK = 128
TOP_K = 4
MOE_BLOCK = 256
ROW_GROUP = 8
SWIGLU_LIMIT = 7.0
SWIGLU_ALPHA = 1.702
LANES = 128
NEG_BIG = -1e30
VMEM_LIMIT = 56 * 1024 * 1024


def _cparams(sem):
    return pltpu.CompilerParams(dimension_semantics=sem, vmem_limit_bytes=VMEM_LIMIT)


def _resident(shape):
    zeros = (0,) * len(shape)
    return pl.BlockSpec(shape, lambda *_: zeros, pipeline_mode=pl.Buffered(1))


def _mod_kernel(c_ref, w_ref, b_ref, o_ref):
    c = c_ref[...]
    h = (c * jax.nn.sigmoid(c)).astype(BF16)
    o_ref[...] = jnp.dot(h, w_ref[...].astype(BF16), preferred_element_type=F32) + b_ref[...]


def _mod_call(cc, w_mod, b_mod):
    rows, d = cc.shape
    width = w_mod.shape[1]
    return pl.pallas_call(
        _mod_kernel,
        grid=(width // d,),
        in_specs=[pl.BlockSpec((rows, d), lambda j: (0, 0)),
                  pl.BlockSpec((d, d), lambda j: (0, j)),
                  pl.BlockSpec((1, d), lambda j: (0, j))],
        out_specs=pl.BlockSpec((rows, d), lambda j: (0, j)),
        out_shape=jax.ShapeDtypeStruct((rows, width), F32),
        compiler_params=_cparams(("arbitrary",)),
        name="mod",
    )(cc, w_mod, b_mod)


def _lower_bound(lb_ref):
    a = lb_ref[0:1, :]
    b = lb_ref[1:2, :]
    m = jnp.maximum(a, b)
    ea = jnp.exp(a - m)
    eb = jnp.exp(b - m)
    return ea / (ea + eb)


def _gelu(z):
    return 0.5 * z * (1.0 + lax.erf(z * (2.0 ** -0.5)))


def _split_activation(s, z, lbf_ref, lbb_ref):
    if s in (S_Q, S_OG):
        return z * jax.nn.sigmoid(z)
    if s == S_I:
        return z
    if s in (S_FF, S_FB):
        lb = _lower_bound(lbf_ref if s == S_FF else lbb_ref)
        return lb + (1.0 - lb) * jax.nn.sigmoid(z)
    if s in (S_U, S_V):
        return _gelu(z)
    return jax.nn.sigmoid(z)


def _proj_kernel(x_ref, g_ref, sh_ref, sc_ref, w_ref, lbf_ref, lbb_ref, *o_refs, splits):
    d = x_ref.shape[1]
    x = x_ref[...]
    y = x * lax.rsqrt(jnp.mean(x * x, axis=-1, keepdims=True) + EPS) * g_ref[...]
    h = (y * (1.0 + sc_ref[...]) + sh_ref[...]).astype(BF16)
    for s, o_ref in zip(splits, o_refs):
        z = jnp.dot(h, w_ref[:, s * d:(s + 1) * d], preferred_element_type=F32)
        o_ref[...] = _split_activation(s, z, lbf_ref, lbb_ref).astype(o_ref.dtype)


def _proj_call(x, g, sh, sc, w_bf, lbf, lbb, *, splits, tm, per_batch_mod):
    bsz, t, d = x.shape
    mod_map = (lambda b, i: (b, 0, 0)) if per_batch_mod else (lambda b, i: (0, 0, 0))
    row = pl.BlockSpec((None, tm, d), lambda b, i: (b, i, 0))
    return pl.pallas_call(
        functools.partial(_proj_kernel, splits=splits),
        grid=(bsz, t // tm),
        in_specs=[row,
                  pl.BlockSpec((1, d), lambda b, i: (0, 0)),
                  pl.BlockSpec((None, 1, d), mod_map),
                  pl.BlockSpec((None, 1, d), mod_map),
                  _resident(w_bf.shape),
                  pl.BlockSpec((2, d), lambda b, i: (0, 0)),
                  pl.BlockSpec((2, d), lambda b, i: (0, 0))],
        out_specs=[row for _ in splits],
        out_shape=[jax.ShapeDtypeStruct((bsz, t, d), SPLIT_DTYPE.get(s, BF16)) for s in splits],
        compiler_params=_cparams(("arbitrary", "arbitrary")),
        name="proj",
    )(x, g, sh, sc, w_bf, lbf, lbb)


def _prefix_sum(x, reverse):
    n = x.shape[0]
    row = lax.broadcasted_iota(jnp.int32, x.shape, 0)
    s = 1
    while s < n:
        if reverse:
            x = x + jnp.where(row < n - s, pltpu.roll(x, n - s, 0), 0.0)
        else:
            x = x + jnp.where(row >= s, pltpu.roll(x, s, 0), 0.0)
        s *= 2
    return x


def _dot_nt(a, b):
    return lax.dot_general(a, b, (((1,), (1,)), ((), ())), preferred_element_type=F32)


def _dot_tn(a, b):
    return lax.dot_general(a, b, (((0,), (0,)), ((), ())), preferred_element_type=F32)


def _gla_prepare(q, f, reverse):
    c = q.shape[0]
    b = _prefix_sum(jnp.log(f), reverse)
    if reverse:
        b_mid = b[c // 2:c // 2 + 1, :]
        b_end = b[0:1, :]
    else:
        b_mid = b[c // 2 - 1:c // 2, :]
        b_end = b[c - 1:c, :]
    qe = q * jnp.exp(b - b_mid)
    ke = (1.0 - f) * jnp.exp(b_mid - b)
    kd = ke * jnp.exp(b_end - b_mid)
    return qe.astype(BF16), ke.astype(BF16), kd.astype(BF16), jnp.exp(b_mid), jnp.exp(b_end)


def _gla_scores(qe, ke, st, e_mid):
    rhs = jnp.concatenate([(st * e_mid).astype(BF16), ke], axis=0)
    return _dot_nt(qe, rhs)


def _gla_finish(both, kd, vb, st, e_end, reverse):
    c = both.shape[0]
    dv = st.shape[0]
    ti = lax.broadcasted_iota(jnp.int32, (c, c), 0)
    si = lax.broadcasted_iota(jnp.int32, (c, c), 1)
    keep = (si >= ti) if reverse else (si <= ti)
    scores = jnp.where(keep, both[:, dv:], 0.0)
    o = both[:, :dv] + jnp.dot(scores.astype(BF16), vb, preferred_element_type=F32)
    return o, st * e_end + _dot_tn(vb, kd)


def _context_state(f, vb, reverse_decay):
    g = jnp.log(f)
    tail = _prefix_sum(g, reverse_decay) - g
    kt = ((1.0 - f) * jnp.exp(tail)).astype(BF16)
    return _dot_tn(vb, kt)


def _gla_kernel(q_ref, i_ref, ff_ref, fb_ref, og_ref, ci_ref, cff_ref, cfb_ref, gn_ref,
                o_ref, of_ref, ob_ref, sf_ref, sb_ref, ops_f, ops_b, rows_f, rows_b):
    t = q_ref.shape[0]
    c = GLA_CHUNK
    n = t // c
    heads = [slice(h * HEAD_DIM, (h + 1) * HEAD_DIM) for h in range(q_ref.shape[1] // HEAD_DIM)]
    for h, cols in enumerate(heads):
        ci = ci_ref[:, cols]
        sf_ref[h] = _context_state(cff_ref[:, cols], ci, True)
        sb_ref[h] = _context_state(cfb_ref[:, cols], ci, False)

    directions = ((ff_ref, ops_f, rows_f, sf_ref, of_ref, False), (fb_ref, ops_b, rows_b, sb_ref, ob_ref, True))

    def chunk_rows(ck):
        return pl.ds(pl.multiple_of(ck * c, c), c)

    def prepare(step):
        slot = step % 2
        for cols in heads:
            for f_ref, ops, rows, _, _, reverse in directions:
                r = chunk_rows((n - 1 - step) if reverse else step)
                qe, ke, kd, e_mid, e_end = _gla_prepare(q_ref[r, cols].astype(F32), f_ref[r, cols], reverse)
                ops[slot, 0, :, cols] = qe
                ops[slot, 1, :, cols] = ke
                ops[slot, 2, :, cols] = kd
                rows[slot, 0, :, cols] = jnp.broadcast_to(e_mid, (8, HEAD_DIM))
                rows[slot, 1, :, cols] = jnp.broadcast_to(e_end, (8, HEAD_DIM))

    prepare(0)

    def scan(step, carry):
        slot = step % 2
        chains = []
        for h, cols in enumerate(heads):
            for _, ops, rows, s_ref, acc, reverse in directions:
                chains.append((h, cols, ops, rows, s_ref, acc, reverse,
                               chunk_rows((n - 1 - step) if reverse else step)))
        firsts = []
        for h, cols, ops, rows, s_ref, acc, reverse, r in chains:
            st = s_ref[h]
            firsts.append((st, _gla_scores(ops[slot, 0, :, cols], ops[slot, 1, :, cols], st,
                                           rows[slot, 0, 0:1, cols])))
        for (h, cols, ops, rows, s_ref, acc, reverse, r), (st, both) in zip(chains, firsts):
            o, st = _gla_finish(both, ops[slot, 2, :, cols], i_ref[r, cols], st, rows[slot, 1, 0:1, cols],
                                reverse)
            acc[r, cols] = o
            s_ref[h] = st
        prepare(jnp.minimum(step + 1, n - 1))
        return carry

    lax.fori_loop(0, n, scan, 0)
    for h, cols in enumerate(heads):
        o = of_ref[:, cols] + ob_ref[:, cols]
        y = o * lax.rsqrt(jnp.mean(o * o, axis=-1, keepdims=True) + EPS) * gn_ref[...]
        o_ref[:, cols] = (y * og_ref[:, cols].astype(F32)).astype(o_ref.dtype)


def _gla_call(q, i, ff, fb, og, ci, cff, cfb, gnorm):
    bsz, t, d = q.shape
    width = GLA_HEADS_PER_STEP * HEAD_DIM
    tc = ci.shape[1]
    lat = pl.BlockSpec((None, t, width), lambda b, h: (b, 0, h))
    ctx = pl.BlockSpec((None, tc, width), lambda b, h: (b, 0, h))
    out_acc = pltpu.VMEM((t, width), F32)
    state = pltpu.VMEM((GLA_HEADS_PER_STEP, HEAD_DIM, HEAD_DIM), F32)
    operands = pltpu.VMEM((2, 3, GLA_CHUNK, width), BF16)
    decay_rows = pltpu.VMEM((2, 2, 8, width), F32)
    return pl.pallas_call(
        _gla_kernel,
        grid=(bsz, d // width),
        in_specs=[lat, lat, lat, lat, lat, ctx, ctx, ctx,
                  pl.BlockSpec((1, HEAD_DIM), lambda b, h: (0, 0))],
        out_specs=lat,
        out_shape=jax.ShapeDtypeStruct((bsz, t, d), BF16),
        scratch_shapes=[out_acc, out_acc, state, state, operands, operands, decay_rows, decay_rows],
        compiler_params=_cparams(("arbitrary", "arbitrary")),
        name="gla",
    )(q, i, ff, fb, og, ci, cff, cfb, gnorm)


def _mix_kernel(x_ref, u_ref, v_ref, ga_ref, gb_ref, ya_ref, ws_ref, bs_ref, wa_ref, wb_ref, wo_ref,
                g1_ref, n2_ref, sh2_ref, sc2_ref, wr_ref, br_ref,
                x1_ref, h2_ref, e_ref, w_ref, rank_ref, cnt_ref, yb_ref, carry_ref):
    tm, d = x_ref.shape
    first = (pl.program_id(0) == 0) & (pl.program_id(1) == 0)

    @pl.when(first)
    def _():
        carry_ref[...] = jnp.zeros_like(carry_ref)

    v = v_ref[...].astype(F32)
    mu = jnp.mean(v, axis=-1, keepdims=True)
    vc = v - mu
    var = jnp.mean(vc * vc, axis=-1, keepdims=True)
    vn = (vc * lax.rsqrt(var + EPS)).astype(BF16)
    for ck in range(tm // MLP_CHUNK):
        rows = slice(ck * MLP_CHUNK, (ck + 1) * MLP_CHUNK)
        for g in range(d // HEAD_DIM):
            cols = slice(g * HEAD_DIM, (g + 1) * HEAD_DIM)
            s = jnp.dot(ws_ref[g], vn[rows, cols], preferred_element_type=F32) + bs_ref[g]
            yb_ref[rows, cols] = (u_ref[rows, cols].astype(F32) * s).astype(BF16)

    pa = jnp.dot(ya_ref[...], wa_ref[...], preferred_element_type=F32)
    pb = jnp.dot(yb_ref[...], wb_ref[...], preferred_element_type=F32)
    merged = ga_ref[...].astype(F32) * pa + gb_ref[...].astype(F32) * pb
    mixed = jnp.dot(merged.astype(BF16), wo_ref[...], preferred_element_type=F32)
    x1 = x_ref[...] + g1_ref[...] * mixed
    x1_ref[...] = x1

    y = x1 * lax.rsqrt(jnp.mean(x1 * x1, axis=-1, keepdims=True) + EPS) * n2_ref[...]
    h2 = y * (1.0 + sc2_ref[...]) + sh2_ref[...]
    h2_ref[...] = h2

    logits = _dot_nt(wr_ref[...], h2.astype(BF16)) + br_ref[...]
    n_rows = logits.shape[0]
    eidx = lax.broadcasted_iota(jnp.int32, (n_rows, tm), 0)
    work = logits
    vals, idxs = [], []
    for _ in range(TOP_K):
        m = jnp.max(work, axis=0, keepdims=True)
        idx = jnp.min(jnp.where(work == m, eidx, n_rows), axis=0, keepdims=True)
        vals.append(m)
        idxs.append(idx)
        work = jnp.where(eidx == idx, -jnp.inf, work)
    exps = [jnp.exp(m - vals[0]) for m in vals]
    denom = exps[0] + exps[1] + exps[2] + exps[3]

    sel = jnp.zeros((n_rows, tm), F32)
    for idx in idxs:
        sel = sel + jnp.where(eidx == idx, 1.0, 0.0)
    tj = lax.broadcasted_iota(jnp.int32, (tm, tm), 0)
    tt = lax.broadcasted_iota(jnp.int32, (tm, tm), 1)
    tri = jnp.where(tj < tt, 1.0, 0.0).astype(BF16)
    before = carry_ref[...] + jnp.dot(sel.astype(BF16), tri, preferred_element_type=F32)

    krow = lax.broadcasted_iota(jnp.int32, e_ref.shape, 0)
    e_out = jnp.zeros(e_ref.shape, jnp.int32)
    w_out = jnp.zeros(e_ref.shape, F32)
    r_out = jnp.zeros(e_ref.shape, F32)
    for k in range(TOP_K):
        rk = jnp.sum(jnp.where(eidx == idxs[k], before, 0.0), axis=0, keepdims=True)
        e_out = jnp.where(krow == k, idxs[k], e_out)
        w_out = jnp.where(krow == k, exps[k] / denom, w_out)
        r_out = jnp.where(krow == k, rk, r_out)
    e_ref[...] = e_out
    w_ref[...] = w_out
    rank_ref[...] = r_out.astype(jnp.int32)
    carry_ref[...] = carry_ref[...] + jnp.sum(sel, axis=1, keepdims=True)
    cnt_ref[...] = carry_ref[...]


def _mix_call(x, u, v, ga, gb, ya, ws_bf, bs_full, wa_bf, wb_bf, wo_bf, g1, n2, sh2, sc2, wr_t, br_t, *, tm):
    bsz, t, d = x.shape
    n = bsz * t
    nt = t // tm
    n_rows = wr_t.shape[0]
    row = pl.BlockSpec((None, tm, d), lambda b, i: (b, i, 0))
    per_batch = pl.BlockSpec((None, 1, d), lambda b, i: (b, 0, 0))
    per_token = pl.BlockSpec((ROW_GROUP, tm), lambda b, i: (0, b * nt + i))

    return pl.pallas_call(
        _mix_kernel,
        grid=(bsz, nt),
        in_specs=[row, row, row, row, row, row,
                  _resident(ws_bf.shape), _resident(bs_full.shape),
                  _resident((d, d)), _resident((d, d)), _resident((d, d)),
                  per_batch, _resident((1, d)), per_batch, per_batch,
                  _resident(wr_t.shape), _resident(br_t.shape)],
        out_specs=[row, pl.BlockSpec((tm, d), lambda b, i: (b * nt + i, 0)),
                   per_token, per_token, per_token,
                   pl.BlockSpec((n_rows, tm), lambda b, i: (0, 0))],
        out_shape=[jax.ShapeDtypeStruct((bsz, t, d), F32),
                   jax.ShapeDtypeStruct((n, d), F32),
                   jax.ShapeDtypeStruct((ROW_GROUP, n), jnp.int32),
                   jax.ShapeDtypeStruct((ROW_GROUP, n), F32),
                   jax.ShapeDtypeStruct((ROW_GROUP, n), jnp.int32),
                   jax.ShapeDtypeStruct((n_rows, tm), F32)],
        scratch_shapes=[pltpu.VMEM((tm, d), BF16), pltpu.VMEM((n_rows, tm), F32)],
        compiler_params=_cparams(("arbitrary", "arbitrary")),
        name="mix",
    )(x, u, v, ga, gb, ya, ws_bf, bs_full, wa_bf, wb_bf, wo_bf, g1, n2, sh2, sc2, wr_t, br_t)


def _padfill_kernel(ends_ref, padded_ref, xb_ref, zero_ref, sem):
    zero_ref[...] = jnp.zeros_like(zero_ref)
    n_exp = ends_ref.shape[0]

    def tail_copy(e):
        start = pl.multiple_of(ends_ref[e] - MOE_BLOCK, MOE_BLOCK)
        return pltpu.make_async_copy(zero_ref, xb_ref.at[pl.ds(start, MOE_BLOCK)], sem)

    for e in range(n_exp):
        @pl.when(padded_ref[e] > 0)
        def _():
            tail_copy(e).start()
    for e in range(n_exp):
        @pl.when(padded_ref[e] > 0)
        def _():
            tail_copy(e).wait()


def _padfill_call(pad_ends, padded, cap, d):
    return pl.pallas_call(
        _padfill_kernel,
        grid_spec=pltpu.PrefetchScalarGridSpec(
            num_scalar_prefetch=2, grid=(1,), in_specs=[],
            out_specs=pl.BlockSpec(memory_space=pl.ANY),
            scratch_shapes=[pltpu.VMEM((MOE_BLOCK, d), F32), pltpu.SemaphoreType.DMA]),
        out_shape=jax.ShapeDtypeStruct((cap, d), F32),
        compiler_params=_cparams(("arbitrary",)),
        name="padfill",
    )(pad_ends, padded)


def _row_copy(src_ref, src_row, dst_ref, dst_row, sem):
    return pltpu.make_async_copy(src_ref.at[pl.ds(src_row, 1)], dst_ref.at[pl.ds(dst_row, 1)], sem)


def _for_row_groups(n_rows, fn):
    def body(g, carry):
        base = pl.multiple_of(g * ROW_GROUP, ROW_GROUP)
        for rr in range(ROW_GROUP):
            fn(base + rr)
        return carry
    lax.fori_loop(0, n_rows // ROW_GROUP, body, 0)


DISPATCH_RING = 3


def _dispatch_kernel(dest_ref, h_ref, xin_ref, xb_ref, hbuf, load_sems, row_sems):
    del xin_ref
    tm = hbuf.shape[1]
    i = pl.program_id(0)
    last = pl.num_programs(0) - 1
    slot = i % DISPATCH_RING

    def load(tile, s):
        start = pl.multiple_of(tile * tm, tm)
        return pltpu.make_async_copy(h_ref.at[pl.ds(start, tm)], hbuf.at[s], load_sems.at[s])

    def drain(s):
        for _ in range(TOP_K):
            pltpu.make_async_copy(hbuf.at[s], xb_ref.at[pl.ds(0, tm)], row_sems.at[s]).wait()

    @pl.when(i == 0)
    def _():
        load(0, 0).start()

    load(i, slot).wait()

    @pl.when(i < last)
    def _():
        load(i + 1, (i + 1) % DISPATCH_RING).start()

    def issue(r):
        for k in range(TOP_K):
            _row_copy(hbuf.at[slot], r, xb_ref, dest_ref[0, 0, k * tm + r],
                      row_sems.at[slot]).start(priority=k % 2)

    _for_row_groups(tm, issue)

    @pl.when(i >= 1)
    def _():
        drain((i + DISPATCH_RING - 1) % DISPATCH_RING)

    @pl.when(i == last)
    def _():
        drain(slot)


def _dispatch_call(dest_tiles, h2, xb_init, *, tm):
    n, d = h2.shape
    return pl.pallas_call(
        _dispatch_kernel,
        grid=(n // tm,),
        in_specs=[pl.BlockSpec((1, 1, tm * TOP_K), lambda i: (i, 0, 0), memory_space=pltpu.SMEM),
                  pl.BlockSpec(memory_space=pl.ANY),
                  pl.BlockSpec(memory_space=pl.ANY)],
        out_specs=pl.BlockSpec(memory_space=pl.ANY),
        out_shape=jax.ShapeDtypeStruct(xb_init.shape, F32),
        scratch_shapes=[pltpu.VMEM((DISPATCH_RING, tm, d), F32),
                        pltpu.SemaphoreType.DMA((DISPATCH_RING,)), pltpu.SemaphoreType.DMA((DISPATCH_RING,))],
        input_output_aliases={2: 0},
        compiler_params=_cparams(("arbitrary",)),
        name="dispatch",
    )(dest_tiles, h2, xb_init)


def _expert_kernel(be_ref, nu_ref, x_ref, w1_ref, b1_ref, w2_ref, b2_ref, y_ref, w1b_ref, w2b_ref):
    i = pl.program_id(0)
    f = w2_ref.shape[1]
    active = i < nu_ref[0]

    @pl.when(active & ((i == 0) | (be_ref[i] != be_ref[jnp.maximum(i - 1, 0)])))
    def _():
        w1b_ref[...] = w1_ref[0].astype(BF16)
        w2b_ref[...] = w2_ref[0].astype(BF16)

    @pl.when(active)
    def _():
        z = jnp.dot(x_ref[...].astype(BF16), w1b_ref[...], preferred_element_type=F32) + b1_ref[0]
        gate = jnp.minimum(z[:, :f], SWIGLU_LIMIT)
        lin = jnp.clip(z[:, f:], -SWIGLU_LIMIT, SWIGLU_LIMIT)
        act = gate * jax.nn.sigmoid(SWIGLU_ALPHA * gate) * (lin + 1.0)
        y_ref[...] = jnp.dot(act.astype(BF16), w2b_ref[...], preferred_element_type=F32) + b2_ref[0]


def _expert_call(block_e, n_used, xb, w1, b1, w2, b2):
    cap, d = xb.shape
    n_blocks = cap // MOE_BLOCK
    f2 = w1.shape[2]
    f = w2.shape[1]

    def blk(i, be, nu):
        return jnp.minimum(i, nu[0] - 1)

    grid_spec = pltpu.PrefetchScalarGridSpec(
        num_scalar_prefetch=2,
        grid=(n_blocks,),
        in_specs=[pl.BlockSpec((MOE_BLOCK, d), lambda i, be, nu: (blk(i, be, nu), 0)),
                  pl.BlockSpec((1, d, f2), lambda i, be, nu: (be[blk(i, be, nu)], 0, 0)),
                  pl.BlockSpec((1, 1, f2), lambda i, be, nu: (be[blk(i, be, nu)], 0, 0)),
                  pl.BlockSpec((1, f, d), lambda i, be, nu: (be[blk(i, be, nu)], 0, 0)),
                  pl.BlockSpec((1, 1, d), lambda i, be, nu: (be[blk(i, be, nu)], 0, 0))],
        out_specs=pl.BlockSpec((MOE_BLOCK, d), lambda i, be, nu: (blk(i, be, nu), 0)),
        scratch_shapes=[pltpu.VMEM((d, f2), BF16), pltpu.VMEM((f, d), BF16)],
    )
    return pl.pallas_call(
        _expert_kernel,
        grid_spec=grid_spec,
        out_shape=jax.ShapeDtypeStruct((cap, d), F32),
        compiler_params=_cparams(("arbitrary",)),
        name="experts",
    )(block_e, n_used, xb, w1, b1, w2, b2)


def _combine_kernel(dest_ref, nxt_ref, x1_ref, w_ref, g2_ref, fg_ref, y_ref, o_ref, buf_ref, sems):
    tm = x1_ref.shape[0]
    nt = pl.num_programs(1)
    i = pl.program_id(0) * nt + pl.program_id(1)
    last = pl.num_programs(0) * nt - 1
    cur = i % 2

    def gather(tab_ref, slot):
        def issue(r):
            for k in range(TOP_K):
                _row_copy(y_ref, tab_ref[0, 0, k * tm + r], buf_ref.at[slot, k], r,
                          sems.at[slot]).start(priority=k % 2)
        _for_row_groups(tm, issue)

    @pl.when(i == 0)
    def _():
        gather(dest_ref, 0)

    @pl.when(i < last)
    def _():
        gather(nxt_ref, 1 - cur)

    for k in range(TOP_K):
        pltpu.make_async_copy(y_ref.at[pl.ds(0, tm)], buf_ref.at[cur, k], sems.at[cur]).wait()

    w = w_ref[...]
    moe = w[:, 0:1] * buf_ref[cur, 0]
    for k in range(1, TOP_K):
        moe = moe + w[:, k:k + 1] * buf_ref[cur, k]
    x2 = x1_ref[...] + g2_ref[...] * moe
    o_ref[...] = x2 * lax.rsqrt(jnp.mean(x2 * x2, axis=-1, keepdims=True) + EPS) * fg_ref[...]


def _combine_call(dest_tiles, x1, w_lanes, g2, final_g, y, *, tm):
    bsz, t, d = x1.shape
    nt = t // tm
    tiles = bsz * nt

    def table(shift):
        return pl.BlockSpec((1, 1, tm * TOP_K), lambda b, i: (jnp.minimum(b * nt + i + shift, tiles - 1), 0, 0),
                            memory_space=pltpu.SMEM)

    return pl.pallas_call(
        _combine_kernel,
        grid=(bsz, nt),
        in_specs=[table(0), table(1),
                  pl.BlockSpec((None, tm, d), lambda b, i: (b, i, 0)),
                  pl.BlockSpec((tm, LANES), lambda b, i: (b * nt + i, 0)),
                  pl.BlockSpec((None, 1, d), lambda b, i: (b, 0, 0)),
                  pl.BlockSpec((1, d), lambda b, i: (0, 0)),
                  pl.BlockSpec(memory_space=pl.ANY)],
        out_specs=pl.BlockSpec((None, tm, d), lambda b, i: (b, i, 0)),
        out_shape=jax.ShapeDtypeStruct((bsz, t, d), F32),
        scratch_shapes=[pltpu.VMEM((2, TOP_K, tm, d), F32), pltpu.SemaphoreType.DMA((2,))],
        compiler_params=_cparams(("arbitrary", "arbitrary")),
        name="combine",
    )(dest_tiles, dest_tiles, x1, w_lanes, g2, final_g, y)


def kernel(x, c, ctx, c_ctx, norm1_g, norm2_g, w_mod, b_mod, w_in, lb_fwd, lb_bwd, gnorm_g, w_s, b_s,
           w_branch_a, w_branch_b, w_out, w_router, b_router, w1, b1, w2, b2, final_g):
    bsz, t, d = x.shape
    assert w_in.shape[0] == 1 and lb_fwd.shape[0] == 2, "single-layer block"
    assert w_in.shape[2] == N_SPLITS * d and d % HEAD_DIM == 0
    assert t % MLP_CHUNK == 0 and ctx.shape[1] % GLA_CHUNK == 0
    n = bsz * t

    pad_rows = (-(bsz + 1)) % 8
    cc = jnp.concatenate([c, c_ctx[None, :], jnp.zeros((pad_rows, d), F32)], axis=0)
    mod = _mod_call(cc, w_mod[0], b_mod[0][None, :])
    lat = mod[:bsz].reshape(bsz, N_MOD, 1, d)
    sh1, sc1, g1, sh2, sc2, g2 = (lat[:, m] for m in range(N_MOD))
    cmod = mod[bsz].reshape(N_MOD, 1, 1, d)
    csh1, csc1 = cmod[0], cmod[1]

    w_in_bf = w_in[0].astype(BF16)
    n1 = norm1_g[0][None, :]
    q, i_, ff, fb, og, u, v, ga, gb = _proj_call(
        x, n1, sh1, sc1, w_in_bf, lb_fwd, lb_bwd,
        splits=tuple(range(N_SPLITS)), tm=min(256, t), per_batch_mod=True)
    ci, cff, cfb = _proj_call(
        ctx, n1, csh1, csc1, w_in_bf, lb_fwd, lb_bwd,
        splits=(S_I, S_FF, S_FB), tm=ctx.shape[1], per_batch_mod=False)

    ya = _gla_call(q, i_, ff, fb, og, ci, cff, cfb, gnorm_g[0][None, :])

    n_exp = w_router.shape[2]
    tm_io = min(256, t)
    exp_rows = -(-n_exp // ROW_GROUP) * ROW_GROUP
    wr_t = jnp.zeros((exp_rows, d), BF16).at[:n_exp].set(w_router[0].T.astype(BF16))
    br_t = jnp.broadcast_to(jnp.full((exp_rows,), NEG_BIG, F32).at[:n_exp].set(b_router[0])[:, None],
                            (exp_rows, tm_io))
    bs_full = jnp.broadcast_to(b_s[0][:, :, None], b_s.shape[1:] + (HEAD_DIM,))
    x1, h2, e_t, w_t, r_t, cnt = _mix_call(
        x, u, v, ga, gb, ya, w_s[0].astype(BF16), bs_full,
        w_branch_a[0].astype(BF16), w_branch_b[0].astype(BF16), w_out[0].astype(BF16),
        g1, norm2_g[0][None, :], sh2, sc2, wr_t, br_t, tm=tm_io)

    counts = cnt[:n_exp, 0].astype(jnp.int32)
    padded = (counts + MOE_BLOCK - 1) // MOE_BLOCK * MOE_BLOCK
    pad_ends = jnp.cumsum(padded)
    pad_starts = pad_ends - padded
    n_blocks = -(-(n * TOP_K) // MOE_BLOCK) + n_exp
    cap = n_blocks * MOE_BLOCK
    dest = pad_starts[e_t[:TOP_K]] + r_t[:TOP_K]
    blk_start = jnp.arange(n_blocks, dtype=jnp.int32) * MOE_BLOCK
    block_e = jnp.minimum(jnp.sum((pad_ends[None, :] <= blk_start[:, None]).astype(jnp.int32), axis=1),
                          n_exp - 1)
    n_used = (pad_ends[-1:] // MOE_BLOCK).astype(jnp.int32)

    dest_tiles = dest.reshape(TOP_K, n // tm_io, tm_io).transpose(1, 0, 2).reshape(n // tm_io, 1, TOP_K * tm_io)
    w_l = jnp.zeros((n, LANES), F32).at[:, :TOP_K].set(w_t[:TOP_K].T)
    xb = _dispatch_call(dest_tiles, h2, _padfill_call(pad_ends, padded, cap, d), tm=tm_io)
    y = _expert_call(block_e, n_used, xb, w1[0], b1[0][:, None, :], w2[0], b2[0][:, None, :])
    return _combine_call(dest_tiles, x1, w_l, g2, final_g[None, :], y, tm=tm_io)
```

```python
import functools

import jax
import jax.numpy as jnp
from jax import lax
from jax.experimental import pallas as pl
from jax.experimental.pallas import tpu as pltpu

F32 = jnp.float32
BF16 = jnp.bfloat16

EPS = 1e-6
N_MOD = 6
S_Q, S_I, S_FF, S_FB, S_OG, S_U, S_V, S_GA, S_GB = range(9)
N_SPLITS = 9
SPLIT_DTYPE = {S_FF: F32, S_FB: F32}
HEAD_DIM = 128
GLA_CHUNK = 64
GLA_HEADS_PER_STEP = 4
MLP_CHUNK = 128
TOP_K = 4
MOE_BLOCK = 256
ROW_GROUP = 8
SWIGLU_LIMIT = 7.0
SWIGLU_ALPHA = 1.702
LANES = 128
NEG_BIG = -1e30
VMEM_LIMIT = 56 * 1024 * 1024


def _cparams(sem):
    return pltpu.CompilerParams(dimension_semantics=sem, vmem_limit_bytes=VMEM_LIMIT)


def _resident(shape):
    zeros = (0,) * len(shape)
    return pl.BlockSpec(shape, lambda *_: zeros, pipeline_mode=pl.Buffered(1))


def _mod_kernel(c_ref, w_ref, b_ref, o_ref):
    c = c_ref[...]
    h = (c * jax.nn.sigmoid(c)).astype(BF16)
    o_ref[...] = jnp.dot(h, w_ref[...].astype(BF16), preferred_element_type=F32) + b_ref[...]


def _mod_call(cc, w_mod, b_mod):
    rows, d = cc.shape
    width = w_mod.shape[1]
    return pl.pallas_call(
        _mod_kernel,
        grid=(width // d,),
        in_specs=[pl.BlockSpec((rows, d), lambda j: (0, 0)),
                  pl.BlockSpec((d, d), lambda j: (0, j)),
                  pl.BlockSpec((1, d), lambda j: (0, j))],
        out_specs=pl.BlockSpec((rows, d), lambda j: (0, j)),
        out_shape=jax.ShapeDtypeStruct((rows, width), F32),
        compiler_params=_cparams(("arbitrary",)),
        name="mod",
    )(cc, w_mod, b_mod)


def _lower_bound(lb_ref):
    a = lb_ref[0:1, :]
    b = lb_ref[1:2, :]
    m = jnp.maximum(a, b)
    ea = jnp.exp(a - m)
    eb = jnp.exp(b - m)
    return ea / (ea + eb)


def _gelu(z):
    return 0.5 * z * (1.0 + lax.erf(z * (2.0 ** -0.5)))


def _split_activation(s, z, lbf_ref, lbb_ref):
    if s in (S_Q, S_OG):
        return z * jax.nn.sigmoid(z)
    if s == S_I:
        return z
    if s in (S_FF, S_FB):
        lb = _lower_bound(lbf_ref if s == S_FF else lbb_ref)
        return lb + (1.0 - lb) * jax.nn.sigmoid(z)
    if s in (S_U, S_V):
        return _gelu(z)
    return jax.nn.sigmoid(z)


def _proj_kernel(x_ref, g_ref, sh_ref, sc_ref, w_ref, lbf_ref, lbb_ref, *o_refs, splits):
    d = x_ref.shape[1]
    x = x_ref[...]
    y = x * lax.rsqrt(jnp.mean(x * x, axis=-1, keepdims=True) + EPS) * g_ref[...]
    h = (y * (1.0 + sc_ref[...]) + sh_ref[...]).astype(BF16)
    for s, o_ref in zip(splits, o_refs):
        z = jnp.dot(h, w_ref[:, s * d:(s + 1) * d], preferred_element_type=F32)
        o_ref[...] = _split_activation(s, z, lbf_ref, lbb_ref).astype(o_ref.dtype)


def _proj_call(x, g, sh, sc, w_bf, lbf, lbb, *, splits, tm, per_batch_mod):
    bsz, t, d = x.shape
    mod_map = (lambda b, i: (b, 0, 0)) if per_batch_mod else (lambda b, i: (0, 0, 0))
    row = pl.BlockSpec((None, tm, d), lambda b, i: (b, i, 0))
    return pl.pallas_call(
        functools.partial(_proj_kernel, splits=splits),
        grid=(bsz, t // tm),
        in_specs=[row,
                  pl.BlockSpec((1, d), lambda b, i: (0, 0)),
                  pl.BlockSpec((None, 1, d), mod_map),
                  pl.BlockSpec((None, 1, d), mod_map),
                  _resident(w_bf.shape),
                  pl.BlockSpec((2, d), lambda b, i: (0, 0)),
                  pl.BlockSpec((2, d), lambda b, i: (0, 0))],
        out_specs=[row for _ in splits],
        out_shape=[jax.ShapeDtypeStruct((bsz, t, d), SPLIT_DTYPE.get(s, BF16)) for s in splits],
        compiler_params=_cparams(("arbitrary", "arbitrary")),
        name="proj",
    )(x, g, sh, sc, w_bf, lbf, lbb)


def _prefix_sum(x, reverse):
    n = x.shape[0]
    row = lax.broadcasted_iota(jnp.int32, x.shape, 0)
    s = 1
    while s < n:
        if reverse:
            x = x + jnp.where(row < n - s, pltpu.roll(x, n - s, 0), 0.0)
        else:
            x = x + jnp.where(row >= s, pltpu.roll(x, s, 0), 0.0)
        s *= 2
    return x


def _dot_nt(a, b):
    return lax.dot_general(a, b, (((1,), (1,)), ((), ())), preferred_element_type=F32)


def _dot_tn(a, b):
    return lax.dot_general(a, b, (((0,), (0,)), ((), ())), preferred_element_type=F32)


def _gla_prepare(q, f, reverse):
    c = q.shape[0]
    b = _prefix_sum(jnp.log(f), reverse)
    if reverse:
        b_mid = b[c // 2:c // 2 + 1, :]
        b_end = b[0:1, :]
    else:
        b_mid = b[c // 2 - 1:c // 2, :]
        b_end = b[c - 1:c, :]
    qe = q * jnp.exp(b - b_mid)
    ke = (1.0 - f) * jnp.exp(b_mid - b)
    kd = ke * jnp.exp(b_end - b_mid)
    return qe.astype(BF16), ke.astype(BF16), kd.astype(BF16), jnp.exp(b_mid), jnp.exp(b_end)


def _gla_scores(qe, ke, st, e_mid):
    rhs = jnp.concatenate([(st * e_mid).astype(BF16), ke], axis=0)
    return _dot_nt(qe, rhs)


def _gla_finish(both, kd, vb, st, e_end, reverse):
    c = both.shape[0]
    dv = st.shape[0]
    ti = lax.broadcasted_iota(jnp.int32, (c, c), 0)
    si = lax.broadcasted_iota(jnp.int32, (c, c), 1)
    keep = (si >= ti) if reverse else (si <= ti)
    scores = jnp.where(keep, both[:, dv:], 0.0)
    o = both[:, :dv] + jnp.dot(scores.astype(BF16), vb, preferred_element_type=F32)
    return o, st * e_end + _dot_tn(vb, kd)


def _context_state(f, vb, reverse_decay):
    g = jnp.log(f)
    tail = _prefix_sum(g, reverse_decay) - g
    kt = ((1.0 - f) * jnp.exp(tail)).astype(BF16)
    return _dot_tn(vb, kt)


def _gla_kernel(q_ref, i_ref, ff_ref, fb_ref, og_ref, ci_ref, cff_ref, cfb_ref, gn_ref,
                o_ref, of_ref, ob_ref, sf_ref, sb_ref, ops_f, ops_b, rows_f, rows_b):
    t = q_ref.shape[0]
    c = GLA_CHUNK
    n = t // c
    heads = [slice(h * HEAD_DIM, (h + 1) * HEAD_DIM) for h in range(q_ref.shape[1] // HEAD_DIM)]
    for h, cols in enumerate(heads):
        ci = ci_ref[:, cols]
        sf_ref[h] = _context_state(cff_ref[:, cols], ci, True)
        sb_ref[h] = _context_state(cfb_ref[:, cols], ci, False)

    directions = ((ff_ref, ops_f, rows_f, sf_ref, of_ref, False), (fb_ref, ops_b, rows_b, sb_ref, ob_ref, True))

    def chunk_rows(ck):
        return pl.ds(pl.multiple_of(ck * c, c), c)

    def prepare(step):
        slot = step % 2
        for cols in heads:
            for f_ref, ops, rows, _, _, reverse in directions:
                r = chunk_rows((n - 1 - step) if reverse else step)
                qe, ke, kd, e_mid, e_end = _gla_prepare(q_ref[r, cols].astype(F32), f_ref[r, cols], reverse)
                ops[slot, 0, :, cols] = qe
                ops[slot, 1, :, cols] = ke
                ops[slot, 2, :, cols] = kd
                rows[slot, 0, :, cols] = jnp.broadcast_to(e_mid, (8, HEAD_DIM))
                rows[slot, 1, :, cols] = jnp.broadcast_to(e_end, (8, HEAD_DIM))

    prepare(0)

    def scan(step, carry):
        slot = step % 2
        chains = []
        for h, cols in enumerate(heads):
            for _, ops, rows, s_ref, acc, reverse in directions:
                chains.append((h, cols, ops, rows, s_ref, acc, reverse,
                               chunk_rows((n - 1 - step) if reverse else step)))
        firsts = []
        for h, cols, ops, rows, s_ref, acc, reverse, r in chains:
            st = s_ref[h]
            firsts.append((st, _gla_scores(ops[slot, 0, :, cols], ops[slot, 1, :, cols], st,
                                           rows[slot, 0, 0:1, cols])))
        for (h, cols, ops, rows, s_ref, acc, reverse, r), (st, both) in zip(chains, firsts):
            o, st = _gla_finish(both, ops[slot, 2, :, cols], i_ref[r, cols], st, rows[slot, 1, 0:1, cols],
                                reverse)
            acc[r, cols] = o
            s_ref[h] = st
        prepare(jnp.minimum(step + 1, n - 1))
        return carry

    lax.fori_loop(0, n, scan, 0)
    for h, cols in enumerate(heads):
        o = of_ref[:, cols] + ob_ref[:, cols]
        y = o * lax.rsqrt(jnp.mean(o * o, axis=-1, keepdims=True) + EPS) * gn_ref[...]
        o_ref[:, cols] = (y * og_ref[:, cols].astype(F32)).astype(o_ref.dtype)


def _gla_call(q, i, ff, fb, og, ci, cff, cfb, gnorm):
    bsz, t, d = q.shape
    width = GLA_HEADS_PER_STEP * HEAD_DIM
    tc = ci.shape[1]
    lat = pl.BlockSpec((None, t, width), lambda b, h: (b, 0, h))
    ctx = pl.BlockSpec((None, tc, width), lambda b, h: (b, 0, h))
    out_acc = pltpu.VMEM((t, width), F32)
    state = pltpu.VMEM((GLA_HEADS_PER_STEP, HEAD_DIM, HEAD_DIM), F32)
    operands = pltpu.VMEM((2, 3, GLA_CHUNK, width), BF16)
    decay_rows = pltpu.VMEM((2, 2, 8, width), F32)
    return pl.pallas_call(
        _gla_kernel,
        grid=(bsz, d // width),
        in_specs=[lat, lat, lat, lat, lat, ctx, ctx, ctx,
                  pl.BlockSpec((1, HEAD_DIM), lambda b, h: (0, 0))],
        out_specs=lat,
        out_shape=jax.ShapeDtypeStruct((bsz, t, d), BF16),
        scratch_shapes=[out_acc, out_acc, state, state, operands, operands, decay_rows, decay_rows],
        compiler_params=_cparams(("arbitrary", "arbitrary")),
        name="gla",
    )(q, i, ff, fb, og, ci, cff, cfb, gnorm)


def _mix_kernel(x_ref, u_ref, v_ref, ga_ref, gb_ref, ya_ref, ws_ref, bs_ref, wa_ref, wb_ref, wo_ref,
                g1_ref, n2_ref, sh2_ref, sc2_ref, wr_ref, br_ref,
                x1_ref, h2_ref, e_ref, w_ref, rank_ref, cnt_ref, yb_ref, carry_ref):
    tm, d = x_ref.shape
    first = (pl.program_id(0) == 0) & (pl.program_id(1) == 0)

    @pl.when(first)
    def _():
        carry_ref[...] = jnp.zeros_like(carry_ref)

    v = v_ref[...].astype(F32)
    mu = jnp.mean(v, axis=-1, keepdims=True)
    vc = v - mu
    var = jnp.mean(vc * vc, axis=-1, keepdims=True)
    vn = (vc * lax.rsqrt(var + EPS)).astype(BF16)
    for ck in range(tm // MLP_CHUNK):
        rows = slice(ck * MLP_CHUNK, (ck + 1) * MLP_CHUNK)
        for g in range(d // HEAD_DIM):
            cols = slice(g * HEAD_DIM, (g + 1) * HEAD_DIM)
            s = jnp.dot(ws_ref[g], vn[rows, cols], preferred_element_type=F32) + bs_ref[g]
            yb_ref[rows, cols] = (u_ref[rows, cols].astype(F32) * s).astype(BF16)

    pa = jnp.dot(ya_ref[...], wa_ref[...], preferred_element_type=F32)
    pb = jnp.dot(yb_ref[...], wb_ref[...], preferred_element_type=F32)
    merged = ga_ref[...].astype(F32) * pa + gb_ref[...].astype(F32) * pb
    mixed = jnp.dot(merged.astype(BF16), wo_ref[...], preferred_element_type=F32)
    x1 = x_ref[...] + g1_ref[...] * mixed
    x1_ref[...] = x1

    y = x1 * lax.rsqrt(jnp.mean(x1 * x1, axis=-1, keepdims=True) + EPS) * n2_ref[...]
    h2 = y * (1.0 + sc2_ref[...]) + sh2_ref[...]
    h2_ref[...] = h2

    logits = _dot_nt(wr_ref[...], h2.astype(BF16)) + br_ref[...]
    n_rows = logits.shape[0]
    eidx = lax.broadcasted_iota(jnp.int32, (n_rows, tm), 0)
    work = logits
    vals, idxs = [], []
    for _ in range(TOP_K):
        m = jnp.max(work, axis=0, keepdims=True)
        idx = jnp.min(jnp.where(work == m, eidx, n_rows), axis=0, keepdims=True)
        vals.append(m)
        idxs.append(idx)
        work = jnp.where(eidx == idx, -jnp.inf, work)
    exps = [jnp.exp(m - vals[0]) for m in vals]
    denom = exps[0] + exps[1] + exps[2] + exps[3]

    sel = jnp.zeros((n_rows, tm), F32)
    for idx in idxs:
        sel = sel + jnp.where(eidx == idx, 1.0, 0.0)
    tj = lax.broadcasted_iota(jnp.int32, (tm, tm), 0)
    tt = lax.broadcasted_iota(jnp.int32, (tm, tm), 1)
    tri = jnp.where(tj < tt, 1.0, 0.0).astype(BF16)
    before = carry_ref[...] + jnp.dot(sel.astype(BF16), tri, preferred_element_type=F32)

    krow = lax.broadcasted_iota(jnp.int32, e_ref.shape, 0)
    e_out = jnp.zeros(e_ref.shape, jnp.int32)
    w_out = jnp.zeros(e_ref.shape, F32)
    r_out = jnp.zeros(e_ref.shape, F32)
    for k in range(TOP_K):
        rk = jnp.sum(jnp.where(eidx == idxs[k], before, 0.0), axis=0, keepdims=True)
        e_out = jnp.where(krow == k, idxs[k], e_out)
        w_out = jnp.where(krow == k, exps[k] / denom, w_out)
        r_out = jnp.where(krow == k, rk, r_out)
    e_ref[...] = e_out
    w_ref[...] = w_out
    rank_ref[...] = r_out.astype(jnp.int32)
    carry_ref[...] = carry_ref[...] + jnp.sum(sel, axis=1, keepdims=True)
    cnt_ref[...] = carry_ref[...]


def _mix_call(x, u, v, ga, gb, ya, ws_bf, bs_full, wa_bf, wb_bf, wo_bf, g1, n2, sh2, sc2, wr_t, br_t, *, tm):
    bsz, t, d = x.shape
    n = bsz * t
    nt = t // tm
    n_rows = wr_t.shape[0]
    row = pl.BlockSpec((None, tm, d), lambda b, i: (b, i, 0))
    per_batch = pl.BlockSpec((None, 1, d), lambda b, i: (b, 0, 0))
    per_token = pl.BlockSpec((ROW_GROUP, tm), lambda b, i: (0, b * nt + i))

    return pl.pallas_call(
        _mix_kernel,
        grid=(bsz, nt),
        in_specs=[row, row, row, row, row, row,
                  _resident(ws_bf.shape), _resident(bs_full.shape),
                  _resident((d, d)), _resident((d, d)), _resident((d, d)),
                  per_batch, _resident((1, d)), per_batch, per_batch,
                  _resident(wr_t.shape), _resident(br_t.shape)],
        out_specs=[row, pl.BlockSpec((tm, d), lambda b, i: (b * nt + i, 0)),
                   per_token, per_token, per_token,
                   pl.BlockSpec((n_rows, tm), lambda b, i: (0, 0))],
        out_shape=[jax.ShapeDtypeStruct((bsz, t, d), F32),
                   jax.ShapeDtypeStruct((n, d), F32),
                   jax.ShapeDtypeStruct((ROW_GROUP, n), jnp.int32),
                   jax.ShapeDtypeStruct((ROW_GROUP, n), F32),
                   jax.ShapeDtypeStruct((ROW_GROUP, n), jnp.int32),
                   jax.ShapeDtypeStruct((n_rows, tm), F32)],
        scratch_shapes=[pltpu.VMEM((tm, d), BF16), pltpu.VMEM((n_rows, tm), F32)],
        compiler_params=_cparams(("arbitrary", "arbitrary")),
        name="mix",
    )(x, u, v, ga, gb, ya, ws_bf, bs_full, wa_bf, wb_bf, wo_bf, g1, n2, sh2, sc2, wr_t, br_t)


def _slot_kernel(starts_ref, e_ref, r_ref, o_ref):
    e = e_ref[...]
    start = jnp.zeros(e.shape, jnp.int32)
    for x in range(starts_ref.shape[0]):
        start = jnp.where(e == x, starts_ref[x], start)
    o_ref[...] = start + r_ref[...]


def _slot_call(pad_starts, e_t, r_t, *, tn):
    rows, n = e_t.shape
    blk = pl.BlockSpec((rows, tn), lambda i, starts: (0, i))
    return pl.pallas_call(
        _slot_kernel,
        grid_spec=pltpu.PrefetchScalarGridSpec(
            num_scalar_prefetch=1, grid=(n // tn,), in_specs=[blk, blk], out_specs=blk),
        out_shape=jax.ShapeDtypeStruct((rows, n), jnp.int32),
        compiler_params=_cparams(("arbitrary",)),
        name="slots",
    )(pad_starts, e_t, r_t)


def _padfill_kernel(ends_ref, padded_ref, xb_ref, zero_ref, sem):
    zero_ref[...] = jnp.zeros_like(zero_ref)
    n_exp = ends_ref.shape[0]

    def tail_copy(e):
        start = pl.multiple_of(ends_ref[e] - MOE_BLOCK, MOE_BLOCK)
        return pltpu.make_async_copy(zero_ref, xb_ref.at[pl.ds(start, MOE_BLOCK)], sem)

    for e in range(n_exp):
        @pl.when(padded_ref[e] > 0)
        def _():
            tail_copy(e).start()
    for e in range(n_exp):
        @pl.when(padded_ref[e] > 0)
        def _():
            tail_copy(e).wait()


def _padfill_call(pad_ends, padded, cap, d):
    return pl.pallas_call(
        _padfill_kernel,
        grid_spec=pltpu.PrefetchScalarGridSpec(
            num_scalar_prefetch=2, grid=(1,), in_specs=[],
            out_specs=pl.BlockSpec(memory_space=pl.ANY),
            scratch_shapes=[pltpu.VMEM((MOE_BLOCK, d), F32), pltpu.SemaphoreType.DMA]),
        out_shape=jax.ShapeDtypeStruct((cap, d), F32),
        compiler_params=_cparams(("arbitrary",)),
        name="padfill",
    )(pad_ends, padded)


def _row_copy(src_ref, src_row, dst_ref, dst_row, sem):
    return pltpu.make_async_copy(src_ref.at[pl.ds(src_row, 1)], dst_ref.at[pl.ds(dst_row, 1)], sem)


def _for_row_groups(n_rows, fn):
    def body(g, carry):
        base = pl.multiple_of(g * ROW_GROUP, ROW_GROUP)
        for rr in range(ROW_GROUP):
            fn(base + rr)
        return carry
    lax.fori_loop(0, n_rows // ROW_GROUP, body, 0)


DISPATCH_RING = 3


def _dispatch_kernel(dest_ref, h_ref, xin_ref, xb_ref, hbuf, load_sems, row_sems):
    del xin_ref
    tm = hbuf.shape[1]
    i = pl.program_id(0)
    last = pl.num_programs(0) - 1
    slot = i % DISPATCH_RING

    def load(tile, s):
        start = pl.multiple_of(tile * tm, tm)
        return pltpu.make_async_copy(h_ref.at[pl.ds(start, tm)], hbuf.at[s], load_sems.at[s])

    def drain(s):
        for _ in range(TOP_K):
            pltpu.make_async_copy(hbuf.at[s], xb_ref.at[pl.ds(0, tm)], row_sems.at[s]).wait()

    @pl.when(i == 0)
    def _():
        load(0, 0).start()

    load(i, slot).wait()

    @pl.when(i < last)
    def _():
        load(i + 1, (i + 1) % DISPATCH_RING).start()

    def issue(r):
        for k in range(TOP_K):
            _row_copy(hbuf.at[slot], r, xb_ref, dest_ref[0, 0, k * tm + r],
                      row_sems.at[slot]).start(priority=k % 2)

    _for_row_groups(tm, issue)

    @pl.when(i >= 1)
    def _():
        drain((i + DISPATCH_RING - 1) % DISPATCH_RING)

    @pl.when(i == last)
    def _():
        drain(slot)


def _dispatch_call(dest_tiles, h2, xb_init, *, tm):
    n, d = h2.shape
    return pl.pallas_call(
        _dispatch_kernel,
        grid=(n // tm,),
        in_specs=[pl.BlockSpec((1, 1, tm * TOP_K), lambda i: (i, 0, 0), memory_space=pltpu.SMEM),
                  pl.BlockSpec(memory_space=pl.ANY),
                  pl.BlockSpec(memory_space=pl.ANY)],
        out_specs=pl.BlockSpec(memory_space=pl.ANY),
        out_shape=jax.ShapeDtypeStruct(xb_init.shape, F32),
        scratch_shapes=[pltpu.VMEM((DISPATCH_RING, tm, d), F32),
                        pltpu.SemaphoreType.DMA((DISPATCH_RING,)), pltpu.SemaphoreType.DMA((DISPATCH_RING,))],
        input_output_aliases={2: 0},
        compiler_params=_cparams(("arbitrary",)),
        name="dispatch",
    )(dest_tiles, h2, xb_init)


def _expert_kernel(be_ref, nu_ref, x_ref, w1_ref, b1_ref, w2_ref, b2_ref, y_ref, w1b_ref, w2b_ref):
    i = pl.program_id(0)
    f = w2_ref.shape[1]
    active = i < nu_ref[0]

    @pl.when(active & ((i == 0) | (be_ref[i] != be_ref[jnp.maximum(i - 1, 0)])))
    def _():
        w1b_ref[...] = w1_ref[0].astype(BF16)
        w2b_ref[...] = w2_ref[0].astype(BF16)

    @pl.when(active)
    def _():
        z = jnp.dot(x_ref[...].astype(BF16), w1b_ref[...], preferred_element_type=F32) + b1_ref[0]
        gate = jnp.minimum(z[:, :f], SWIGLU_LIMIT)
        lin = jnp.clip(z[:, f:], -SWIGLU_LIMIT, SWIGLU_LIMIT)
        act = gate * jax.nn.sigmoid(SWIGLU_ALPHA * gate) * (lin + 1.0)
        y_ref[...] = jnp.dot(act.astype(BF16), w2b_ref[...], preferred_element_type=F32) + b2_ref[0]


def _expert_call(block_e, n_used, xb, w1, b1, w2, b2):
    cap, d = xb.shape
    n_blocks = cap // MOE_BLOCK
    f2 = w1.shape[2]
    f = w2.shape[1]

    def blk(i, be, nu):
        return jnp.minimum(i, nu[0] - 1)

    grid_spec = pltpu.PrefetchScalarGridSpec(
        num_scalar_prefetch=2,
        grid=(n_blocks,),
        in_specs=[pl.BlockSpec((MOE_BLOCK, d), lambda i, be, nu: (blk(i, be, nu), 0)),
                  pl.BlockSpec((1, d, f2), lambda i, be, nu: (be[blk(i, be, nu)], 0, 0)),
                  pl.BlockSpec((1, 1, f2), lambda i, be, nu: (be[blk(i, be, nu)], 0, 0)),
                  pl.BlockSpec((1, f, d), lambda i, be, nu: (be[blk(i, be, nu)], 0, 0)),
                  pl.BlockSpec((1, 1, d), lambda i, be, nu: (be[blk(i, be, nu)], 0, 0))],
        out_specs=pl.BlockSpec((MOE_BLOCK, d), lambda i, be, nu: (blk(i, be, nu), 0)),
        scratch_shapes=[pltpu.VMEM((d, f2), BF16), pltpu.VMEM((f, d), BF16)],
    )
    return pl.pallas_call(
        _expert_kernel,
        grid_spec=grid_spec,
        out_shape=jax.ShapeDtypeStruct((cap, d), F32),
        compiler_params=_cparams(("arbitrary",)),
        name="experts",
    )(block_e, n_used, xb, w1, b1, w2, b2)


def _combine_kernel(dest_ref, nxt_ref, x1_ref, w_ref, g2_ref, fg_ref, y_ref, o_ref, buf_ref, sems):
    tm = x1_ref.shape[0]
    nt = pl.num_programs(1)
    i = pl.program_id(0) * nt + pl.program_id(1)
    last = pl.num_programs(0) * nt - 1
    cur = i % 2

    def gather(tab_ref, slot):
        def issue(r):
            for k in range(TOP_K):
                _row_copy(y_ref, tab_ref[0, 0, k * tm + r], buf_ref.at[slot, k], r,
                          sems.at[slot]).start(priority=k % 2)
        _for_row_groups(tm, issue)

    @pl.when(i == 0)
    def _():
        gather(dest_ref, 0)

    @pl.when(i < last)
    def _():
        gather(nxt_ref, 1 - cur)

    for k in range(TOP_K):
        pltpu.make_async_copy(y_ref.at[pl.ds(0, tm)], buf_ref.at[cur, k], sems.at[cur]).wait()

    w = w_ref[...]
    moe = w[:, 0:1] * buf_ref[cur, 0]
    for k in range(1, TOP_K):
        moe = moe + w[:, k:k + 1] * buf_ref[cur, k]
    x2 = x1_ref[...] + g2_ref[...] * moe
    o_ref[...] = x2 * lax.rsqrt(jnp.mean(x2 * x2, axis=-1, keepdims=True) + EPS) * fg_ref[...]


def _combine_call(dest_tiles, x1, w_lanes, g2, final_g, y, *, tm):
    bsz, t, d = x1.shape
    nt = t // tm
    tiles = bsz * nt

    def table(shift):
        return pl.BlockSpec((1, 1, tm * TOP_K), lambda b, i: (jnp.minimum(b * nt + i + shift, tiles - 1), 0, 0),
                            memory_space=pltpu.SMEM)

    return pl.pallas_call(
        _combine_kernel,
        grid=(bsz, nt),
        in_specs=[table(0), table(1),
                  pl.BlockSpec((None, tm, d), lambda b, i: (b, i, 0)),
                  pl.BlockSpec((tm, LANES), lambda b, i: (b * nt + i, 0)),
                  pl.BlockSpec((None, 1, d), lambda b, i: (b, 0, 0)),
                  pl.BlockSpec((1, d), lambda b, i: (0, 0)),
                  pl.BlockSpec(memory_space=pl.ANY)],
        out_specs=pl.BlockSpec((None, tm, d), lambda b, i: (b, i, 0)),
        out_shape=jax.ShapeDtypeStruct((bsz, t, d), F32),
        scratch_shapes=[pltpu.VMEM((2, TOP_K, tm, d), F32), pltpu.SemaphoreType.DMA((2,))],
        compiler_params=_cparams(("arbitrary", "arbitrary")),
        name="combine",
    )(dest_tiles, dest_tiles, x1, w_lanes, g2, final_g, y)


def kernel(x, c, ctx, c_ctx, norm1_g, norm2_g, w_mod, b_mod, w_in, lb_fwd, lb_bwd, gnorm_g, w_s, b_s,
           w_branch_a, w_branch_b, w_out, w_router, b_router, w1, b1, w2, b2, final_g):
    bsz, t, d = x.shape
    assert w_in.shape[0] == 1 and lb_fwd.shape[0] == 2, "single-layer block"
    assert w_in.shape[2] == N_SPLITS * d and d % HEAD_DIM == 0
    assert t % MLP_CHUNK == 0 and ctx.shape[1] % GLA_CHUNK == 0
    n = bsz * t

    pad_rows = (-(bsz + 1)) % 8
    cc = jnp.concatenate([c, c_ctx[None, :], jnp.zeros((pad_rows, d), F32)], axis=0)
    mod = _mod_call(cc, w_mod[0], b_mod[0][None, :])
    lat = mod[:bsz].reshape(bsz, N_MOD, 1, d)
    sh1, sc1, g1, sh2, sc2, g2 = (lat[:, m] for m in range(N_MOD))
    cmod = mod[bsz].reshape(N_MOD, 1, 1, d)
    csh1, csc1 = cmod[0], cmod[1]

    w_in_bf = w_in[0].astype(BF16)
    n1 = norm1_g[0][None, :]
    q, i_, ff, fb, og, u, v, ga, gb = _proj_call(
        x, n1, sh1, sc1, w_in_bf, lb_fwd, lb_bwd,
        splits=tuple(range(N_SPLITS)), tm=min(256, t), per_batch_mod=True)
    ci, cff, cfb = _proj_call(
        ctx, n1, csh1, csc1, w_in_bf, lb_fwd, lb_bwd,
        splits=(S_I, S_FF, S_FB), tm=ctx.shape[1], per_batch_mod=False)

    ya = _gla_call(q, i_, ff, fb, og, ci, cff, cfb, gnorm_g[0][None, :])

    n_exp = w_router.shape[2]
    tm_io = min(256, t)
    exp_rows = -(-n_exp // ROW_GROUP) * ROW_GROUP
    wr_t = jnp.zeros((exp_rows, d), BF16).at[:n_exp].set(w_router[0].T.astype(BF16))
    br_t = jnp.broadcast_to(jnp.full((exp_rows,), NEG_BIG, F32).at[:n_exp].set(b_router[0])[:, None],
                            (exp_rows, tm_io))
    bs_full = jnp.broadcast_to(b_s[0][:, :, None], b_s.shape[1:] + (HEAD_DIM,))
    x1, h2, e_t, w_t, r_t, cnt = _mix_call(
        x, u, v, ga, gb, ya, w_s[0].astype(BF16), bs_full,
        w_branch_a[0].astype(BF16), w_branch_b[0].astype(BF16), w_out[0].astype(BF16),
        g1, norm2_g[0][None, :], sh2, sc2, wr_t, br_t, tm=tm_io)

    counts = cnt[:n_exp, 0].astype(jnp.int32)
    padded = (counts + MOE_BLOCK - 1) // MOE_BLOCK * MOE_BLOCK
    pad_ends = jnp.cumsum(padded)
    pad_starts = pad_ends - padded
    n_blocks = -(-(n * TOP_K) // MOE_BLOCK) + n_exp
    cap = n_blocks * MOE_BLOCK
    dest = _slot_call(pad_starts, e_t, r_t, tn=min(8192, n))[:TOP_K]
    blk_start = jnp.arange(n_blocks, dtype=jnp.int32) * MOE_BLOCK
    block_e = jnp.minimum(jnp.sum((pad_ends[None, :] <= blk_start[:, None]).astype(jnp.int32), axis=1),
                          n_exp - 1)
    n_used = (pad_ends[-1:] // MOE_BLOCK).astype(jnp.int32)

    dest_tiles = dest.reshape(TOP_K, n // tm_io, tm_io).transpose(1, 0, 2).reshape(n // tm_io, 1, TOP_K * tm_io)
    w_l = jnp.zeros((n, LANES), F32).at[:, :TOP_K].set(w_t[:TOP_K].T)
    xb = _dispatch_call(dest_tiles, h2, _padfill_call(pad_ends, padded, cap, d), tm=tm_io)
    y = _expert_call(block_e, n_used, xb, w1[0], b1[0][:, None, :], w2[0], b2[0][:, None, :])
    return _combine_call(dest_tiles, x1, w_l, g2, final_g[None, :], y, tm=tm_io)
```

```python
import functools

import jax
import jax.numpy as jnp
from jax import lax
from jax.experimental import pallas as pl
from jax.experimental.pallas import tpu as pltpu

F32 = jnp.float32
BF16 = jnp.bfloat16

EPS = 1e-6
N_MOD = 6
S_Q, S_I, S_FF, S_FB, S_OG, S_U, S_V, S_GA, S_GB = range(9)
N_SPLITS = 9
SPLIT_DTYPE = {S_FF: F32, S_FB: F32}
HEAD_DIM = 128
GLA_CHUNK = 64
GLA_HEADS_PER_STEP = 4
MLP_CHUNK = 128
TOP_K = 4
MOE_BLOCK = 512
ROW_GROUP = 8
SWIGLU_LIMIT = 7.0
SWIGLU_ALPHA = 1.702
LANES = 128
NEG_BIG = -1e30
VMEM_LIMIT = 56 * 1024 * 1024


def _cparams(sem):
    return pltpu.CompilerParams(dimension_semantics=sem, vmem_limit_bytes=VMEM_LIMIT)


def _resident(shape):
    zeros = (0,) * len(shape)
    return pl.BlockSpec(shape, lambda *_: zeros, pipeline_mode=pl.Buffered(1))


def _mod_kernel(c_ref, w_ref, b_ref, o_ref):
    c = c_ref[...]
    h = (c * jax.nn.sigmoid(c)).astype(BF16)
    o_ref[...] = jnp.dot(h, w_ref[...].astype(BF16), preferred_element_type=F32) + b_ref[...]


def _mod_call(cc, w_mod, b_mod):
    rows, d = cc.shape
    width = w_mod.shape[1]
    return pl.pallas_call(
        _mod_kernel,
        grid=(width // d,),
        in_specs=[pl.BlockSpec((rows, d), lambda j: (0, 0)),
                  pl.BlockSpec((d, d), lambda j: (0, j)),
                  pl.BlockSpec((1, d), lambda j: (0, j))],
        out_specs=pl.BlockSpec((rows, d), lambda j: (0, j)),
        out_shape=jax.ShapeDtypeStruct((rows, width), F32),
        compiler_params=_cparams(("arbitrary",)),
        name="mod",
    )(cc, w_mod, b_mod)


def _lower_bound(lb_ref):
    a = lb_ref[0:1, :]
    b = lb_ref[1:2, :]
    m = jnp.maximum(a, b)
    ea = jnp.exp(a - m)
    eb = jnp.exp(b - m)
    return ea / (ea + eb)


def _gelu(z):
    return 0.5 * z * (1.0 + lax.erf(z * (2.0 ** -0.5)))


def _split_activation(s, z, lbf_ref, lbb_ref):
    if s in (S_Q, S_OG):
        return z * jax.nn.sigmoid(z)
    if s == S_I:
        return z
    if s in (S_FF, S_FB):
        lb = _lower_bound(lbf_ref if s == S_FF else lbb_ref)
        return lb + (1.0 - lb) * jax.nn.sigmoid(z)
    if s in (S_U, S_V):
        return _gelu(z)
    return jax.nn.sigmoid(z)


def _proj_kernel(x_ref, g_ref, sh_ref, sc_ref, w_ref, lbf_ref, lbb_ref, *o_refs, splits):
    d = x_ref.shape[1]
    x = x_ref[...]
    y = x * lax.rsqrt(jnp.mean(x * x, axis=-1, keepdims=True) + EPS) * g_ref[...]
    h = (y * (1.0 + sc_ref[...]) + sh_ref[...]).astype(BF16)
    for s, o_ref in zip(splits, o_refs):
        z = jnp.dot(h, w_ref[:, s * d:(s + 1) * d], preferred_element_type=F32)
        o_ref[...] = _split_activation(s, z, lbf_ref, lbb_ref).astype(o_ref.dtype)


def _proj_call(x, g, sh, sc, w_bf, lbf, lbb, *, splits, tm, per_batch_mod):
    bsz, t, d = x.shape
    mod_map = (lambda b, i: (b, 0, 0)) if per_batch_mod else (lambda b, i: (0, 0, 0))
    row = pl.BlockSpec((None, tm, d), lambda b, i: (b, i, 0))
    return pl.pallas_call(
        functools.partial(_proj_kernel, splits=splits),
        grid=(bsz, t // tm),
        in_specs=[row,
                  pl.BlockSpec((1, d), lambda b, i: (0, 0)),
                  pl.BlockSpec((None, 1, d), mod_map),
                  pl.BlockSpec((None, 1, d), mod_map),
                  _resident(w_bf.shape),
                  pl.BlockSpec((2, d), lambda b, i: (0, 0)),
                  pl.BlockSpec((2, d), lambda b, i: (0, 0))],
        out_specs=[row for _ in splits],
        out_shape=[jax.ShapeDtypeStruct((bsz, t, d), SPLIT_DTYPE.get(s, BF16)) for s in splits],
        compiler_params=_cparams(("arbitrary", "arbitrary")),
        name="proj",
    )(x, g, sh, sc, w_bf, lbf, lbb)


def _prefix_sum(x, reverse):
    n = x.shape[0]
    row = lax.broadcasted_iota(jnp.int32, x.shape, 0)
    s = 1
    while s < n:
        if reverse:
            x = x + jnp.where(row < n - s, pltpu.roll(x, n - s, 0), 0.0)
        else:
            x = x + jnp.where(row >= s, pltpu.roll(x, s, 0), 0.0)
        s *= 2
    return x


def _dot_nt(a, b):
    return lax.dot_general(a, b, (((1,), (1,)), ((), ())), preferred_element_type=F32)


def _dot_tn(a, b):
    return lax.dot_general(a, b, (((0,), (0,)), ((), ())), preferred_element_type=F32)


def _gla_prepare(q, f, reverse):
    c = q.shape[0]
    b = _prefix_sum(jnp.log(f), reverse)
    if reverse:
        b_mid = b[c // 2:c // 2 + 1, :]
        b_end = b[0:1, :]
    else:
        b_mid = b[c // 2 - 1:c // 2, :]
        b_end = b[c - 1:c, :]
    qe = q * jnp.exp(b - b_mid)
    ke = (1.0 - f) * jnp.exp(b_mid - b)
    kd = ke * jnp.exp(b_end - b_mid)
    return qe.astype(BF16), ke.astype(BF16), kd.astype(BF16), jnp.exp(b_mid), jnp.exp(b_end)


def _gla_scores(qe, ke, st, e_mid):
    rhs = jnp.concatenate([(st * e_mid).astype(BF16), ke], axis=0)
    return _dot_nt(qe, rhs)


def _gla_finish(both, kd, vb, st, e_end, reverse):
    c = both.shape[0]
    dv = st.shape[0]
    ti = lax.broadcasted_iota(jnp.int32, (c, c), 0)
    si = lax.broadcasted_iota(jnp.int32, (c, c), 1)
    keep = (si >= ti) if reverse else (si <= ti)
    scores = jnp.where(keep, both[:, dv:], 0.0)
    o = both[:, :dv] + jnp.dot(scores.astype(BF16), vb, preferred_element_type=F32)
    return o, st * e_end + _dot_tn(vb, kd)


def _context_state(f, vb, reverse_decay):
    g = jnp.log(f)
    tail = _prefix_sum(g, reverse_decay) - g
    kt = ((1.0 - f) * jnp.exp(tail)).astype(BF16)
    return _dot_tn(vb, kt)


def _gla_kernel(q_ref, i_ref, ff_ref, fb_ref, og_ref, ci_ref, cff_ref, cfb_ref, gn_ref,
                o_ref, of_ref, ob_ref, sf_ref, sb_ref, ops_f, ops_b, rows_f, rows_b):
    t = q_ref.shape[0]
    c = GLA_CHUNK
    n = t // c
    heads = [slice(h * HEAD_DIM, (h + 1) * HEAD_DIM) for h in range(q_ref.shape[1] // HEAD_DIM)]
    for h, cols in enumerate(heads):
        ci = ci_ref[:, cols]
        sf_ref[h] = _context_state(cff_ref[:, cols], ci, True)
        sb_ref[h] = _context_state(cfb_ref[:, cols], ci, False)

    directions = ((ff_ref, ops_f, rows_f, sf_ref, of_ref, False), (fb_ref, ops_b, rows_b, sb_ref, ob_ref, True))

    def chunk_rows(ck):
        return pl.ds(pl.multiple_of(ck * c, c), c)

    def prepare(step):
        slot = step % 2
        for cols in heads:
            for f_ref, ops, rows, _, _, reverse in directions:
                r = chunk_rows((n - 1 - step) if reverse else step)
                qe, ke, kd, e_mid, e_end = _gla_prepare(q_ref[r, cols].astype(F32), f_ref[r, cols], reverse)
                ops[slot, 0, :, cols] = qe
                ops[slot, 1, :, cols] = ke
                ops[slot, 2, :, cols] = kd
                rows[slot, 0, :, cols] = jnp.broadcast_to(e_mid, (8, HEAD_DIM))
                rows[slot, 1, :, cols] = jnp.broadcast_to(e_end, (8, HEAD_DIM))

    prepare(0)

    def scan(step, carry):
        slot = step % 2
        chains = []
        for h, cols in enumerate(heads):
            for _, ops, rows, s_ref, acc, reverse in directions:
                chains.append((h, cols, ops, rows, s_ref, acc, reverse,
                               chunk_rows((n - 1 - step) if reverse else step)))
        firsts = []
        for h, cols, ops, rows, s_ref, acc, reverse, r in chains:
            st = s_ref[h]
            firsts.append((st, _gla_scores(ops[slot, 0, :, cols], ops[slot, 1, :, cols], st,
                                           rows[slot, 0, 0:1, cols])))
        for (h, cols, ops, rows, s_ref, acc, reverse, r), (st, both) in zip(chains, firsts):
            o, st = _gla_finish(both, ops[slot, 2, :, cols], i_ref[r, cols], st, rows[slot, 1, 0:1, cols],
                                reverse)
            acc[r, cols] = o
            s_ref[h] = st
        prepare(jnp.minimum(step + 1, n - 1))
        return carry

    lax.fori_loop(0, n, scan, 0)
    for h, cols in enumerate(heads):
        o = of_ref[:, cols] + ob_ref[:, cols]
        y = o * lax.rsqrt(jnp.mean(o * o, axis=-1, keepdims=True) + EPS) * gn_ref[...]
        o_ref[:, cols] = (y * og_ref[:, cols].astype(F32)).astype(o_ref.dtype)


def _gla_call(q, i, ff, fb, og, ci, cff, cfb, gnorm):
    bsz, t, d = q.shape
    width = GLA_HEADS_PER_STEP * HEAD_DIM
    tc = ci.shape[1]
    lat = pl.BlockSpec((None, t, width), lambda b, h: (b, 0, h))
    ctx = pl.BlockSpec((None, tc, width), lambda b, h: (b, 0, h))
    out_acc = pltpu.VMEM((t, width), F32)
    state = pltpu.VMEM((GLA_HEADS_PER_STEP, HEAD_DIM, HEAD_DIM), F32)
    operands = pltpu.VMEM((2, 3, GLA_CHUNK, width), BF16)
    decay_rows = pltpu.VMEM((2, 2, 8, width), F32)
    return pl.pallas_call(
        _gla_kernel,
        grid=(bsz, d // width),
        in_specs=[lat, lat, lat, lat, lat, ctx, ctx, ctx,
                  pl.BlockSpec((1, HEAD_DIM), lambda b, h: (0, 0))],
        out_specs=lat,
        out_shape=jax.ShapeDtypeStruct((bsz, t, d), BF16),
        scratch_shapes=[out_acc, out_acc, state, state, operands, operands, decay_rows, decay_rows],
        compiler_params=_cparams(("arbitrary", "arbitrary")),
        name="gla",
    )(q, i, ff, fb, og, ci, cff, cfb, gnorm)


def _mix_kernel(x_ref, u_ref, v_ref, ga_ref, gb_ref, ya_ref, ws_ref, bs_ref, wa_ref, wb_ref, wo_ref,
                g1_ref, n2_ref, sh2_ref, sc2_ref, wr_ref, br_ref,
                x1_ref, h2_ref, e_ref, w_ref, rank_ref, cnt_ref, yb_ref, carry_ref):
    tm, d = x_ref.shape
    first = (pl.program_id(0) == 0) & (pl.program_id(1) == 0)

    @pl.when(first)
    def _():
        carry_ref[...] = jnp.zeros_like(carry_ref)

    v = v_ref[...].astype(F32)
    mu = jnp.mean(v, axis=-1, keepdims=True)
    vc = v - mu
    var = jnp.mean(vc * vc, axis=-1, keepdims=True)
    vn = (vc * lax.rsqrt(var + EPS)).astype(BF16)
    for ck in range(tm // MLP_CHUNK):
        rows = slice(ck * MLP_CHUNK, (ck + 1) * MLP_CHUNK)
        for g in range(d // HEAD_DIM):
            cols = slice(g * HEAD_DIM, (g + 1) * HEAD_DIM)
            s = jnp.dot(ws_ref[g], vn[rows, cols], preferred_element_type=F32) + bs_ref[g]
            yb_ref[rows, cols] = (u_ref[rows, cols].astype(F32) * s).astype(BF16)

    pa = jnp.dot(ya_ref[...], wa_ref[...], preferred_element_type=F32)
    pb = jnp.dot(yb_ref[...], wb_ref[...], preferred_element_type=F32)
    merged = ga_ref[...].astype(F32) * pa + gb_ref[...].astype(F32) * pb
    mixed = jnp.dot(merged.astype(BF16), wo_ref[...], preferred_element_type=F32)
    x1 = x_ref[...] + g1_ref[...] * mixed
    x1_ref[...] = x1

    y = x1 * lax.rsqrt(jnp.mean(x1 * x1, axis=-1, keepdims=True) + EPS) * n2_ref[...]
    h2 = y * (1.0 + sc2_ref[...]) + sh2_ref[...]
    h2_ref[...] = h2

    logits = _dot_nt(wr_ref[...], h2.astype(BF16)) + br_ref[...]
    n_rows = logits.shape[0]
    eidx = lax.broadcasted_iota(jnp.int32, (n_rows, tm), 0)
    work = logits
    vals, idxs = [], []
    for _ in range(TOP_K):
        m = jnp.max(work, axis=0, keepdims=True)
        idx = jnp.min(jnp.where(work == m, eidx, n_rows), axis=0, keepdims=True)
        vals.append(m)
        idxs.append(idx)
        work = jnp.where(eidx == idx, -jnp.inf, work)
    exps = [jnp.exp(m - vals[0]) for m in vals]
    denom = exps[0] + exps[1] + exps[2] + exps[3]

    sel = jnp.zeros((n_rows, tm), F32)
    for idx in idxs:
        sel = sel + jnp.where(eidx == idx, 1.0, 0.0)
    tj = lax.broadcasted_iota(jnp.int32, (tm, tm), 0)
    tt = lax.broadcasted_iota(jnp.int32, (tm, tm), 1)
    tri = jnp.where(tj < tt, 1.0, 0.0).astype(BF16)
    before = carry_ref[...] + jnp.dot(sel.astype(BF16), tri, preferred_element_type=F32)

    krow = lax.broadcasted_iota(jnp.int32, e_ref.shape, 0)
    e_out = jnp.zeros(e_ref.shape, jnp.int32)
    w_out = jnp.zeros(e_ref.shape, F32)
    r_out = jnp.zeros(e_ref.shape, F32)
    for k in range(TOP_K):
        rk = jnp.sum(jnp.where(eidx == idxs[k], before, 0.0), axis=0, keepdims=True)
        e_out = jnp.where(krow == k, idxs[k], e_out)
        w_out = jnp.where(krow == k, exps[k] / denom, w_out)
        r_out = jnp.where(krow == k, rk, r_out)
    e_ref[...] = e_out
    w_ref[...] = w_out
    rank_ref[...] = r_out.astype(jnp.int32)
    carry_ref[...] = carry_ref[...] + jnp.sum(sel, axis=1, keepdims=True)
    cnt_ref[...] = carry_ref[...]


def _mix_call(x, u, v, ga, gb, ya, ws_bf, bs_full, wa_bf, wb_bf, wo_bf, g1, n2, sh2, sc2, wr_t, br_t, *, tm):
    bsz, t, d = x.shape
    n = bsz * t
    nt = t // tm
    n_rows = wr_t.shape[0]
    row = pl.BlockSpec((None, tm, d), lambda b, i: (b, i, 0))
    per_batch = pl.BlockSpec((None, 1, d), lambda b, i: (b, 0, 0))
    per_token = pl.BlockSpec((ROW_GROUP, tm), lambda b, i: (0, b * nt + i))

    return pl.pallas_call(
        _mix_kernel,
        grid=(bsz, nt),
        in_specs=[row, row, row, row, row, row,
                  _resident(ws_bf.shape), _resident(bs_full.shape),
                  _resident((d, d)), _resident((d, d)), _resident((d, d)),
                  per_batch, _resident((1, d)), per_batch, per_batch,
                  _resident(wr_t.shape), _resident(br_t.shape)],
        out_specs=[row, pl.BlockSpec((tm, d), lambda b, i: (b * nt + i, 0)),
                   per_token, per_token, per_token,
                   pl.BlockSpec((n_rows, tm), lambda b, i: (0, 0))],
        out_shape=[jax.ShapeDtypeStruct((bsz, t, d), F32),
                   jax.ShapeDtypeStruct((n, d), F32),
                   jax.ShapeDtypeStruct((ROW_GROUP, n), jnp.int32),
                   jax.ShapeDtypeStruct((ROW_GROUP, n), F32),
                   jax.ShapeDtypeStruct((ROW_GROUP, n), jnp.int32),
                   jax.ShapeDtypeStruct((n_rows, tm), F32)],
        scratch_shapes=[pltpu.VMEM((tm, d), BF16), pltpu.VMEM((n_rows, tm), F32)],
        compiler_params=_cparams(("arbitrary", "arbitrary")),
        name="mix",
    )(x, u, v, ga, gb, ya, ws_bf, bs_full, wa_bf, wb_bf, wo_bf, g1, n2, sh2, sc2, wr_t, br_t)


def _slot_kernel(starts_ref, e_ref, r_ref, o_ref):
    e = e_ref[...]
    start = jnp.zeros(e.shape, jnp.int32)
    for x in range(starts_ref.shape[0]):
        start = jnp.where(e == x, starts_ref[x], start)
    o_ref[...] = start + r_ref[...]


def _slot_call(pad_starts, e_t, r_t, *, tn):
    rows, n = e_t.shape
    blk = pl.BlockSpec((rows, tn), lambda i, starts: (0, i))
    return pl.pallas_call(
        _slot_kernel,
        grid_spec=pltpu.PrefetchScalarGridSpec(
            num_scalar_prefetch=1, grid=(n // tn,), in_specs=[blk, blk], out_specs=blk),
        out_shape=jax.ShapeDtypeStruct((rows, n), jnp.int32),
        compiler_params=_cparams(("arbitrary",)),
        name="slots",
    )(pad_starts, e_t, r_t)


def _padfill_kernel(ends_ref, padded_ref, xb_ref, zero_ref, sem):
    zero_ref[...] = jnp.zeros_like(zero_ref)
    n_exp = ends_ref.shape[0]

    def tail_copy(e):
        start = pl.multiple_of(ends_ref[e] - MOE_BLOCK, MOE_BLOCK)
        return pltpu.make_async_copy(zero_ref, xb_ref.at[pl.ds(start, MOE_BLOCK)], sem)

    for e in range(n_exp):
        @pl.when(padded_ref[e] > 0)
        def _():
            tail_copy(e).start()
    for e in range(n_exp):
        @pl.when(padded_ref[e] > 0)
        def _():
            tail_copy(e).wait()


def _padfill_call(pad_ends, padded, cap, d):
    return pl.pallas_call(
        _padfill_kernel,
        grid_spec=pltpu.PrefetchScalarGridSpec(
            num_scalar_prefetch=2, grid=(1,), in_specs=[],
            out_specs=pl.BlockSpec(memory_space=pl.ANY),
            scratch_shapes=[pltpu.VMEM((MOE_BLOCK, d), F32), pltpu.SemaphoreType.DMA]),
        out_shape=jax.ShapeDtypeStruct((cap, d), F32),
        compiler_params=_cparams(("arbitrary",)),
        name="padfill",
    )(pad_ends, padded)


def _row_copy(src_ref, src_row, dst_ref, dst_row, sem):
    return pltpu.make_async_copy(src_ref.at[pl.ds(src_row, 1)], dst_ref.at[pl.ds(dst_row, 1)], sem)


def _for_row_groups(n_rows, fn):
    def body(g, carry):
        base = pl.multiple_of(g * ROW_GROUP, ROW_GROUP)
        for rr in range(ROW_GROUP):
            fn(base + rr)
        return carry
    lax.fori_loop(0, n_rows // ROW_GROUP, body, 0)


DISPATCH_RING = 3


def _dispatch_kernel(dest_ref, h_ref, xin_ref, xb_ref, hbuf, load_sems, row_sems):
    del xin_ref
    tm = hbuf.shape[1]
    i = pl.program_id(0)
    last = pl.num_programs(0) - 1
    slot = i % DISPATCH_RING

    def load(tile, s):
        start = pl.multiple_of(tile * tm, tm)
        return pltpu.make_async_copy(h_ref.at[pl.ds(start, tm)], hbuf.at[s], load_sems.at[s])

    def drain(s):
        for _ in range(TOP_K):
            pltpu.make_async_copy(hbuf.at[s], xb_ref.at[pl.ds(0, tm)], row_sems.at[s]).wait()

    @pl.when(i == 0)
    def _():
        load(0, 0).start()

    load(i, slot).wait()

    @pl.when(i < last)
    def _():
        load(i + 1, (i + 1) % DISPATCH_RING).start()

    def issue(r):
        for k in range(TOP_K):
            _row_copy(hbuf.at[slot], r, xb_ref, dest_ref[0, 0, k * tm + r],
                      row_sems.at[slot]).start(priority=k % 2)

    _for_row_groups(tm, issue)

    @pl.when(i >= 1)
    def _():
        drain((i + DISPATCH_RING - 1) % DISPATCH_RING)

    @pl.when(i == last)
    def _():
        drain(slot)


def _dispatch_call(dest_tiles, h2, xb_init, *, tm):
    n, d = h2.shape
    return pl.pallas_call(
        _dispatch_kernel,
        grid=(n // tm,),
        in_specs=[pl.BlockSpec((1, 1, tm * TOP_K), lambda i: (i, 0, 0), memory_space=pltpu.SMEM),
                  pl.BlockSpec(memory_space=pl.ANY),
                  pl.BlockSpec(memory_space=pl.ANY)],
        out_specs=pl.BlockSpec(memory_space=pl.ANY),
        out_shape=jax.ShapeDtypeStruct(xb_init.shape, F32),
        scratch_shapes=[pltpu.VMEM((DISPATCH_RING, tm, d), F32),
                        pltpu.SemaphoreType.DMA((DISPATCH_RING,)), pltpu.SemaphoreType.DMA((DISPATCH_RING,))],
        input_output_aliases={2: 0},
        compiler_params=_cparams(("arbitrary",)),
        name="dispatch",
    )(dest_tiles, h2, xb_init)


def _expert_kernel(be_ref, nu_ref, x_ref, w1_ref, b1_ref, w2_ref, b2_ref, y_ref, w1b_ref, w2b_ref):
    i = pl.program_id(0)
    f = w2_ref.shape[1]
    active = i < nu_ref[0]

    @pl.when(active & ((i == 0) | (be_ref[i] != be_ref[jnp.maximum(i - 1, 0)])))
    def _():
        w1b_ref[...] = w1_ref[0].astype(BF16)
        w2b_ref[...] = w2_ref[0].astype(BF16)

    @pl.when(active)
    def _():
        z = jnp.dot(x_ref[...].astype(BF16), w1b_ref[...], preferred_element_type=F32) + b1_ref[0]
        gate = jnp.minimum(z[:, :f], SWIGLU_LIMIT)
        lin = jnp.clip(z[:, f:], -SWIGLU_LIMIT, SWIGLU_LIMIT)
        act = gate * jax.nn.sigmoid(SWIGLU_ALPHA * gate) * (lin + 1.0)
        y_ref[...] = jnp.dot(act.astype(BF16), w2b_ref[...], preferred_element_type=F32) + b2_ref[0]


def _expert_call(block_e, n_used, xb, w1, b1, w2, b2):
    cap, d = xb.shape
    n_blocks = cap // MOE_BLOCK
    f2 = w1.shape[2]
    f = w2.shape[1]

    def blk(i, be, nu):
        return jnp.minimum(i, nu[0] - 1)

    grid_spec = pltpu.PrefetchScalarGridSpec(
        num_scalar_prefetch=2,
        grid=(n_blocks,),
        in_specs=[pl.BlockSpec((MOE_BLOCK, d), lambda i, be, nu: (blk(i, be, nu), 0)),
                  pl.BlockSpec((1, d, f2), lambda i, be, nu: (be[blk(i, be, nu)], 0, 0)),
                  pl.BlockSpec((1, 1, f2), lambda i, be, nu: (be[blk(i, be, nu)], 0, 0)),
                  pl.BlockSpec((1, f, d), lambda i, be, nu: (be[blk(i, be, nu)], 0, 0)),
                  pl.BlockSpec((1, 1, d), lambda i, be, nu: (be[blk(i, be, nu)], 0, 0))],
        out_specs=pl.BlockSpec((MOE_BLOCK, d), lambda i, be, nu: (blk(i, be, nu), 0)),
        scratch_shapes=[pltpu.VMEM((d, f2), BF16), pltpu.VMEM((f, d), BF16)],
    )
    return pl.pallas_call(
        _expert_kernel,
        grid_spec=grid_spec,
        out_shape=jax.ShapeDtypeStruct((cap, d), F32),
        compiler_params=_cparams(("arbitrary",)),
        name="experts",
    )(block_e, n_used, xb, w1, b1, w2, b2)


def _combine_kernel(dest_ref, nxt_ref, x1_ref, w_ref, g2_ref, fg_ref, y_ref, o_ref, buf_ref, sems):
    tm = x1_ref.shape[0]
    nt = pl.num_programs(1)
    i = pl.program_id(0) * nt + pl.program_id(1)
    last = pl.num_programs(0) * nt - 1
    cur = i % 2

    def gather(tab_ref, slot):
        def issue(r):
            for k in range(TOP_K):
                _row_copy(y_ref, tab_ref[0, 0, k * tm + r], buf_ref.at[slot, k], r,
                          sems.at[slot]).start(priority=k % 2)
        _for_row_groups(tm, issue)

    @pl.when(i == 0)
    def _():
        gather(dest_ref, 0)

    @pl.when(i < last)
    def _():
        gather(nxt_ref, 1 - cur)

    for k in range(TOP_K):
        pltpu.make_async_copy(y_ref.at[pl.ds(0, tm)], buf_ref.at[cur, k], sems.at[cur]).wait()

    w = w_ref[...]
    moe = w[:, 0:1] * buf_ref[cur, 0]
    for k in range(1, TOP_K):
        moe = moe + w[:, k:k + 1] * buf_ref[cur, k]
    x2 = x1_ref[...] + g2_ref[...] * moe
    o_ref[...] = x2 * lax.rsqrt(jnp.mean(x2 * x2, axis=-1, keepdims=True) + EPS) * fg_ref[...]


def _combine_call(dest_tiles, x1, w_lanes, g2, final_g, y, *, tm):
    bsz, t, d = x1.shape
    nt = t // tm
    tiles = bsz * nt

    def table(shift):
        return pl.BlockSpec((1, 1, tm * TOP_K), lambda b, i: (jnp.minimum(b * nt + i + shift, tiles - 1), 0, 0),
                            memory_space=pltpu.SMEM)

    return pl.pallas_call(
        _combine_kernel,
        grid=(bsz, nt),
        in_specs=[table(0), table(1),
                  pl.BlockSpec((None, tm, d), lambda b, i: (b, i, 0)),
                  pl.BlockSpec((tm, LANES), lambda b, i: (b * nt + i, 0)),
                  pl.BlockSpec((None, 1, d), lambda b, i: (b, 0, 0)),
                  pl.BlockSpec((1, d), lambda b, i: (0, 0)),
                  pl.BlockSpec(memory_space=pl.ANY)],
        out_specs=pl.BlockSpec((None, tm, d), lambda b, i: (b, i, 0)),
        out_shape=jax.ShapeDtypeStruct((bsz, t, d), F32),
        scratch_shapes=[pltpu.VMEM((2, TOP_K, tm, d), F32), pltpu.SemaphoreType.DMA((2,))],
        compiler_params=_cparams(("arbitrary", "arbitrary")),
        name="combine",
    )(dest_tiles, dest_tiles, x1, w_lanes, g2, final_g, y)


def kernel(x, c, ctx, c_ctx, norm1_g, norm2_g, w_mod, b_mod, w_in, lb_fwd, lb_bwd, gnorm_g, w_s, b_s,
           w_branch_a, w_branch_b, w_out, w_router, b_router, w1, b1, w2, b2, final_g):
    bsz, t, d = x.shape
    assert w_in.shape[0] == 1 and lb_fwd.shape[0] == 2, "single-layer block"
    assert w_in.shape[2] == N_SPLITS * d and d % HEAD_DIM == 0
    assert t % MLP_CHUNK == 0 and ctx.shape[1] % GLA_CHUNK == 0
    n = bsz * t

    pad_rows = (-(bsz + 1)) % 8
    cc = jnp.concatenate([c, c_ctx[None, :], jnp.zeros((pad_rows, d), F32)], axis=0)
    mod = _mod_call(cc, w_mod[0], b_mod[0][None, :])
    lat = mod[:bsz].reshape(bsz, N_MOD, 1, d)
    sh1, sc1, g1, sh2, sc2, g2 = (lat[:, m] for m in range(N_MOD))
    cmod = mod[bsz].reshape(N_MOD, 1, 1, d)
    csh1, csc1 = cmod[0], cmod[1]

    w_in_bf = w_in[0].astype(BF16)
    n1 = norm1_g[0][None, :]
    q, i_, ff, fb, og, u, v, ga, gb = _proj_call(
        x, n1, sh1, sc1, w_in_bf, lb_fwd, lb_bwd,
        splits=tuple(range(N_SPLITS)), tm=min(256, t), per_batch_mod=True)
    ci, cff, cfb = _proj_call(
        ctx, n1, csh1, csc1, w_in_bf, lb_fwd, lb_bwd,
        splits=(S_I, S_FF, S_FB), tm=ctx.shape[1], per_batch_mod=False)

    ya = _gla_call(q, i_, ff, fb, og, ci, cff, cfb, gnorm_g[0][None, :])

    n_exp = w_router.shape[2]
    tm_io = min(256, t)
    exp_rows = -(-n_exp // ROW_GROUP) * ROW_GROUP
    wr_t = jnp.zeros((exp_rows, d), BF16).at[:n_exp].set(w_router[0].T.astype(BF16))
    br_t = jnp.broadcast_to(jnp.full((exp_rows,), NEG_BIG, F32).at[:n_exp].set(b_router[0])[:, None],
                            (exp_rows, tm_io))
    bs_full = jnp.broadcast_to(b_s[0][:, :, None], b_s.shape[1:] + (HEAD_DIM,))
    x1, h2, e_t, w_t, r_t, cnt = _mix_call(
        x, u, v, ga, gb, ya, w_s[0].astype(BF16), bs_full,
        w_branch_a[0].astype(BF16), w_branch_b[0].astype(BF16), w_out[0].astype(BF16),
        g1, norm2_g[0][None, :], sh2, sc2, wr_t, br_t, tm=tm_io)

    counts = cnt[:n_exp, 0].astype(jnp.int32)
    padded = (counts + MOE_BLOCK - 1) // MOE_BLOCK * MOE_BLOCK
    pad_ends = jnp.cumsum(padded)
    pad_starts = pad_ends - padded
    n_blocks = -(-(n * TOP_K) // MOE_BLOCK) + n_exp
    cap = n_blocks * MOE_BLOCK
    dest = _slot_call(pad_starts, e_t, r_t, tn=min(8192, n))[:TOP_K]
    blk_start = jnp.arange(n_blocks, dtype=jnp.int32) * MOE_BLOCK
    block_e = jnp.minimum(jnp.sum((pad_ends[None, :] <= blk_start[:, None]).astype(jnp.int32), axis=1),
                          n_exp - 1)
    n_used = (pad_ends[-1:] // MOE_BLOCK).astype(jnp.int32)

    dest_tiles = dest.reshape(TOP_K, n // tm_io, tm_io).transpose(1, 0, 2).reshape(n // tm_io, 1, TOP_K * tm_io)
    w_l = jnp.zeros((n, LANES), F32).at[:, :TOP_K].set(w_t[:TOP_K].T)
    xb = _dispatch_call(dest_tiles, h2, _padfill_call(pad_ends, padded, cap, d), tm=tm_io)
    y = _expert_call(block_e, n_used, xb, w1[0], b1[0][:, None, :], w2[0], b2[0][:, None, :])
    return _combine_call(dest_tiles, x1, w_l, g2, final_g[None, :], y, tm=tm_io)
```

```python
import functools

import jax
import jax.numpy as jnp
from jax import lax
from jax.experimental import pallas as pl
from jax.experimental.pallas import tpu as pltpu

F32 = jnp.float32
BF16 = jnp.bfloat16

EPS = 1e-6
N_MOD = 6
S_Q, S_I, S_FF, S_FB, S_OG, S_U, S_V, S_GA, S_GB = range(9)
N_SPLITS = 9
SPLIT_DTYPE = {S_FF: F32, S_FB: F32}
HEAD_DIM = 128
GLA_CHUNK = 64
GLA_HEADS_PER_STEP = 4
MLP_CHUNK = 128
TOP_K = 4
MOE_BLOCK = 512
ROW_GROUP = 8
SWIGLU_LIMIT = 7.0
SWIGLU_ALPHA = 1.702
LANES = 128
NEG_BIG = -1e30
VMEM_LIMIT = 56 * 1024 * 1024


def _cparams(sem):
    return pltpu.CompilerParams(dimension_semantics=sem, vmem_limit_bytes=VMEM_LIMIT)


def _resident(shape):
    zeros = (0,) * len(shape)
    return pl.BlockSpec(shape, lambda *_: zeros, pipeline_mode=pl.Buffered(1))


def _mod_kernel(c_ref, w_ref, b_ref, o_ref):
    c = c_ref[...]
    h = (c * jax.nn.sigmoid(c)).astype(BF16)
    o_ref[...] = jnp.dot(h, w_ref[...].astype(BF16), preferred_element_type=F32) + b_ref[...]


def _mod_call(cc, w_mod, b_mod):
    rows, d = cc.shape
    width = w_mod.shape[1]
    return pl.pallas_call(
        _mod_kernel,
        grid=(width // d,),
        in_specs=[pl.BlockSpec((rows, d), lambda j: (0, 0)),
                  pl.BlockSpec((d, d), lambda j: (0, j)),
                  pl.BlockSpec((1, d), lambda j: (0, j))],
        out_specs=pl.BlockSpec((rows, d), lambda j: (0, j)),
        out_shape=jax.ShapeDtypeStruct((rows, width), F32),
        compiler_params=_cparams(("arbitrary",)),
        name="mod",
    )(cc, w_mod, b_mod)


def _lower_bound(lb_ref):
    a = lb_ref[0:1, :]
    b = lb_ref[1:2, :]
    m = jnp.maximum(a, b)
    ea = jnp.exp(a - m)
    eb = jnp.exp(b - m)
    return ea / (ea + eb)


def _gelu(z):
    return 0.5 * z * (1.0 + lax.erf(z * (2.0 ** -0.5)))


def _split_activation(s, z, lbf_ref, lbb_ref):
    if s in (S_Q, S_OG):
        return z * jax.nn.sigmoid(z)
    if s == S_I:
        return z
    if s in (S_FF, S_FB):
        lb = _lower_bound(lbf_ref if s == S_FF else lbb_ref)
        return lb + (1.0 - lb) * jax.nn.sigmoid(z)
    if s in (S_U, S_V):
        return _gelu(z)
    return jax.nn.sigmoid(z)


def _proj_kernel(x_ref, g_ref, sh_ref, sc_ref, w_ref, lbf_ref, lbb_ref, *o_refs, splits):
    d = x_ref.shape[1]
    x = x_ref[...]
    y = x * lax.rsqrt(jnp.mean(x * x, axis=-1, keepdims=True) + EPS) * g_ref[...]
    h = (y * (1.0 + sc_ref[...]) + sh_ref[...]).astype(BF16)
    for s, o_ref in zip(splits, o_refs):
        z = jnp.dot(h, w_ref[:, s * d:(s + 1) * d], preferred_element_type=F32)
        o_ref[...] = _split_activation(s, z, lbf_ref, lbb_ref).astype(o_ref.dtype)


def _proj_call(x, g, sh, sc, w_bf, lbf, lbb, *, splits, tm, per_batch_mod):
    bsz, t, d = x.shape
    mod_map = (lambda b, i: (b, 0, 0)) if per_batch_mod else (lambda b, i: (0, 0, 0))
    row = pl.BlockSpec((None, tm, d), lambda b, i: (b, i, 0))
    return pl.pallas_call(
        functools.partial(_proj_kernel, splits=splits),
        grid=(bsz, t // tm),
        in_specs=[row,
                  pl.BlockSpec((1, d), lambda b, i: (0, 0)),
                  pl.BlockSpec((None, 1, d), mod_map),
                  pl.BlockSpec((None, 1, d), mod_map),
                  _resident(w_bf.shape),
                  pl.BlockSpec((2, d), lambda b, i: (0, 0)),
                  pl.BlockSpec((2, d), lambda b, i: (0, 0))],
        out_specs=[row for _ in splits],
        out_shape=[jax.ShapeDtypeStruct((bsz, t, d), SPLIT_DTYPE.get(s, BF16)) for s in splits],
        compiler_params=_cparams(("arbitrary", "arbitrary")),
        name="proj",
    )(x, g, sh, sc, w_bf, lbf, lbb)


def _prefix_sum(x, reverse):
    n = x.shape[0]
    row = lax.broadcasted_iota(jnp.int32, x.shape, 0)
    s = 1
    while s < n:
        if reverse:
            x = x + jnp.where(row < n - s, pltpu.roll(x, n - s, 0), 0.0)
        else:
            x = x + jnp.where(row >= s, pltpu.roll(x, s, 0), 0.0)
        s *= 2
    return x


def _dot_nt(a, b):
    return lax.dot_general(a, b, (((1,), (1,)), ((), ())), preferred_element_type=F32)


def _dot_tn(a, b):
    return lax.dot_general(a, b, (((0,), (0,)), ((), ())), preferred_element_type=F32)


def _gla_prepare(q, f, reverse):
    c = q.shape[0]
    b = _prefix_sum(jnp.log(f), reverse)
    if reverse:
        b_mid = b[c // 2:c // 2 + 1, :]
        b_end = b[0:1, :]
    else:
        b_mid = b[c // 2 - 1:c // 2, :]
        b_end = b[c - 1:c, :]
    qe = q * jnp.exp(b - b_mid)
    ke = (1.0 - f) * jnp.exp(b_mid - b)
    kd = ke * jnp.exp(b_end - b_mid)
    return qe.astype(BF16), ke.astype(BF16), kd.astype(BF16), jnp.exp(b_mid), jnp.exp(b_end)


def _gla_scores(qe, ke, st, e_mid):
    rhs = jnp.concatenate([(st * e_mid).astype(BF16), ke], axis=0)
    return _dot_nt(qe, rhs)


def _gla_finish(both, kd, vb, st, e_end, reverse):
    c = both.shape[0]
    dv = st.shape[0]
    ti = lax.broadcasted_iota(jnp.int32, (c, c), 0)
    si = lax.broadcasted_iota(jnp.int32, (c, c), 1)
    keep = (si >= ti) if reverse else (si <= ti)
    scores = jnp.where(keep, both[:, dv:], 0.0)
    o = both[:, :dv] + jnp.dot(scores.astype(BF16), vb, preferred_element_type=F32)
    return o, st * e_end + _dot_tn(vb, kd)


def _context_state(f, vb, reverse_decay):
    g = jnp.log(f)
    tail = _prefix_sum(g, reverse_decay) - g
    kt = ((1.0 - f) * jnp.exp(tail)).astype(BF16)
    return _dot_tn(vb, kt)


def _gla_kernel(q_ref, i_ref, ff_ref, fb_ref, og_ref, ci_ref, cff_ref, cfb_ref, gn_ref,
                o_ref, of_ref, ob_ref, sf_ref, sb_ref, ops_f, ops_b, rows_f, rows_b):
    t = q_ref.shape[0]
    c = GLA_CHUNK
    n = t // c
    heads = [slice(h * HEAD_DIM, (h + 1) * HEAD_DIM) for h in range(q_ref.shape[1] // HEAD_DIM)]
    for h, cols in enumerate(heads):
        ci = ci_ref[:, cols]
        sf_ref[h] = _context_state(cff_ref[:, cols], ci, True)
        sb_ref[h] = _context_state(cfb_ref[:, cols], ci, False)

    directions = ((ff_ref, ops_f, rows_f, sf_ref, of_ref, False), (fb_ref, ops_b, rows_b, sb_ref, ob_ref, True))

    def chunk_rows(ck):
        return pl.ds(pl.multiple_of(ck * c, c), c)

    def prepare(step):
        slot = step % 2
        for cols in heads:
            for f_ref, ops, rows, _, _, reverse in directions:
                r = chunk_rows((n - 1 - step) if reverse else step)
                qe, ke, kd, e_mid, e_end = _gla_prepare(q_ref[r, cols].astype(F32), f_ref[r, cols], reverse)
                ops[slot, 0, :, cols] = qe
                ops[slot, 1, :, cols] = ke
                ops[slot, 2, :, cols] = kd
                rows[slot, 0, :, cols] = jnp.broadcast_to(e_mid, (8, HEAD_DIM))
                rows[slot, 1, :, cols] = jnp.broadcast_to(e_end, (8, HEAD_DIM))

    prepare(0)

    def scan(step, carry):
        slot = step % 2
        chains = []
        for h, cols in enumerate(heads):
            for _, ops, rows, s_ref, acc, reverse in directions:
                chains.append((h, cols, ops, rows, s_ref, acc, reverse,
                               chunk_rows((n - 1 - step) if reverse else step)))
        firsts = []
        for h, cols, ops, rows, s_ref, acc, reverse, r in chains:
            st = s_ref[h]
            firsts.append((st, _gla_scores(ops[slot, 0, :, cols], ops[slot, 1, :, cols], st,
                                           rows[slot, 0, 0:1, cols])))
        for (h, cols, ops, rows, s_ref, acc, reverse, r), (st, both) in zip(chains, firsts):
            o, st = _gla_finish(both, ops[slot, 2, :, cols], i_ref[r, cols], st, rows[slot, 1, 0:1, cols],
                                reverse)
            acc[r, cols] = o
            s_ref[h] = st
        prepare(jnp.minimum(step + 1, n - 1))
        return carry

    lax.fori_loop(0, n, scan, 0)
    for h, cols in enumerate(heads):
        o = of_ref[:, cols] + ob_ref[:, cols]
        y = o * lax.rsqrt(jnp.mean(o * o, axis=-1, keepdims=True) + EPS) * gn_ref[...]
        o_ref[:, cols] = (y * og_ref[:, cols].astype(F32)).astype(o_ref.dtype)


def _gla_call(q, i, ff, fb, og, ci, cff, cfb, gnorm):
    bsz, t, d = q.shape
    width = GLA_HEADS_PER_STEP * HEAD_DIM
    tc = ci.shape[1]
    lat = pl.BlockSpec((None, t, width), lambda b, h: (b, 0, h))
    ctx = pl.BlockSpec((None, tc, width), lambda b, h: (b, 0, h))
    out_acc = pltpu.VMEM((t, width), F32)
    state = pltpu.VMEM((GLA_HEADS_PER_STEP, HEAD_DIM, HEAD_DIM), F32)
    operands = pltpu.VMEM((2, 3, GLA_CHUNK, width), BF16)
    decay_rows = pltpu.VMEM((2, 2, 8, width), F32)
    return pl.pallas_call(
        _gla_kernel,
        grid=(bsz, d // width),
        in_specs=[lat, lat, lat, lat, lat, ctx, ctx, ctx,
                  pl.BlockSpec((1, HEAD_DIM), lambda b, h: (0, 0))],
        out_specs=lat,
        out_shape=jax.ShapeDtypeStruct((bsz, t, d), BF16),
        scratch_shapes=[out_acc, out_acc, state, state, operands, operands, decay_rows, decay_rows],
        compiler_params=_cparams(("arbitrary", "arbitrary")),
        name="gla",
    )(q, i, ff, fb, og, ci, cff, cfb, gnorm)


def _store_row_tiles(ref, value):
    rows, d = value.shape
    per_row = d // LANES
    for j in range(per_row):
        ref[pl.ds(j, rows, stride=per_row), :] = value[:, j * LANES:(j + 1) * LANES]


def _load_row_tiles(ref, d):
    per_row = d // LANES
    rows = ref.shape[0] // per_row
    return jnp.concatenate([ref[pl.ds(j, rows, stride=per_row), :] for j in range(per_row)], axis=1)


def _mix_kernel(x_ref, u_ref, v_ref, ga_ref, gb_ref, ya_ref, ws_ref, bs_ref, wa_ref, wb_ref, wo_ref,
                g1_ref, n2_ref, sh2_ref, sc2_ref, wr_ref, br_ref,
                x1_ref, h2_ref, e_ref, w_ref, rank_ref, cnt_ref, yb_ref, carry_ref):
    tm, d = x_ref.shape
    first = (pl.program_id(0) == 0) & (pl.program_id(1) == 0)

    @pl.when(first)
    def _():
        carry_ref[...] = jnp.zeros_like(carry_ref)

    v = v_ref[...].astype(F32)
    mu = jnp.mean(v, axis=-1, keepdims=True)
    vc = v - mu
    var = jnp.mean(vc * vc, axis=-1, keepdims=True)
    vn = (vc * lax.rsqrt(var + EPS)).astype(BF16)
    for ck in range(tm // MLP_CHUNK):
        rows = slice(ck * MLP_CHUNK, (ck + 1) * MLP_CHUNK)
        for g in range(d // HEAD_DIM):
            cols = slice(g * HEAD_DIM, (g + 1) * HEAD_DIM)
            s = jnp.dot(ws_ref[g], vn[rows, cols], preferred_element_type=F32) + bs_ref[g]
            yb_ref[rows, cols] = (u_ref[rows, cols].astype(F32) * s).astype(BF16)

    pa = jnp.dot(ya_ref[...], wa_ref[...], preferred_element_type=F32)
    pb = jnp.dot(yb_ref[...], wb_ref[...], preferred_element_type=F32)
    merged = ga_ref[...].astype(F32) * pa + gb_ref[...].astype(F32) * pb
    mixed = jnp.dot(merged.astype(BF16), wo_ref[...], preferred_element_type=F32)
    x1 = x_ref[...] + g1_ref[...] * mixed
    x1_ref[...] = x1

    y = x1 * lax.rsqrt(jnp.mean(x1 * x1, axis=-1, keepdims=True) + EPS) * n2_ref[...]
    h2 = y * (1.0 + sc2_ref[...]) + sh2_ref[...]
    _store_row_tiles(h2_ref, h2)

    logits = _dot_nt(wr_ref[...], h2.astype(BF16)) + br_ref[...]
    n_rows = logits.shape[0]
    eidx = lax.broadcasted_iota(jnp.int32, (n_rows, tm), 0)
    work = logits
    vals, idxs = [], []
    for _ in range(TOP_K):
        m = jnp.max(work, axis=0, keepdims=True)
        idx = jnp.min(jnp.where(work == m, eidx, n_rows), axis=0, keepdims=True)
        vals.append(m)
        idxs.append(idx)
        work = jnp.where(eidx == idx, -jnp.inf, work)
    exps = [jnp.exp(m - vals[0]) for m in vals]
    denom = exps[0] + exps[1] + exps[2] + exps[3]

    sel = jnp.zeros((n_rows, tm), F32)
    for idx in idxs:
        sel = sel + jnp.where(eidx == idx, 1.0, 0.0)
    tj = lax.broadcasted_iota(jnp.int32, (tm, tm), 0)
    tt = lax.broadcasted_iota(jnp.int32, (tm, tm), 1)
    tri = jnp.where(tj < tt, 1.0, 0.0).astype(BF16)
    before = carry_ref[...] + jnp.dot(sel.astype(BF16), tri, preferred_element_type=F32)

    krow = lax.broadcasted_iota(jnp.int32, e_ref.shape, 0)
    e_out = jnp.zeros(e_ref.shape, jnp.int32)
    w_out = jnp.zeros(e_ref.shape, F32)
    r_out = jnp.zeros(e_ref.shape, F32)
    for k in range(TOP_K):
        rk = jnp.sum(jnp.where(eidx == idxs[k], before, 0.0), axis=0, keepdims=True)
        e_out = jnp.where(krow == k, idxs[k], e_out)
        w_out = jnp.where(krow == k, exps[k] / denom, w_out)
        r_out = jnp.where(krow == k, rk, r_out)
    e_ref[...] = e_out
    w_ref[...] = w_out
    rank_ref[...] = r_out.astype(jnp.int32)
    carry_ref[...] = carry_ref[...] + jnp.sum(sel, axis=1, keepdims=True)
    cnt_ref[...] = carry_ref[...]


def _mix_call(x, u, v, ga, gb, ya, ws_bf, bs_full, wa_bf, wb_bf, wo_bf, g1, n2, sh2, sc2, wr_t, br_t, *, tm):
    bsz, t, d = x.shape
    n = bsz * t
    nt = t // tm
    n_rows = wr_t.shape[0]
    row = pl.BlockSpec((None, tm, d), lambda b, i: (b, i, 0))
    per_batch = pl.BlockSpec((None, 1, d), lambda b, i: (b, 0, 0))
    per_token = pl.BlockSpec((ROW_GROUP, tm), lambda b, i: (0, b * nt + i))

    return pl.pallas_call(
        _mix_kernel,
        grid=(bsz, nt),
        in_specs=[row, row, row, row, row, row,
                  _resident(ws_bf.shape), _resident(bs_full.shape),
                  _resident((d, d)), _resident((d, d)), _resident((d, d)),
                  per_batch, _resident((1, d)), per_batch, per_batch,
                  _resident(wr_t.shape), _resident(br_t.shape)],
        out_specs=[row, pl.BlockSpec((tm * d // LANES, LANES), lambda b, i: (b * nt + i, 0)),
                   per_token, per_token, per_token,
                   pl.BlockSpec((n_rows, tm), lambda b, i: (0, 0))],
        out_shape=[jax.ShapeDtypeStruct((bsz, t, d), F32),
                   jax.ShapeDtypeStruct((n * d // LANES, LANES), F32),
                   jax.ShapeDtypeStruct((ROW_GROUP, n), jnp.int32),
                   jax.ShapeDtypeStruct((ROW_GROUP, n), F32),
                   jax.ShapeDtypeStruct((ROW_GROUP, n), jnp.int32),
                   jax.ShapeDtypeStruct((n_rows, tm), F32)],
        scratch_shapes=[pltpu.VMEM((tm, d), BF16), pltpu.VMEM((n_rows, tm), F32)],
        compiler_params=_cparams(("arbitrary", "arbitrary")),
        name="mix",
    )(x, u, v, ga, gb, ya, ws_bf, bs_full, wa_bf, wb_bf, wo_bf, g1, n2, sh2, sc2, wr_t, br_t)


def _slot_kernel(starts_ref, e_ref, r_ref, o_ref):
    e = e_ref[...]
    start = jnp.zeros(e.shape, jnp.int32)
    for x in range(starts_ref.shape[0]):
        start = jnp.where(e == x, starts_ref[x], start)
    o_ref[...] = start + r_ref[...]


def _slot_call(pad_starts, e_t, r_t, *, tn):
    rows, n = e_t.shape
    blk = pl.BlockSpec((rows, tn), lambda i, starts: (0, i))
    return pl.pallas_call(
        _slot_kernel,
        grid_spec=pltpu.PrefetchScalarGridSpec(
            num_scalar_prefetch=1, grid=(n // tn,), in_specs=[blk, blk], out_specs=blk),
        out_shape=jax.ShapeDtypeStruct((rows, n), jnp.int32),
        compiler_params=_cparams(("arbitrary",)),
        name="slots",
    )(pad_starts, e_t, r_t)


def _padfill_kernel(ends_ref, padded_ref, xb_ref, zero_ref, sem):
    zero_ref[...] = jnp.zeros_like(zero_ref)
    n_exp = ends_ref.shape[0]
    per_row = zero_ref.shape[0] // MOE_BLOCK

    def tail_copy(e):
        start = pl.multiple_of((ends_ref[e] - MOE_BLOCK) * per_row, MOE_BLOCK)
        return pltpu.make_async_copy(zero_ref, xb_ref.at[pl.ds(start, MOE_BLOCK * per_row)], sem)

    for e in range(n_exp):
        @pl.when(padded_ref[e] > 0)
        def _():
            tail_copy(e).start()
    for e in range(n_exp):
        @pl.when(padded_ref[e] > 0)
        def _():
            tail_copy(e).wait()


def _padfill_call(pad_ends, padded, cap, d):
    return pl.pallas_call(
        _padfill_kernel,
        grid_spec=pltpu.PrefetchScalarGridSpec(
            num_scalar_prefetch=2, grid=(1,), in_specs=[],
            out_specs=pl.BlockSpec(memory_space=pl.ANY),
            scratch_shapes=[pltpu.VMEM((MOE_BLOCK * d // LANES, LANES), F32), pltpu.SemaphoreType.DMA]),
        out_shape=jax.ShapeDtypeStruct((cap * d // LANES, LANES), F32),
        compiler_params=_cparams(("arbitrary",)),
        name="padfill",
    )(pad_ends, padded)


def _row_copy(src_ref, src_row, dst_ref, dst_row, sem, per_row):
    src = pl.ds(pl.multiple_of(src_row * per_row, per_row), per_row)
    dst = pl.ds(pl.multiple_of(dst_row * per_row, per_row), per_row)
    return pltpu.make_async_copy(src_ref.at[src], dst_ref.at[dst], sem)


def _for_row_groups(n_rows, fn):
    def body(g, carry):
        base = pl.multiple_of(g * ROW_GROUP, ROW_GROUP)
        for rr in range(ROW_GROUP):
            fn(base + rr)
        return carry
    lax.fori_loop(0, n_rows // ROW_GROUP, body, 0)


DISPATCH_RING = 3


def _dispatch_kernel(dest_ref, h_ref, xin_ref, xb_ref, hbuf, load_sems, row_sems):
    del xin_ref
    tm = dest_ref.shape[2] // TOP_K
    per_row = hbuf.shape[1] // tm
    i = pl.program_id(0)
    last = pl.num_programs(0) - 1
    slot = i % DISPATCH_RING

    def load(tile, s):
        start = pl.multiple_of(tile * tm * per_row, tm * per_row)
        return pltpu.make_async_copy(h_ref.at[pl.ds(start, tm * per_row)], hbuf.at[s], load_sems.at[s])

    def drain(s):
        for _ in range(TOP_K):
            pltpu.make_async_copy(hbuf.at[s], xb_ref.at[pl.ds(0, tm * per_row)], row_sems.at[s]).wait()

    @pl.when(i == 0)
    def _():
        load(0, 0).start()

    load(i, slot).wait()

    @pl.when(i < last)
    def _():
        load(i + 1, (i + 1) % DISPATCH_RING).start()

    def issue(r):
        for k in range(TOP_K):
            _row_copy(hbuf.at[slot], r, xb_ref, dest_ref[0, 0, k * tm + r],
                      row_sems.at[slot], per_row).start(priority=k % 2)

    _for_row_groups(tm, issue)

    @pl.when(i >= 1)
    def _():
        drain((i + DISPATCH_RING - 1) % DISPATCH_RING)

    @pl.when(i == last)
    def _():
        drain(slot)


def _dispatch_call(dest_tiles, h2, xb_init, *, tm):
    n = dest_tiles.shape[0] * tm
    return pl.pallas_call(
        _dispatch_kernel,
        grid=(n // tm,),
        in_specs=[pl.BlockSpec((1, 1, tm * TOP_K), lambda i: (i, 0, 0), memory_space=pltpu.SMEM),
                  pl.BlockSpec(memory_space=pl.ANY),
                  pl.BlockSpec(memory_space=pl.ANY)],
        out_specs=pl.BlockSpec(memory_space=pl.ANY),
        out_shape=jax.ShapeDtypeStruct(xb_init.shape, F32),
        scratch_shapes=[pltpu.VMEM((DISPATCH_RING, h2.shape[0] // n * tm, LANES), F32),
                        pltpu.SemaphoreType.DMA((DISPATCH_RING,)), pltpu.SemaphoreType.DMA((DISPATCH_RING,))],
        input_output_aliases={2: 0},
        compiler_params=_cparams(("arbitrary",)),
        name="dispatch",
    )(dest_tiles, h2, xb_init)


def _expert_kernel(be_ref, nu_ref, x_ref, w1_ref, b1_ref, w2_ref, b2_ref, y_ref, w1b_ref, w2b_ref):
    i = pl.program_id(0)
    f = w2_ref.shape[1]
    active = i < nu_ref[0]

    @pl.when(active & ((i == 0) | (be_ref[i] != be_ref[jnp.maximum(i - 1, 0)])))
    def _():
        w1b_ref[...] = w1_ref[0].astype(BF16)
        w2b_ref[...] = w2_ref[0].astype(BF16)

    @pl.when(active)
    def _():
        x = _load_row_tiles(x_ref, w1b_ref.shape[0])
        z = jnp.dot(x.astype(BF16), w1b_ref[...], preferred_element_type=F32) + b1_ref[0]
        gate = jnp.minimum(z[:, :f], SWIGLU_LIMIT)
        lin = jnp.clip(z[:, f:], -SWIGLU_LIMIT, SWIGLU_LIMIT)
        act = gate * jax.nn.sigmoid(SWIGLU_ALPHA * gate) * (lin + 1.0)
        _store_row_tiles(y_ref, jnp.dot(act.astype(BF16), w2b_ref[...], preferred_element_type=F32) + b2_ref[0])


def _expert_call(block_e, n_used, xb, w1, b1, w2, b2):
    d = w1.shape[1]
    per_row = d // LANES
    n_blocks = xb.shape[0] // (MOE_BLOCK * per_row)
    f2 = w1.shape[2]
    f = w2.shape[1]

    def blk(i, be, nu):
        return jnp.minimum(i, nu[0] - 1)

    grid_spec = pltpu.PrefetchScalarGridSpec(
        num_scalar_prefetch=2,
        grid=(n_blocks,),
        in_specs=[pl.BlockSpec((MOE_BLOCK * per_row, LANES), lambda i, be, nu: (blk(i, be, nu), 0)),
                  pl.BlockSpec((1, d, f2), lambda i, be, nu: (be[blk(i, be, nu)], 0, 0)),
                  pl.BlockSpec((1, 1, f2), lambda i, be, nu: (be[blk(i, be, nu)], 0, 0)),
                  pl.BlockSpec((1, f, d), lambda i, be, nu: (be[blk(i, be, nu)], 0, 0)),
                  pl.BlockSpec((1, 1, d), lambda i, be, nu: (be[blk(i, be, nu)], 0, 0))],
        out_specs=pl.BlockSpec((MOE_BLOCK * per_row, LANES), lambda i, be, nu: (blk(i, be, nu), 0)),
        scratch_shapes=[pltpu.VMEM((d, f2), BF16), pltpu.VMEM((f, d), BF16)],
    )
    return pl.pallas_call(
        _expert_kernel,
        grid_spec=grid_spec,
        out_shape=jax.ShapeDtypeStruct(xb.shape, F32),
        compiler_params=_cparams(("arbitrary",)),
        name="experts",
    )(block_e, n_used, xb, w1, b1, w2, b2)


def _combine_kernel(dest_ref, nxt_ref, x1_ref, w_ref, g2_ref, fg_ref, y_ref, o_ref, buf_ref, sems):
    tm, d = x1_ref.shape
    per_row = d // LANES
    nt = pl.num_programs(1)
    i = pl.program_id(0) * nt + pl.program_id(1)
    last = pl.num_programs(0) * nt - 1
    cur = i % 2

    def gather(tab_ref, slot):
        def issue(r):
            for k in range(TOP_K):
                _row_copy(y_ref, tab_ref[0, 0, k * tm + r], buf_ref.at[slot, k], r,
                          sems.at[slot], per_row).start(priority=k % 2)
        _for_row_groups(tm, issue)

    @pl.when(i == 0)
    def _():
        gather(dest_ref, 0)

    @pl.when(i < last)
    def _():
        gather(nxt_ref, 1 - cur)

    for k in range(TOP_K):
        pltpu.make_async_copy(y_ref.at[pl.ds(0, tm * per_row)], buf_ref.at[cur, k], sems.at[cur]).wait()

    w = w_ref[...]
    moe = w[:, 0:1] * _load_row_tiles(buf_ref.at[cur, 0], d)
    for k in range(1, TOP_K):
        moe = moe + w[:, k:k + 1] * _load_row_tiles(buf_ref.at[cur, k], d)
    x2 = x1_ref[...] + g2_ref[...] * moe
    o_ref[...] = x2 * lax.rsqrt(jnp.mean(x2 * x2, axis=-1, keepdims=True) + EPS) * fg_ref[...]


def _combine_call(dest_tiles, x1, w_lanes, g2, final_g, y, *, tm):
    bsz, t, d = x1.shape
    nt = t // tm
    tiles = bsz * nt

    def table(shift):
        return pl.BlockSpec((1, 1, tm * TOP_K), lambda b, i: (jnp.minimum(b * nt + i + shift, tiles - 1), 0, 0),
                            memory_space=pltpu.SMEM)

    return pl.pallas_call(
        _combine_kernel,
        grid=(bsz, nt),
        in_specs=[table(0), table(1),
                  pl.BlockSpec((None, tm, d), lambda b, i: (b, i, 0)),
                  pl.BlockSpec((tm, LANES), lambda b, i: (b * nt + i, 0)),
                  pl.BlockSpec((None, 1, d), lambda b, i: (b, 0, 0)),
                  pl.BlockSpec((1, d), lambda b, i: (0, 0)),
                  pl.BlockSpec(memory_space=pl.ANY)],
        out_specs=pl.BlockSpec((None, tm, d), lambda b, i: (b, i, 0)),
        out_shape=jax.ShapeDtypeStruct((bsz, t, d), F32),
        scratch_shapes=[pltpu.VMEM((2, TOP_K, tm * d // LANES, LANES), F32), pltpu.SemaphoreType.DMA((2,))],
        compiler_params=_cparams(("arbitrary", "arbitrary")),
        name="combine",
    )(dest_tiles, dest_tiles, x1, w_lanes, g2, final_g, y)


def kernel(x, c, ctx, c_ctx, norm1_g, norm2_g, w_mod, b_mod, w_in, lb_fwd, lb_bwd, gnorm_g, w_s, b_s,
           w_branch_a, w_branch_b, w_out, w_router, b_router, w1, b1, w2, b2, final_g):
    bsz, t, d = x.shape
    assert w_in.shape[0] == 1 and lb_fwd.shape[0] == 2, "single-layer block"
    assert w_in.shape[2] == N_SPLITS * d and d % HEAD_DIM == 0
    assert t % MLP_CHUNK == 0 and ctx.shape[1] % GLA_CHUNK == 0
    n = bsz * t

    pad_rows = (-(bsz + 1)) % 8
    cc = jnp.concatenate([c, c_ctx[None, :], jnp.zeros((pad_rows, d), F32)], axis=0)
    mod = _mod_call(cc, w_mod[0], b_mod[0][None, :])
    lat = mod[:bsz].reshape(bsz, N_MOD, 1, d)
    sh1, sc1, g1, sh2, sc2, g2 = (lat[:, m] for m in range(N_MOD))
    cmod = mod[bsz].reshape(N_MOD, 1, 1, d)
    csh1, csc1 = cmod[0], cmod[1]

    w_in_bf = w_in[0].astype(BF16)
    n1 = norm1_g[0][None, :]
    q, i_, ff, fb, og, u, v, ga, gb = _proj_call(
        x, n1, sh1, sc1, w_in_bf, lb_fwd, lb_bwd,
        splits=tuple(range(N_SPLITS)), tm=min(256, t), per_batch_mod=True)
    ci, cff, cfb = _proj_call(
        ctx, n1, csh1, csc1, w_in_bf, lb_fwd, lb_bwd,
        splits=(S_I, S_FF, S_FB), tm=ctx.shape[1], per_batch_mod=False)

    ya = _gla_call(q, i_, ff, fb, og, ci, cff, cfb, gnorm_g[0][None, :])

    n_exp = w_router.shape[2]
    tm_io = min(256, t)
    exp_rows = -(-n_exp // ROW_GROUP) * ROW_GROUP
    wr_t = jnp.zeros((exp_rows, d), BF16).at[:n_exp].set(w_router[0].T.astype(BF16))
    br_t = jnp.broadcast_to(jnp.full((exp_rows,), NEG_BIG, F32).at[:n_exp].set(b_router[0])[:, None],
                            (exp_rows, tm_io))
    bs_full = jnp.broadcast_to(b_s[0][:, :, None], b_s.shape[1:] + (HEAD_DIM,))
    x1, h2, e_t, w_t, r_t, cnt = _mix_call(
        x, u, v, ga, gb, ya, w_s[0].astype(BF16), bs_full,
        w_branch_a[0].astype(BF16), w_branch_b[0].astype(BF16), w_out[0].astype(BF16),
        g1, norm2_g[0][None, :], sh2, sc2, wr_t, br_t, tm=tm_io)

    counts = cnt[:n_exp, 0].astype(jnp.int32)
    padded = (counts + MOE_BLOCK - 1) // MOE_BLOCK * MOE_BLOCK
    pad_ends = jnp.cumsum(padded)
    pad_starts = pad_ends - padded
    n_blocks = -(-(n * TOP_K) // MOE_BLOCK) + n_exp
    cap = n_blocks * MOE_BLOCK
    dest = _slot_call(pad_starts, e_t, r_t, tn=min(8192, n))[:TOP_K]
    blk_start = jnp.arange(n_blocks, dtype=jnp.int32) * MOE_BLOCK
    block_e = jnp.minimum(jnp.sum((pad_ends[None, :] <= blk_start[:, None]).astype(jnp.int32), axis=1),
                          n_exp - 1)
    n_used = (pad_ends[-1:] // MOE_BLOCK).astype(jnp.int32)

    dest_tiles = dest.reshape(TOP_K, n // tm_io, tm_io).transpose(1, 0, 2).reshape(n // tm_io, 1, TOP_K * tm_io)
    w_l = jnp.zeros((n, LANES), F32).at[:, :TOP_K].set(w_t[:TOP_K].T)
    xb = _dispatch_call(dest_tiles, h2, _padfill_call(pad_ends, padded, cap, d), tm=tm_io)
    y = _expert_call(block_e, n_used, xb, w1[0], b1[0][:, None, :], w2[0], b2[0][:, None, :])
    return _combine_call(dest_tiles, x1, w_l, g2, final_g[None, :], y, tm=tm_io)
```

```python
import functools

import jax
import jax.numpy as jnp
from jax import lax
from jax.experimental import pallas as pl
from jax.experimental.pallas import tpu as pltpu

F32 = jnp.float32
BF16 = jnp.bfloat16

EPS = 1e-6
N_MOD = 6
S_Q, S_I, S_FF, S_FB, S_OG, S_U, S_V, S_GA, S_GB = range(9)
N_SPLITS = 9
SPLIT_DTYPE = {S_FF: F32, S_FB: F32}
HEAD_DIM = 128
GLA_CHUNK = 64
GLA_HEADS_PER_STEP = 4
MIX_BLOCK = 512
MLP_CHUNK = 128
TOP_K = 4
MOE_BLOCK = 512
ROW_GROUP = 8
SWIGLU_LIMIT = 7.0
SWIGLU_ALPHA = 1.702
LANES = 128
NEG_BIG = -1e30
VMEM_LIMIT = 56 * 1024 * 1024


def _cparams(sem):
    return pltpu.CompilerParams(dimension_semantics=sem, vmem_limit_bytes=VMEM_LIMIT)


def _resident(shape):
    zeros = (0,) * len(shape)
    return pl.BlockSpec(shape, lambda *_: zeros, pipeline_mode=pl.Buffered(1))


def _mod_kernel(c_ref, w_ref, b_ref, o_ref):
    c = c_ref[...]
    h = (c * jax.nn.sigmoid(c)).astype(BF16)
    o_ref[...] = jnp.dot(h, w_ref[...].astype(BF16), preferred_element_type=F32) + b_ref[...]


def _mod_call(cc, w_mod, b_mod):
    rows, d = cc.shape
    width = w_mod.shape[1]
    return pl.pallas_call(
        _mod_kernel,
        grid=(width // d,),
        in_specs=[pl.BlockSpec((rows, d), lambda j: (0, 0)),
                  pl.BlockSpec((d, d), lambda j: (0, j)),
                  pl.BlockSpec((1, d), lambda j: (0, j))],
        out_specs=pl.BlockSpec((rows, d), lambda j: (0, j)),
        out_shape=jax.ShapeDtypeStruct((rows, width), F32),
        compiler_params=_cparams(("arbitrary",)),
        name="mod",
    )(cc, w_mod, b_mod)


def _lower_bound(lb_ref):
    a = lb_ref[0:1, :]
    b = lb_ref[1:2, :]
    m = jnp.maximum(a, b)
    ea = jnp.exp(a - m)
    eb = jnp.exp(b - m)
    return ea / (ea + eb)


def _gelu(z):
    return 0.5 * z * (1.0 + lax.erf(z * (2.0 ** -0.5)))


def _split_activation(s, z, lbf_ref, lbb_ref):
    if s in (S_Q, S_OG):
        return z * jax.nn.sigmoid(z)
    if s == S_I:
        return z
    if s in (S_FF, S_FB):
        lb = _lower_bound(lbf_ref if s == S_FF else lbb_ref)
        return lb + (1.0 - lb) * jax.nn.sigmoid(z)
    if s in (S_U, S_V):
        return _gelu(z)
    return jax.nn.sigmoid(z)


def _proj_kernel(x_ref, g_ref, sh_ref, sc_ref, w_ref, lbf_ref, lbb_ref, *o_refs, splits):
    d = x_ref.shape[1]
    x = x_ref[...]
    y = x * lax.rsqrt(jnp.mean(x * x, axis=-1, keepdims=True) + EPS) * g_ref[...]
    h = (y * (1.0 + sc_ref[...]) + sh_ref[...]).astype(BF16)
    for s, o_ref in zip(splits, o_refs):
        z = jnp.dot(h, w_ref[:, s * d:(s + 1) * d], preferred_element_type=F32)
        o_ref[...] = _split_activation(s, z, lbf_ref, lbb_ref).astype(o_ref.dtype)


def _proj_call(x, g, sh, sc, w_bf, lbf, lbb, *, splits, tm, per_batch_mod):
    bsz, t, d = x.shape
    mod_map = (lambda b, i: (b, 0, 0)) if per_batch_mod else (lambda b, i: (0, 0, 0))
    row = pl.BlockSpec((None, tm, d), lambda b, i: (b, i, 0))
    return pl.pallas_call(
        functools.partial(_proj_kernel, splits=splits),
        grid=(bsz, t // tm),
        in_specs=[row,
                  pl.BlockSpec((1, d), lambda b, i: (0, 0)),
                  pl.BlockSpec((None, 1, d), mod_map),
                  pl.BlockSpec((None, 1, d), mod_map),
                  _resident(w_bf.shape),
                  pl.BlockSpec((2, d), lambda b, i: (0, 0)),
                  pl.BlockSpec((2, d), lambda b, i: (0, 0))],
        out_specs=[row for _ in splits],
        out_shape=[jax.ShapeDtypeStruct((bsz, t, d), SPLIT_DTYPE.get(s, BF16)) for s in splits],
        compiler_params=_cparams(("arbitrary", "arbitrary")),
        name="proj",
    )(x, g, sh, sc, w_bf, lbf, lbb)


def _prefix_sum(x, reverse):
    n = x.shape[0]
    row = lax.broadcasted_iota(jnp.int32, x.shape, 0)
    s = 1
    while s < n:
        if reverse:
            x = x + jnp.where(row < n - s, pltpu.roll(x, n - s, 0), 0.0)
        else:
            x = x + jnp.where(row >= s, pltpu.roll(x, s, 0), 0.0)
        s *= 2
    return x


def _dot_nt(a, b):
    return lax.dot_general(a, b, (((1,), (1,)), ((), ())), preferred_element_type=F32)


def _dot_tn(a, b):
    return lax.dot_general(a, b, (((0,), (0,)), ((), ())), preferred_element_type=F32)


def _gla_prepare(q, f, reverse):
    c = q.shape[0]
    b = _prefix_sum(jnp.log(f), reverse)
    if reverse:
        b_mid = b[c // 2:c // 2 + 1, :]
        b_end = b[0:1, :]
    else:
        b_mid = b[c // 2 - 1:c // 2, :]
        b_end = b[c - 1:c, :]
    qe = q * jnp.exp(b - b_mid)
    ke = (1.0 - f) * jnp.exp(b_mid - b)
    kd = ke * jnp.exp(b_end - b_mid)
    return qe.astype(BF16), ke.astype(BF16), kd.astype(BF16), jnp.exp(b_mid), jnp.exp(b_end)


def _gla_scores(qe, ke, st, e_mid):
    rhs = jnp.concatenate([(st * e_mid).astype(BF16), ke], axis=0)
    return _dot_nt(qe, rhs)


def _gla_finish(both, kd, vb, st, e_end, reverse):
    c = both.shape[0]
    dv = st.shape[0]
    ti = lax.broadcasted_iota(jnp.int32, (c, c), 0)
    si = lax.broadcasted_iota(jnp.int32, (c, c), 1)
    keep = (si >= ti) if reverse else (si <= ti)
    scores = jnp.where(keep, both[:, dv:], 0.0)
    o = both[:, :dv] + jnp.dot(scores.astype(BF16), vb, preferred_element_type=F32)
    return o, st * e_end + _dot_tn(vb, kd)


def _context_state(f, vb, reverse_decay):
    g = jnp.log(f)
    tail = _prefix_sum(g, reverse_decay) - g
    kt = ((1.0 - f) * jnp.exp(tail)).astype(BF16)
    return _dot_tn(vb, kt)


def _gla_kernel(q_ref, i_ref, ff_ref, fb_ref, og_ref, ci_ref, cff_ref, cfb_ref, gn_ref,
                o_ref, of_ref, ob_ref, sf_ref, sb_ref, ops_f, ops_b, rows_f, rows_b):
    t = q_ref.shape[0]
    c = GLA_CHUNK
    n = t // c
    heads = [slice(h * HEAD_DIM, (h + 1) * HEAD_DIM) for h in range(q_ref.shape[1] // HEAD_DIM)]
    for h, cols in enumerate(heads):
        ci = ci_ref[:, cols]
        sf_ref[h] = _context_state(cff_ref[:, cols], ci, True)
        sb_ref[h] = _context_state(cfb_ref[:, cols], ci, False)

    directions = ((ff_ref, ops_f, rows_f, sf_ref, of_ref, False), (fb_ref, ops_b, rows_b, sb_ref, ob_ref, True))

    def chunk_rows(ck):
        return pl.ds(pl.multiple_of(ck * c, c), c)

    def prepare(step):
        slot = step % 2
        for cols in heads:
            for f_ref, ops, rows, _, _, reverse in directions:
                r = chunk_rows((n - 1 - step) if reverse else step)
                qe, ke, kd, e_mid, e_end = _gla_prepare(q_ref[r, cols].astype(F32), f_ref[r, cols], reverse)
                ops[slot, 0, :, cols] = qe
                ops[slot, 1, :, cols] = ke
                ops[slot, 2, :, cols] = kd
                rows[slot, 0, :, cols] = jnp.broadcast_to(e_mid, (8, HEAD_DIM))
                rows[slot, 1, :, cols] = jnp.broadcast_to(e_end, (8, HEAD_DIM))

    prepare(0)

    def scan(step, carry):
        slot = step % 2
        chains = []
        for h, cols in enumerate(heads):
            for _, ops, rows, s_ref, acc, reverse in directions:
                chains.append((h, cols, ops, rows, s_ref, acc, reverse,
                               chunk_rows((n - 1 - step) if reverse else step)))
        firsts = []
        for h, cols, ops, rows, s_ref, acc, reverse, r in chains:
            st = s_ref[h]
            firsts.append((st, _gla_scores(ops[slot, 0, :, cols], ops[slot, 1, :, cols], st,
                                           rows[slot, 0, 0:1, cols])))
        for (h, cols, ops, rows, s_ref, acc, reverse, r), (st, both) in zip(chains, firsts):
            o, st = _gla_finish(both, ops[slot, 2, :, cols], i_ref[r, cols], st, rows[slot, 1, 0:1, cols],
                                reverse)
            acc[r, cols] = o
            s_ref[h] = st
        prepare(jnp.minimum(step + 1, n - 1))
        return carry

    lax.fori_loop(0, n, scan, 0)
    for h, cols in enumerate(heads):
        o = of_ref[:, cols] + ob_ref[:, cols]
        y = o * lax.rsqrt(jnp.mean(o * o, axis=-1, keepdims=True) + EPS) * gn_ref[...]
        o_ref[:, cols] = (y * og_ref[:, cols].astype(F32)).astype(o_ref.dtype)


def _gla_call(q, i, ff, fb, og, ci, cff, cfb, gnorm):
    bsz, t, d = q.shape
    width = GLA_HEADS_PER_STEP * HEAD_DIM
    tc = ci.shape[1]
    lat = pl.BlockSpec((None, t, width), lambda b, h: (b, 0, h))
    ctx = pl.BlockSpec((None, tc, width), lambda b, h: (b, 0, h))
    out_acc = pltpu.VMEM((t, width), F32)
    state = pltpu.VMEM((GLA_HEADS_PER_STEP, HEAD_DIM, HEAD_DIM), F32)
    operands = pltpu.VMEM((2, 3, GLA_CHUNK, width), BF16)
    decay_rows = pltpu.VMEM((2, 2, 8, width), F32)
    return pl.pallas_call(
        _gla_kernel,
        grid=(bsz, d // width),
        in_specs=[lat, lat, lat, lat, lat, ctx, ctx, ctx,
                  pl.BlockSpec((1, HEAD_DIM), lambda b, h: (0, 0))],
        out_specs=lat,
        out_shape=jax.ShapeDtypeStruct((bsz, t, d), BF16),
        scratch_shapes=[out_acc, out_acc, state, state, operands, operands, decay_rows, decay_rows],
        compiler_params=_cparams(("arbitrary", "arbitrary")),
        name="gla",
    )(q, i, ff, fb, og, ci, cff, cfb, gnorm)


def _store_row_tiles(ref, value):
    rows, d = value.shape
    per_row = d // LANES
    for j in range(per_row):
        ref[pl.ds(j, rows, stride=per_row), :] = value[:, j * LANES:(j + 1) * LANES]


def _load_row_tiles(ref, d):
    per_row = d // LANES
    rows = ref.shape[0] // per_row
    return jnp.concatenate([ref[pl.ds(j, rows, stride=per_row), :] for j in range(per_row)], axis=1)


def _mix_kernel(x_ref, u_ref, v_ref, ga_ref, gb_ref, ya_ref, ws_ref, bs_ref, wa_ref, wb_ref, wo_ref,
                g1_ref, n2_ref, sh2_ref, sc2_ref, wr_ref, br_ref,
                x1_ref, h2_ref, e_ref, w_ref, rank_ref, cnt_ref, yb_ref, carry_ref):
    tm, d = x_ref.shape
    sub = carry_ref.shape[1]
    per_row = d // LANES
    first = (pl.program_id(0) == 0) & (pl.program_id(1) == 0)

    @pl.when(first)
    def _():
        carry_ref[...] = jnp.zeros_like(carry_ref)

    subs = [(s0, pl.ds(s0, sub), slice(s0, s0 + sub)) for s0 in range(0, tm, sub)]

    for s0, rs, _ in subs:
        v = v_ref[rs, :].astype(F32)
        mu = jnp.mean(v, axis=-1, keepdims=True)
        vc = v - mu
        var = jnp.mean(vc * vc, axis=-1, keepdims=True)
        vn = (vc * lax.rsqrt(var + EPS)).astype(BF16)
        for ck in range(sub // MLP_CHUNK):
            rows = slice(ck * MLP_CHUNK, (ck + 1) * MLP_CHUNK)
            orow = slice(s0 + ck * MLP_CHUNK, s0 + (ck + 1) * MLP_CHUNK)
            for g in range(d // HEAD_DIM):
                cols = slice(g * HEAD_DIM, (g + 1) * HEAD_DIM)
                s = jnp.dot(ws_ref[g], vn[rows, cols], preferred_element_type=F32) + bs_ref[g]
                yb_ref[orow, cols] = (u_ref[orow, cols].astype(F32) * s).astype(BF16)

    merged = []
    for _, rs, _ in subs:
        pa = jnp.dot(ya_ref[rs, :], wa_ref[...], preferred_element_type=F32)
        pb = jnp.dot(yb_ref[rs, :], wb_ref[...], preferred_element_type=F32)
        merged.append((ga_ref[rs, :].astype(F32) * pa + gb_ref[rs, :].astype(F32) * pb).astype(BF16))

    logits = []
    for (s0, rs, _), mg in zip(subs, merged):
        mixed = jnp.dot(mg, wo_ref[...], preferred_element_type=F32)
        x1 = x_ref[rs, :] + g1_ref[...] * mixed
        x1_ref[rs, :] = x1
        y = x1 * lax.rsqrt(jnp.mean(x1 * x1, axis=-1, keepdims=True) + EPS) * n2_ref[...]
        h2 = y * (1.0 + sc2_ref[...]) + sh2_ref[...]
        _store_row_tiles(h2_ref.at[pl.ds(s0 * per_row, sub * per_row)], h2)
        logits.append(_dot_nt(wr_ref[...], h2.astype(BF16)) + br_ref[...])

    for (_, _, ls), work in zip(subs, logits):
        n_rows = work.shape[0]
        eidx = lax.broadcasted_iota(jnp.int32, (n_rows, sub), 0)
        vals, idxs = [], []
        for _ in range(TOP_K):
            m = jnp.max(work, axis=0, keepdims=True)
            idx = jnp.min(jnp.where(work == m, eidx, n_rows), axis=0, keepdims=True)
            vals.append(m)
            idxs.append(idx)
            work = jnp.where(eidx == idx, -jnp.inf, work)
        exps = [jnp.exp(m - vals[0]) for m in vals]
        denom = exps[0] + exps[1] + exps[2] + exps[3]

        sel = jnp.zeros((n_rows, sub), F32)
        for idx in idxs:
            sel = sel + jnp.where(eidx == idx, 1.0, 0.0)
        tj = lax.broadcasted_iota(jnp.int32, (sub, sub), 0)
        tt = lax.broadcasted_iota(jnp.int32, (sub, sub), 1)
        tri = jnp.where(tj < tt, 1.0, 0.0).astype(BF16)
        before = carry_ref[...] + jnp.dot(sel.astype(BF16), tri, preferred_element_type=F32)

        out_shape = (e_ref.shape[0], sub)
        krow = lax.broadcasted_iota(jnp.int32, out_shape, 0)
        e_out = jnp.zeros(out_shape, jnp.int32)
        w_out = jnp.zeros(out_shape, F32)
        r_out = jnp.zeros(out_shape, F32)
        for k in range(TOP_K):
            rk = jnp.sum(jnp.where(eidx == idxs[k], before, 0.0), axis=0, keepdims=True)
            e_out = jnp.where(krow == k, idxs[k], e_out)
            w_out = jnp.where(krow == k, exps[k] / denom, w_out)
            r_out = jnp.where(krow == k, rk, r_out)
        e_ref[:, ls] = e_out
        w_ref[:, ls] = w_out
        rank_ref[:, ls] = r_out.astype(jnp.int32)
        carry_ref[...] = carry_ref[...] + jnp.sum(sel, axis=1, keepdims=True)
    cnt_ref[...] = carry_ref[...]


def _mix_call(x, u, v, ga, gb, ya, ws_bf, bs_full, wa_bf, wb_bf, wo_bf, g1, n2, sh2, sc2, wr_t, br_t, *, tm):
    bsz, t, d = x.shape
    n = bsz * t
    nt = t // tm
    n_rows, sub = br_t.shape
    row = pl.BlockSpec((None, tm, d), lambda b, i: (b, i, 0))
    per_batch = pl.BlockSpec((None, 1, d), lambda b, i: (b, 0, 0))
    per_token = pl.BlockSpec((ROW_GROUP, tm), lambda b, i: (0, b * nt + i))

    return pl.pallas_call(
        _mix_kernel,
        grid=(bsz, nt),
        in_specs=[row, row, row, row, row, row,
                  _resident(ws_bf.shape), _resident(bs_full.shape),
                  _resident((d, d)), _resident((d, d)), _resident((d, d)),
                  per_batch, _resident((1, d)), per_batch, per_batch,
                  _resident(wr_t.shape), _resident(br_t.shape)],
        out_specs=[row, pl.BlockSpec((tm * d // LANES, LANES), lambda b, i: (b * nt + i, 0)),
                   per_token, per_token, per_token,
                   pl.BlockSpec((n_rows, sub), lambda b, i: (0, 0))],
        out_shape=[jax.ShapeDtypeStruct((bsz, t, d), F32),
                   jax.ShapeDtypeStruct((n * d // LANES, LANES), F32),
                   jax.ShapeDtypeStruct((ROW_GROUP, n), jnp.int32),
                   jax.ShapeDtypeStruct((ROW_GROUP, n), F32),
                   jax.ShapeDtypeStruct((ROW_GROUP, n), jnp.int32),
                   jax.ShapeDtypeStruct((n_rows, sub), F32)],
        scratch_shapes=[pltpu.VMEM((tm, d), BF16), pltpu.VMEM((n_rows, sub), F32)],
        compiler_params=_cparams(("arbitrary", "arbitrary")),
        name="mix",
    )(x, u, v, ga, gb, ya, ws_bf, bs_full, wa_bf, wb_bf, wo_bf, g1, n2, sh2, sc2, wr_t, br_t)


def _slot_kernel(starts_ref, e_ref, r_ref, o_ref):
    e = e_ref[...]
    start = jnp.zeros(e.shape, jnp.int32)
    for x in range(starts_ref.shape[0]):
        start = jnp.where(e == x, starts_ref[x], start)
    o_ref[...] = start + r_ref[...]


def _slot_call(pad_starts, e_t, r_t, *, tn):
    rows, n = e_t.shape
    blk = pl.BlockSpec((rows, tn), lambda i, starts: (0, i))
    return pl.pallas_call(
        _slot_kernel,
        grid_spec=pltpu.PrefetchScalarGridSpec(
            num_scalar_prefetch=1, grid=(n // tn,), in_specs=[blk, blk], out_specs=blk),
        out_shape=jax.ShapeDtypeStruct((rows, n), jnp.int32),
        compiler_params=_cparams(("arbitrary",)),
        name="slots",
    )(pad_starts, e_t, r_t)


def _padfill_kernel(ends_ref, padded_ref, xb_ref, zero_ref, sem):
    zero_ref[...] = jnp.zeros_like(zero_ref)
    n_exp = ends_ref.shape[0]
    per_row = zero_ref.shape[0] // MOE_BLOCK

    def tail_copy(e):
        start = pl.multiple_of((ends_ref[e] - MOE_BLOCK) * per_row, MOE_BLOCK)
        return pltpu.make_async_copy(zero_ref, xb_ref.at[pl.ds(start, MOE_BLOCK * per_row)], sem)

    for e in range(n_exp):
        @pl.when(padded_ref[e] > 0)
        def _():
            tail_copy(e).start()
    for e in range(n_exp):
        @pl.when(padded_ref[e] > 0)
        def _():
            tail_copy(e).wait()


def _padfill_call(pad_ends, padded, cap, d):
    return pl.pallas_call(
        _padfill_kernel,
        grid_spec=pltpu.PrefetchScalarGridSpec(
            num_scalar_prefetch=2, grid=(1,), in_specs=[],
            out_specs=pl.BlockSpec(memory_space=pl.ANY),
            scratch_shapes=[pltpu.VMEM((MOE_BLOCK * d // LANES, LANES), F32), pltpu.SemaphoreType.DMA]),
        out_shape=jax.ShapeDtypeStruct((cap * d // LANES, LANES), F32),
        compiler_params=_cparams(("arbitrary",)),
        name="padfill",
    )(pad_ends, padded)


def _row_copy(src_ref, src_row, dst_ref, dst_row, sem, per_row):
    src = pl.ds(pl.multiple_of(src_row * per_row, per_row), per_row)
    dst = pl.ds(pl.multiple_of(dst_row * per_row, per_row), per_row)
    return pltpu.make_async_copy(src_ref.at[src], dst_ref.at[dst], sem)


def _for_row_groups(n_rows, fn):
    def body(g, carry):
        base = pl.multiple_of(g * ROW_GROUP, ROW_GROUP)
        for rr in range(ROW_GROUP):
            fn(base + rr)
        return carry
    lax.fori_loop(0, n_rows // ROW_GROUP, body, 0)


DISPATCH_RING = 3


def _dispatch_kernel(dest_ref, h_ref, xin_ref, xb_ref, hbuf, load_sems, row_sems):
    del xin_ref
    tm = dest_ref.shape[2] // TOP_K
    per_row = hbuf.shape[1] // tm
    i = pl.program_id(0)
    last = pl.num_programs(0) - 1
    slot = i % DISPATCH_RING

    def load(tile, s):
        start = pl.multiple_of(tile * tm * per_row, tm * per_row)
        return pltpu.make_async_copy(h_ref.at[pl.ds(start, tm * per_row)], hbuf.at[s], load_sems.at[s])

    def drain(s):
        for _ in range(TOP_K):
            pltpu.make_async_copy(hbuf.at[s], xb_ref.at[pl.ds(0, tm * per_row)], row_sems.at[s]).wait()

    @pl.when(i == 0)
    def _():
        load(0, 0).start()

    load(i, slot).wait()

    @pl.when(i < last)
    def _():
        load(i + 1, (i + 1) % DISPATCH_RING).start()

    def issue(r):
        for k in range(TOP_K):
            _row_copy(hbuf.at[slot], r, xb_ref, dest_ref[0, 0, k * tm + r],
                      row_sems.at[slot], per_row).start(priority=k % 2)

    _for_row_groups(tm, issue)

    @pl.when(i >= 1)
    def _():
        drain((i + DISPATCH_RING - 1) % DISPATCH_RING)

    @pl.when(i == last)
    def _():
        drain(slot)


def _dispatch_call(dest_tiles, h2, xb_init, *, tm):
    n = dest_tiles.shape[0] * tm
    return pl.pallas_call(
        _dispatch_kernel,
        grid=(n // tm,),
        in_specs=[pl.BlockSpec((1, 1, tm * TOP_K), lambda i: (i, 0, 0), memory_space=pltpu.SMEM),
                  pl.BlockSpec(memory_space=pl.ANY),
                  pl.BlockSpec(memory_space=pl.ANY)],
        out_specs=pl.BlockSpec(memory_space=pl.ANY),
        out_shape=jax.ShapeDtypeStruct(xb_init.shape, F32),
        scratch_shapes=[pltpu.VMEM((DISPATCH_RING, h2.shape[0] // n * tm, LANES), F32),
                        pltpu.SemaphoreType.DMA((DISPATCH_RING,)), pltpu.SemaphoreType.DMA((DISPATCH_RING,))],
        input_output_aliases={2: 0},
        compiler_params=_cparams(("arbitrary",)),
        name="dispatch",
    )(dest_tiles, h2, xb_init)


def _expert_kernel(be_ref, nu_ref, x_ref, w1_ref, b1_ref, w2_ref, b2_ref, y_ref, w1b_ref, w2b_ref):
    i = pl.program_id(0)
    f = w2_ref.shape[1]
    active = i < nu_ref[0]

    @pl.when(active & ((i == 0) | (be_ref[i] != be_ref[jnp.maximum(i - 1, 0)])))
    def _():
        w1b_ref[...] = w1_ref[0].astype(BF16)
        w2b_ref[...] = w2_ref[0].astype(BF16)

    @pl.when(active)
    def _():
        x = _load_row_tiles(x_ref, w1b_ref.shape[0])
        z = jnp.dot(x.astype(BF16), w1b_ref[...], preferred_element_type=F32) + b1_ref[0]
        gate = jnp.minimum(z[:, :f], SWIGLU_LIMIT)
        lin = jnp.clip(z[:, f:], -SWIGLU_LIMIT, SWIGLU_LIMIT)
        act = gate * jax.nn.sigmoid(SWIGLU_ALPHA * gate) * (lin + 1.0)
        _store_row_tiles(y_ref, jnp.dot(act.astype(BF16), w2b_ref[...], preferred_element_type=F32) + b2_ref[0])


def _expert_call(block_e, n_used, xb, w1, b1, w2, b2):
    d = w1.shape[1]
    per_row = d // LANES
    n_blocks = xb.shape[0] // (MOE_BLOCK * per_row)
    f2 = w1.shape[2]
    f = w2.shape[1]

    def blk(i, be, nu):
        return jnp.minimum(i, nu[0] - 1)

    grid_spec = pltpu.PrefetchScalarGridSpec(
        num_scalar_prefetch=2,
        grid=(n_blocks,),
        in_specs=[pl.BlockSpec((MOE_BLOCK * per_row, LANES), lambda i, be, nu: (blk(i, be, nu), 0)),
                  pl.BlockSpec((1, d, f2), lambda i, be, nu: (be[blk(i, be, nu)], 0, 0)),
                  pl.BlockSpec((1, 1, f2), lambda i, be, nu: (be[blk(i, be, nu)], 0, 0)),
                  pl.BlockSpec((1, f, d), lambda i, be, nu: (be[blk(i, be, nu)], 0, 0)),
                  pl.BlockSpec((1, 1, d), lambda i, be, nu: (be[blk(i, be, nu)], 0, 0))],
        out_specs=pl.BlockSpec((MOE_BLOCK * per_row, LANES), lambda i, be, nu: (blk(i, be, nu), 0)),
        scratch_shapes=[pltpu.VMEM((d, f2), BF16), pltpu.VMEM((f, d), BF16)],
    )
    return pl.pallas_call(
        _expert_kernel,
        grid_spec=grid_spec,
        out_shape=jax.ShapeDtypeStruct(xb.shape, F32),
        compiler_params=_cparams(("arbitrary",)),
        name="experts",
    )(block_e, n_used, xb, w1, b1, w2, b2)


def _combine_kernel(dest_ref, nxt_ref, x1_ref, w_ref, g2_ref, fg_ref, y_ref, o_ref, buf_ref, sems):
    tm, d = x1_ref.shape
    per_row = d // LANES
    nt = pl.num_programs(1)
    i = pl.program_id(0) * nt + pl.program_id(1)
    last = pl.num_programs(0) * nt - 1
    cur = i % 2

    def gather(tab_ref, slot):
        def issue(r):
            for k in range(TOP_K):
                _row_copy(y_ref, tab_ref[0, 0, k * tm + r], buf_ref.at[slot, k], r,
                          sems.at[slot], per_row).start(priority=k % 2)
        _for_row_groups(tm, issue)

    @pl.when(i == 0)
    def _():
        gather(dest_ref, 0)

    @pl.when(i < last)
    def _():
        gather(nxt_ref, 1 - cur)

    for k in range(TOP_K):
        pltpu.make_async_copy(y_ref.at[pl.ds(0, tm * per_row)], buf_ref.at[cur, k], sems.at[cur]).wait()

    w = w_ref[...]
    moe = w[:, 0:1] * _load_row_tiles(buf_ref.at[cur, 0], d)
    for k in range(1, TOP_K):
        moe = moe + w[:, k:k + 1] * _load_row_tiles(buf_ref.at[cur, k], d)
    x2 = x1_ref[...] + g2_ref[...] * moe
    o_ref[...] = x2 * lax.rsqrt(jnp.mean(x2 * x2, axis=-1, keepdims=True) + EPS) * fg_ref[...]


def _combine_call(dest_tiles, x1, w_lanes, g2, final_g, y, *, tm):
    bsz, t, d = x1.shape
    nt = t // tm
    tiles = bsz * nt

    def table(shift):
        return pl.BlockSpec((1, 1, tm * TOP_K), lambda b, i: (jnp.minimum(b * nt + i + shift, tiles - 1), 0, 0),
                            memory_space=pltpu.SMEM)

    return pl.pallas_call(
        _combine_kernel,
        grid=(bsz, nt),
        in_specs=[table(0), table(1),
                  pl.BlockSpec((None, tm, d), lambda b, i: (b, i, 0)),
                  pl.BlockSpec((tm, LANES), lambda b, i: (b * nt + i, 0)),
                  pl.BlockSpec((None, 1, d), lambda b, i: (b, 0, 0)),
                  pl.BlockSpec((1, d), lambda b, i: (0, 0)),
                  pl.BlockSpec(memory_space=pl.ANY)],
        out_specs=pl.BlockSpec((None, tm, d), lambda b, i: (b, i, 0)),
        out_shape=jax.ShapeDtypeStruct((bsz, t, d), F32),
        scratch_shapes=[pltpu.VMEM((2, TOP_K, tm * d // LANES, LANES), F32), pltpu.SemaphoreType.DMA((2,))],
        compiler_params=_cparams(("arbitrary", "arbitrary")),
        name="combine",
    )(dest_tiles, dest_tiles, x1, w_lanes, g2, final_g, y)


def kernel(x, c, ctx, c_ctx, norm1_g, norm2_g, w_mod, b_mod, w_in, lb_fwd, lb_bwd, gnorm_g, w_s, b_s,
           w_branch_a, w_branch_b, w_out, w_router, b_router, w1, b1, w2, b2, final_g):
    bsz, t, d = x.shape
    assert w_in.shape[0] == 1 and lb_fwd.shape[0] == 2, "single-layer block"
    assert w_in.shape[2] == N_SPLITS * d and d % HEAD_DIM == 0
    assert t % MLP_CHUNK == 0 and ctx.shape[1] % GLA_CHUNK == 0
    n = bsz * t

    pad_rows = (-(bsz + 1)) % 8
    cc = jnp.concatenate([c, c_ctx[None, :], jnp.zeros((pad_rows, d), F32)], axis=0)
    mod = _mod_call(cc, w_mod[0], b_mod[0][None, :])
    lat = mod[:bsz].reshape(bsz, N_MOD, 1, d)
    sh1, sc1, g1, sh2, sc2, g2 = (lat[:, m] for m in range(N_MOD))
    cmod = mod[bsz].reshape(N_MOD, 1, 1, d)
    csh1, csc1 = cmod[0], cmod[1]

    w_in_bf = w_in[0].astype(BF16)
    n1 = norm1_g[0][None, :]
    q, i_, ff, fb, og, u, v, ga, gb = _proj_call(
        x, n1, sh1, sc1, w_in_bf, lb_fwd, lb_bwd,
        splits=tuple(range(N_SPLITS)), tm=min(256, t), per_batch_mod=True)
    ci, cff, cfb = _proj_call(
        ctx, n1, csh1, csc1, w_in_bf, lb_fwd, lb_bwd,
        splits=(S_I, S_FF, S_FB), tm=ctx.shape[1], per_batch_mod=False)

    ya = _gla_call(q, i_, ff, fb, og, ci, cff, cfb, gnorm_g[0][None, :])

    n_exp = w_router.shape[2]
    tm_io = min(256, t)
    exp_rows = -(-n_exp // ROW_GROUP) * ROW_GROUP
    wr_t = jnp.zeros((exp_rows, d), BF16).at[:n_exp].set(w_router[0].T.astype(BF16))
    br_t = jnp.broadcast_to(jnp.full((exp_rows,), NEG_BIG, F32).at[:n_exp].set(b_router[0])[:, None],
                            (exp_rows, tm_io))
    bs_full = jnp.broadcast_to(b_s[0][:, :, None], b_s.shape[1:] + (HEAD_DIM,))
    x1, h2, e_t, w_t, r_t, cnt = _mix_call(
        x, u, v, ga, gb, ya, w_s[0].astype(BF16), bs_full,
        w_branch_a[0].astype(BF16), w_branch_b[0].astype(BF16), w_out[0].astype(BF16),
        g1, norm2_g[0][None, :], sh2, sc2, wr_t, br_t, tm=min(MIX_BLOCK, t))

    counts = cnt[:n_exp, 0].astype(jnp.int32)
    padded = (counts + MOE_BLOCK - 1) // MOE_BLOCK * MOE_BLOCK
    pad_ends = jnp.cumsum(padded)
    pad_starts = pad_ends - padded
    n_blocks = -(-(n * TOP_K) // MOE_BLOCK) + n_exp
    cap = n_blocks * MOE_BLOCK
    dest = _slot_call(pad_starts, e_t, r_t, tn=min(8192, n))[:TOP_K]
    blk_start = jnp.arange(n_blocks, dtype=jnp.int32) * MOE_BLOCK
    block_e = jnp.minimum(jnp.sum((pad_ends[None, :] <= blk_start[:, None]).astype(jnp.int32), axis=1),
                          n_exp - 1)
    n_used = (pad_ends[-1:] // MOE_BLOCK).astype(jnp.int32)

    dest_tiles = dest.reshape(TOP_K, n // tm_io, tm_io).transpose(1, 0, 2).reshape(n // tm_io, 1, TOP_K * tm_io)
    w_l = jnp.zeros((n, LANES), F32).at[:, :TOP_K].set(w_t[:TOP_K].T)
    xb = _dispatch_call(dest_tiles, h2, _padfill_call(pad_ends, padded, cap, d), tm=tm_io)
    y = _expert_call(block_e, n_used, xb, w1[0], b1[0][:, None, :], w2[0], b2[0][:, None, :])
    return _combine_call(dest_tiles, x1, w_l, g2, final_g[None, :], y, tm=tm_io)
```

```python
import functools

import jax
import jax.numpy as jnp
from jax import lax
from jax.experimental import pallas as pl
from jax.experimental.pallas import tpu as pltpu

F32 = jnp.float32
BF16 = jnp.bfloat16

EPS = 1e-6
N_MOD = 6
S_Q, S_I, S_FF, S_FB, S_OG, S_U, S_V, S_GA, S_GB = range(9)
N_SPLITS = 9
SPLIT_DTYPE = {S_FF: F32, S_FB: F32}
HEAD_DIM = 128
GLA_CHUNK = 64
GLA_HEADS_PER_STEP = 4
MIX_BLOCK = 512
MLP_CHUNK = 128
TOP_K = 4
MOE_BLOCK = 512
ROW_GROUP = 8
COMBINE_ROWS = 64
SWIGLU_LIMIT = 7.0
SWIGLU_ALPHA = 1.702
LANES = 128
NEG_BIG = -1e30
VMEM_LIMIT = 56 * 1024 * 1024


def _cparams(sem):
    return pltpu.CompilerParams(dimension_semantics=sem, vmem_limit_bytes=VMEM_LIMIT)


def _resident(shape):
    zeros = (0,) * len(shape)
    return pl.BlockSpec(shape, lambda *_: zeros, pipeline_mode=pl.Buffered(1))


def _mod_kernel(c_ref, w_ref, b_ref, o_ref):
    c = c_ref[...]
    h = (c * jax.nn.sigmoid(c)).astype(BF16)
    o_ref[...] = jnp.dot(h, w_ref[...].astype(BF16), preferred_element_type=F32) + b_ref[...]


def _mod_call(cc, w_mod, b_mod):
    rows, d = cc.shape
    width = w_mod.shape[1]
    return pl.pallas_call(
        _mod_kernel,
        grid=(width // d,),
        in_specs=[pl.BlockSpec((rows, d), lambda j: (0, 0)),
                  pl.BlockSpec((d, d), lambda j: (0, j)),
                  pl.BlockSpec((1, d), lambda j: (0, j))],
        out_specs=pl.BlockSpec((rows, d), lambda j: (0, j)),
        out_shape=jax.ShapeDtypeStruct((rows, width), F32),
        compiler_params=_cparams(("arbitrary",)),
        name="mod",
    )(cc, w_mod, b_mod)


def _lower_bound(lb_ref):
    a = lb_ref[0:1, :]
    b = lb_ref[1:2, :]
    m = jnp.maximum(a, b)
    ea = jnp.exp(a - m)
    eb = jnp.exp(b - m)
    return ea / (ea + eb)


def _gelu(z):
    return 0.5 * z * (1.0 + lax.erf(z * (2.0 ** -0.5)))


def _split_activation(s, z, lbf_ref, lbb_ref):
    if s in (S_Q, S_OG):
        return z * jax.nn.sigmoid(z)
    if s == S_I:
        return z
    if s in (S_FF, S_FB):
        lb = _lower_bound(lbf_ref if s == S_FF else lbb_ref)
        return lb + (1.0 - lb) * jax.nn.sigmoid(z)
    if s in (S_U, S_V):
        return _gelu(z)
    return jax.nn.sigmoid(z)


def _proj_kernel(x_ref, g_ref, sh_ref, sc_ref, w_ref, lbf_ref, lbb_ref, *o_refs, splits):
    d = x_ref.shape[1]
    x = x_ref[...]
    y = x * lax.rsqrt(jnp.mean(x * x, axis=-1, keepdims=True) + EPS) * g_ref[...]
    h = (y * (1.0 + sc_ref[...]) + sh_ref[...]).astype(BF16)
    for s, o_ref in zip(splits, o_refs):
        z = jnp.dot(h, w_ref[:, s * d:(s + 1) * d], preferred_element_type=F32)
        o_ref[...] = _split_activation(s, z, lbf_ref, lbb_ref).astype(o_ref.dtype)


def _proj_call(x, g, sh, sc, w_bf, lbf, lbb, *, splits, tm, per_batch_mod):
    bsz, t, d = x.shape
    mod_map = (lambda b, i: (b, 0, 0)) if per_batch_mod else (lambda b, i: (0, 0, 0))
    row = pl.BlockSpec((None, tm, d), lambda b, i: (b, i, 0))
    return pl.pallas_call(
        functools.partial(_proj_kernel, splits=splits),
        grid=(bsz, t // tm),
        in_specs=[row,
                  pl.BlockSpec((1, d), lambda b, i: (0, 0)),
                  pl.BlockSpec((None, 1, d), mod_map),
                  pl.BlockSpec((None, 1, d), mod_map),
                  _resident(w_bf.shape),
                  pl.BlockSpec((2, d), lambda b, i: (0, 0)),
                  pl.BlockSpec((2, d), lambda b, i: (0, 0))],
        out_specs=[row for _ in splits],
        out_shape=[jax.ShapeDtypeStruct((bsz, t, d), SPLIT_DTYPE.get(s, BF16)) for s in splits],
        compiler_params=_cparams(("arbitrary", "arbitrary")),
        name="proj",
    )(x, g, sh, sc, w_bf, lbf, lbb)


def _prefix_sum(x, reverse):
    n = x.shape[0]
    row = lax.broadcasted_iota(jnp.int32, x.shape, 0)
    s = 1
    while s < n:
        if reverse:
            x = x + jnp.where(row < n - s, pltpu.roll(x, n - s, 0), 0.0)
        else:
            x = x + jnp.where(row >= s, pltpu.roll(x, s, 0), 0.0)
        s *= 2
    return x


def _dot_nt(a, b):
    return lax.dot_general(a, b, (((1,), (1,)), ((), ())), preferred_element_type=F32)


def _dot_tn(a, b):
    return lax.dot_general(a, b, (((0,), (0,)), ((), ())), preferred_element_type=F32)


def _gla_prepare(q, f, reverse):
    c = q.shape[0]
    b = _prefix_sum(jnp.log(f), reverse)
    if reverse:
        b_mid = b[c // 2:c // 2 + 1, :]
        b_end = b[0:1, :]
    else:
        b_mid = b[c // 2 - 1:c // 2, :]
        b_end = b[c - 1:c, :]
    qe = q * jnp.exp(b - b_mid)
    ke = (1.0 - f) * jnp.exp(b_mid - b)
    kd = ke * jnp.exp(b_end - b_mid)
    return qe.astype(BF16), ke.astype(BF16), kd.astype(BF16), jnp.exp(b_mid), jnp.exp(b_end)


def _gla_scores(qe, ke, st, e_mid):
    rhs = jnp.concatenate([(st * e_mid).astype(BF16), ke], axis=0)
    return _dot_nt(qe, rhs)


def _gla_finish(both, kd, vb, st, e_end, reverse):
    c = both.shape[0]
    dv = st.shape[0]
    ti = lax.broadcasted_iota(jnp.int32, (c, c), 0)
    si = lax.broadcasted_iota(jnp.int32, (c, c), 1)
    keep = (si >= ti) if reverse else (si <= ti)
    scores = jnp.where(keep, both[:, dv:], 0.0)
    o = both[:, :dv] + jnp.dot(scores.astype(BF16), vb, preferred_element_type=F32)
    return o, st * e_end + _dot_tn(vb, kd)


def _context_state(f, vb, reverse_decay):
    g = jnp.log(f)
    tail = _prefix_sum(g, reverse_decay) - g
    kt = ((1.0 - f) * jnp.exp(tail)).astype(BF16)
    return _dot_tn(vb, kt)


def _gla_kernel(q_ref, i_ref, ff_ref, fb_ref, og_ref, ci_ref, cff_ref, cfb_ref, gn_ref,
                o_ref, of_ref, ob_ref, sf_ref, sb_ref, ops_f, ops_b, rows_f, rows_b):
    t = q_ref.shape[0]
    c = GLA_CHUNK
    n = t // c
    heads = [slice(h * HEAD_DIM, (h + 1) * HEAD_DIM) for h in range(q_ref.shape[1] // HEAD_DIM)]
    for h, cols in enumerate(heads):
        ci = ci_ref[:, cols]
        sf_ref[h] = _context_state(cff_ref[:, cols], ci, True)
        sb_ref[h] = _context_state(cfb_ref[:, cols], ci, False)

    directions = ((ff_ref, ops_f, rows_f, sf_ref, of_ref, False), (fb_ref, ops_b, rows_b, sb_ref, ob_ref, True))

    def chunk_rows(ck):
        return pl.ds(pl.multiple_of(ck * c, c), c)

    def prepare(step):
        slot = step % 2
        for cols in heads:
            for f_ref, ops, rows, _, _, reverse in directions:
                r = chunk_rows((n - 1 - step) if reverse else step)
                qe, ke, kd, e_mid, e_end = _gla_prepare(q_ref[r, cols].astype(F32), f_ref[r, cols], reverse)
                ops[slot, 0, :, cols] = qe
                ops[slot, 1, :, cols] = ke
                ops[slot, 2, :, cols] = kd
                rows[slot, 0, :, cols] = jnp.broadcast_to(e_mid, (8, HEAD_DIM))
                rows[slot, 1, :, cols] = jnp.broadcast_to(e_end, (8, HEAD_DIM))

    prepare(0)

    def scan(step, carry):
        slot = step % 2
        chains = []
        for h, cols in enumerate(heads):
            for _, ops, rows, s_ref, acc, reverse in directions:
                chains.append((h, cols, ops, rows, s_ref, acc, reverse,
                               chunk_rows((n - 1 - step) if reverse else step)))
        firsts = []
        for h, cols, ops, rows, s_ref, acc, reverse, r in chains:
            st = s_ref[h]
            firsts.append((st, _gla_scores(ops[slot, 0, :, cols], ops[slot, 1, :, cols], st,
                                           rows[slot, 0, 0:1, cols])))
        for (h, cols, ops, rows, s_ref, acc, reverse, r), (st, both) in zip(chains, firsts):
            o, st = _gla_finish(both, ops[slot, 2, :, cols], i_ref[r, cols], st, rows[slot, 1, 0:1, cols],
                                reverse)
            acc[r, cols] = o
            s_ref[h] = st
        prepare(jnp.minimum(step + 1, n - 1))
        return carry

    lax.fori_loop(0, n, scan, 0)
    for h, cols in enumerate(heads):
        o = of_ref[:, cols] + ob_ref[:, cols]
        y = o * lax.rsqrt(jnp.mean(o * o, axis=-1, keepdims=True) + EPS) * gn_ref[...]
        o_ref[:, cols] = (y * og_ref[:, cols].astype(F32)).astype(o_ref.dtype)


def _gla_call(q, i, ff, fb, og, ci, cff, cfb, gnorm):
    bsz, t, d = q.shape
    width = GLA_HEADS_PER_STEP * HEAD_DIM
    tc = ci.shape[1]
    lat = pl.BlockSpec((None, t, width), lambda b, h: (b, 0, h))
    ctx = pl.BlockSpec((None, tc, width), lambda b, h: (b, 0, h))
    out_acc = pltpu.VMEM((t, width), F32)
    state = pltpu.VMEM((GLA_HEADS_PER_STEP, HEAD_DIM, HEAD_DIM), F32)
    operands = pltpu.VMEM((2, 3, GLA_CHUNK, width), BF16)
    decay_rows = pltpu.VMEM((2, 2, 8, width), F32)
    return pl.pallas_call(
        _gla_kernel,
        grid=(bsz, d // width),
        in_specs=[lat, lat, lat, lat, lat, ctx, ctx, ctx,
                  pl.BlockSpec((1, HEAD_DIM), lambda b, h: (0, 0))],
        out_specs=lat,
        out_shape=jax.ShapeDtypeStruct((bsz, t, d), BF16),
        scratch_shapes=[out_acc, out_acc, state, state, operands, operands, decay_rows, decay_rows],
        compiler_params=_cparams(("arbitrary", "arbitrary")),
        name="gla",
    )(q, i, ff, fb, og, ci, cff, cfb, gnorm)


def _store_row_tiles(ref, value):
    rows, d = value.shape
    per_row = d // LANES
    for j in range(per_row):
        ref[pl.ds(j, rows, stride=per_row), :] = value[:, j * LANES:(j + 1) * LANES]


def _load_row_tiles(ref, d):
    per_row = d // LANES
    rows = ref.shape[0] // per_row
    return jnp.concatenate([ref[pl.ds(j, rows, stride=per_row), :] for j in range(per_row)], axis=1)


def _mix_kernel(x_ref, u_ref, v_ref, ga_ref, gb_ref, ya_ref, ws_ref, bs_ref, wa_ref, wb_ref, wo_ref,
                g1_ref, n2_ref, sh2_ref, sc2_ref, wr_ref, br_ref,
                x1_ref, h2_ref, e_ref, w_ref, rank_ref, cnt_ref, yb_ref, carry_ref):
    tm, d = x_ref.shape
    sub = carry_ref.shape[1]
    per_row = d // LANES
    first = (pl.program_id(0) == 0) & (pl.program_id(1) == 0)

    @pl.when(first)
    def _():
        carry_ref[...] = jnp.zeros_like(carry_ref)

    subs = [(s0, pl.ds(s0, sub), slice(s0, s0 + sub)) for s0 in range(0, tm, sub)]

    for s0, rs, _ in subs:
        v = v_ref[rs, :].astype(F32)
        mu = jnp.mean(v, axis=-1, keepdims=True)
        vc = v - mu
        var = jnp.mean(vc * vc, axis=-1, keepdims=True)
        vn = (vc * lax.rsqrt(var + EPS)).astype(BF16)
        for ck in range(sub // MLP_CHUNK):
            rows = slice(ck * MLP_CHUNK, (ck + 1) * MLP_CHUNK)
            orow = slice(s0 + ck * MLP_CHUNK, s0 + (ck + 1) * MLP_CHUNK)
            for g in range(d // HEAD_DIM):
                cols = slice(g * HEAD_DIM, (g + 1) * HEAD_DIM)
                s = jnp.dot(ws_ref[g], vn[rows, cols], preferred_element_type=F32) + bs_ref[g]
                yb_ref[orow, cols] = (u_ref[orow, cols].astype(F32) * s).astype(BF16)

    merged = []
    for _, rs, _ in subs:
        pa = jnp.dot(ya_ref[rs, :], wa_ref[...], preferred_element_type=F32)
        pb = jnp.dot(yb_ref[rs, :], wb_ref[...], preferred_element_type=F32)
        merged.append((ga_ref[rs, :].astype(F32) * pa + gb_ref[rs, :].astype(F32) * pb).astype(BF16))

    logits = []
    for (s0, rs, _), mg in zip(subs, merged):
        mixed = jnp.dot(mg, wo_ref[...], preferred_element_type=F32)
        x1 = x_ref[rs, :] + g1_ref[...] * mixed
        x1_ref[rs, :] = x1
        y = x1 * lax.rsqrt(jnp.mean(x1 * x1, axis=-1, keepdims=True) + EPS) * n2_ref[...]
        h2 = y * (1.0 + sc2_ref[...]) + sh2_ref[...]
        _store_row_tiles(h2_ref.at[pl.ds(s0 * per_row, sub * per_row)], h2)
        logits.append(_dot_nt(wr_ref[...], h2.astype(BF16)) + br_ref[...])

    for (_, _, ls), work in zip(subs, logits):
        n_rows = work.shape[0]
        eidx = lax.broadcasted_iota(jnp.int32, (n_rows, sub), 0)
        vals, idxs = [], []
        for _ in range(TOP_K):
            m = jnp.max(work, axis=0, keepdims=True)
            idx = jnp.min(jnp.where(work == m, eidx, n_rows), axis=0, keepdims=True)
            vals.append(m)
            idxs.append(idx)
            work = jnp.where(eidx == idx, -jnp.inf, work)
        exps = [jnp.exp(m - vals[0]) for m in vals]
        denom = exps[0] + exps[1] + exps[2] + exps[3]

        sel = jnp.zeros((n_rows, sub), F32)
        for idx in idxs:
            sel = sel + jnp.where(eidx == idx, 1.0, 0.0)
        tj = lax.broadcasted_iota(jnp.int32, (sub, sub), 0)
        tt = lax.broadcasted_iota(jnp.int32, (sub, sub), 1)
        tri = jnp.where(tj < tt, 1.0, 0.0).astype(BF16)
        before = carry_ref[...] + jnp.dot(sel.astype(BF16), tri, preferred_element_type=F32)

        out_shape = (e_ref.shape[0], sub)
        krow = lax.broadcasted_iota(jnp.int32, out_shape, 0)
        e_out = jnp.zeros(out_shape, jnp.int32)
        w_out = jnp.zeros(out_shape, F32)
        r_out = jnp.zeros(out_shape, F32)
        for k in range(TOP_K):
            rk = jnp.sum(jnp.where(eidx == idxs[k], before, 0.0), axis=0, keepdims=True)
            e_out = jnp.where(krow == k, idxs[k], e_out)
            w_out = jnp.where(krow == k, exps[k] / denom, w_out)
            r_out = jnp.where(krow == k, rk, r_out)
        e_ref[:, ls] = e_out
        w_ref[:, ls] = w_out
        rank_ref[:, ls] = r_out.astype(jnp.int32)
        carry_ref[...] = carry_ref[...] + jnp.sum(sel, axis=1, keepdims=True)
    cnt_ref[...] = carry_ref[...]


def _mix_call(x, u, v, ga, gb, ya, ws_bf, bs_full, wa_bf, wb_bf, wo_bf, g1, n2, sh2, sc2, wr_t, br_t, *, tm):
    bsz, t, d = x.shape
    n = bsz * t
    nt = t // tm
    n_rows, sub = br_t.shape
    row = pl.BlockSpec((None, tm, d), lambda b, i: (b, i, 0))
    per_batch = pl.BlockSpec((None, 1, d), lambda b, i: (b, 0, 0))
    per_token = pl.BlockSpec((ROW_GROUP, tm), lambda b, i: (0, b * nt + i))

    return pl.pallas_call(
        _mix_kernel,
        grid=(bsz, nt),
        in_specs=[row, row, row, row, row, row,
                  _resident(ws_bf.shape), _resident(bs_full.shape),
                  _resident((d, d)), _resident((d, d)), _resident((d, d)),
                  per_batch, _resident((1, d)), per_batch, per_batch,
                  _resident(wr_t.shape), _resident(br_t.shape)],
        out_specs=[row, pl.BlockSpec((tm * d // LANES, LANES), lambda b, i: (b * nt + i, 0)),
                   per_token, per_token, per_token,
                   pl.BlockSpec((n_rows, sub), lambda b, i: (0, 0))],
        out_shape=[jax.ShapeDtypeStruct((bsz, t, d), F32),
                   jax.ShapeDtypeStruct((n * d // LANES, LANES), F32),
                   jax.ShapeDtypeStruct((ROW_GROUP, n), jnp.int32),
                   jax.ShapeDtypeStruct((ROW_GROUP, n), F32),
                   jax.ShapeDtypeStruct((ROW_GROUP, n), jnp.int32),
                   jax.ShapeDtypeStruct((n_rows, sub), F32)],
        scratch_shapes=[pltpu.VMEM((tm, d), BF16), pltpu.VMEM((n_rows, sub), F32)],
        compiler_params=_cparams(("arbitrary", "arbitrary")),
        name="mix",
    )(x, u, v, ga, gb, ya, ws_bf, bs_full, wa_bf, wb_bf, wo_bf, g1, n2, sh2, sc2, wr_t, br_t)


def _slot_kernel(starts_ref, e_ref, r_ref, o_ref):
    e = e_ref[...]
    start = jnp.zeros(e.shape, jnp.int32)
    for x in range(starts_ref.shape[0]):
        start = jnp.where(e == x, starts_ref[x], start)
    o_ref[...] = start + r_ref[...]


def _slot_call(pad_starts, e_t, r_t, *, tn):
    rows, n = e_t.shape
    blk = pl.BlockSpec((rows, tn), lambda i, starts: (0, i))
    return pl.pallas_call(
        _slot_kernel,
        grid_spec=pltpu.PrefetchScalarGridSpec(
            num_scalar_prefetch=1, grid=(n // tn,), in_specs=[blk, blk], out_specs=blk),
        out_shape=jax.ShapeDtypeStruct((rows, n), jnp.int32),
        compiler_params=_cparams(("arbitrary",)),
        name="slots",
    )(pad_starts, e_t, r_t)


def _padfill_kernel(ends_ref, padded_ref, xb_ref, zero_ref, sem):
    zero_ref[...] = jnp.zeros_like(zero_ref)
    n_exp = ends_ref.shape[0]
    per_row = zero_ref.shape[0] // MOE_BLOCK

    def tail_copy(e):
        start = pl.multiple_of((ends_ref[e] - MOE_BLOCK) * per_row, MOE_BLOCK)
        return pltpu.make_async_copy(zero_ref, xb_ref.at[pl.ds(start, MOE_BLOCK * per_row)], sem)

    for e in range(n_exp):
        @pl.when(padded_ref[e] > 0)
        def _():
            tail_copy(e).start()
    for e in range(n_exp):
        @pl.when(padded_ref[e] > 0)
        def _():
            tail_copy(e).wait()


def _padfill_call(pad_ends, padded, cap, d):
    return pl.pallas_call(
        _padfill_kernel,
        grid_spec=pltpu.PrefetchScalarGridSpec(
            num_scalar_prefetch=2, grid=(1,), in_specs=[],
            out_specs=pl.BlockSpec(memory_space=pl.ANY),
            scratch_shapes=[pltpu.VMEM((MOE_BLOCK * d // LANES, LANES), F32), pltpu.SemaphoreType.DMA]),
        out_shape=jax.ShapeDtypeStruct((cap * d // LANES, LANES), F32),
        compiler_params=_cparams(("arbitrary",)),
        name="padfill",
    )(pad_ends, padded)


def _row_copy(src_ref, src_row, dst_ref, dst_row, sem, per_row):
    src = pl.ds(pl.multiple_of(src_row * per_row, per_row), per_row)
    dst = pl.ds(pl.multiple_of(dst_row * per_row, per_row), per_row)
    return pltpu.make_async_copy(src_ref.at[src], dst_ref.at[dst], sem)


def _for_row_groups(n_rows, fn):
    def body(g, carry):
        base = pl.multiple_of(g * ROW_GROUP, ROW_GROUP)
        for rr in range(ROW_GROUP):
            fn(base + rr)
        return carry
    lax.fori_loop(0, n_rows // ROW_GROUP, body, 0)


DISPATCH_RING = 3


def _dispatch_kernel(dest_ref, h_ref, xin_ref, xb_ref, hbuf, load_sems, row_sems):
    del xin_ref
    tm = dest_ref.shape[2] // TOP_K
    per_row = hbuf.shape[1] // tm
    i = pl.program_id(0)
    last = pl.num_programs(0) - 1
    slot = i % DISPATCH_RING

    def load(tile, s):
        start = pl.multiple_of(tile * tm * per_row, tm * per_row)
        return pltpu.make_async_copy(h_ref.at[pl.ds(start, tm * per_row)], hbuf.at[s], load_sems.at[s])

    def drain(s):
        for _ in range(TOP_K):
            pltpu.make_async_copy(hbuf.at[s], xb_ref.at[pl.ds(0, tm * per_row)], row_sems.at[s]).wait()

    @pl.when(i == 0)
    def _():
        load(0, 0).start()

    load(i, slot).wait()

    @pl.when(i < last)
    def _():
        load(i + 1, (i + 1) % DISPATCH_RING).start()

    def issue(r):
        for k in range(TOP_K):
            _row_copy(hbuf.at[slot], r, xb_ref, dest_ref[0, 0, k * tm + r],
                      row_sems.at[slot], per_row).start(priority=k % 2)

    _for_row_groups(tm, issue)

    @pl.when(i >= 1)
    def _():
        drain((i + DISPATCH_RING - 1) % DISPATCH_RING)

    @pl.when(i == last)
    def _():
        drain(slot)


def _dispatch_call(dest_tiles, h2, xb_init, *, tm):
    n = dest_tiles.shape[0] * tm
    return pl.pallas_call(
        _dispatch_kernel,
        grid=(n // tm,),
        in_specs=[pl.BlockSpec((1, 1, tm * TOP_K), lambda i: (i, 0, 0), memory_space=pltpu.SMEM),
                  pl.BlockSpec(memory_space=pl.ANY),
                  pl.BlockSpec(memory_space=pl.ANY)],
        out_specs=pl.BlockSpec(memory_space=pl.ANY),
        out_shape=jax.ShapeDtypeStruct(xb_init.shape, F32),
        scratch_shapes=[pltpu.VMEM((DISPATCH_RING, h2.shape[0] // n * tm, LANES), F32),
                        pltpu.SemaphoreType.DMA((DISPATCH_RING,)), pltpu.SemaphoreType.DMA((DISPATCH_RING,))],
        input_output_aliases={2: 0},
        compiler_params=_cparams(("arbitrary",)),
        name="dispatch",
    )(dest_tiles, h2, xb_init)


def _expert_kernel(be_ref, nu_ref, x_ref, w1_ref, b1_ref, w2_ref, b2_ref, y_ref, w1b_ref, w2b_ref):
    i = pl.program_id(0)
    f = w2_ref.shape[1]
    active = i < nu_ref[0]

    @pl.when(active & ((i == 0) | (be_ref[i] != be_ref[jnp.maximum(i - 1, 0)])))
    def _():
        w1b_ref[...] = w1_ref[0].astype(BF16)
        w2b_ref[...] = w2_ref[0].astype(BF16)

    @pl.when(active)
    def _():
        x = _load_row_tiles(x_ref, w1b_ref.shape[0])
        z = jnp.dot(x.astype(BF16), w1b_ref[...], preferred_element_type=F32) + b1_ref[0]
        gate = jnp.minimum(z[:, :f], SWIGLU_LIMIT)
        lin = jnp.clip(z[:, f:], -SWIGLU_LIMIT, SWIGLU_LIMIT)
        act = gate * jax.nn.sigmoid(SWIGLU_ALPHA * gate) * (lin + 1.0)
        _store_row_tiles(y_ref, jnp.dot(act.astype(BF16), w2b_ref[...], preferred_element_type=F32) + b2_ref[0])


def _expert_call(block_e, n_used, xb, w1, b1, w2, b2):
    d = w1.shape[1]
    per_row = d // LANES
    n_blocks = xb.shape[0] // (MOE_BLOCK * per_row)
    f2 = w1.shape[2]
    f = w2.shape[1]

    def blk(i, be, nu):
        return jnp.minimum(i, nu[0] - 1)

    grid_spec = pltpu.PrefetchScalarGridSpec(
        num_scalar_prefetch=2,
        grid=(n_blocks,),
        in_specs=[pl.BlockSpec((MOE_BLOCK * per_row, LANES), lambda i, be, nu: (blk(i, be, nu), 0)),
                  pl.BlockSpec((1, d, f2), lambda i, be, nu: (be[blk(i, be, nu)], 0, 0)),
                  pl.BlockSpec((1, 1, f2), lambda i, be, nu: (be[blk(i, be, nu)], 0, 0)),
                  pl.BlockSpec((1, f, d), lambda i, be, nu: (be[blk(i, be, nu)], 0, 0)),
                  pl.BlockSpec((1, 1, d), lambda i, be, nu: (be[blk(i, be, nu)], 0, 0))],
        out_specs=pl.BlockSpec((MOE_BLOCK * per_row, LANES), lambda i, be, nu: (blk(i, be, nu), 0)),
        scratch_shapes=[pltpu.VMEM((d, f2), BF16), pltpu.VMEM((f, d), BF16)],
    )
    return pl.pallas_call(
        _expert_kernel,
        grid_spec=grid_spec,
        out_shape=jax.ShapeDtypeStruct(xb.shape, F32),
        compiler_params=_cparams(("arbitrary",)),
        name="experts",
    )(block_e, n_used, xb, w1, b1, w2, b2)


def _combine_kernel(dest_ref, nxt_ref, x1_ref, w_ref, g2_ref, fg_ref, y_ref, o_ref, buf0, buf1, sems):
    tm, d = x1_ref.shape
    per_row = d // LANES
    nt = pl.num_programs(1)
    i = pl.program_id(0) * nt + pl.program_id(1)
    last = pl.num_programs(0) * nt - 1
    bufs = (buf0, buf1)

    def issue(tab_ref, slot, r):
        for k in range(TOP_K):
            _row_copy(y_ref, tab_ref[0, 0, k * tm + r], bufs[slot].at[k], r,
                      sems.at[slot], per_row).start(priority=k % 2)

    def drain(slot):
        for k in range(TOP_K):
            pltpu.make_async_copy(y_ref.at[pl.ds(0, tm * per_row)], bufs[slot].at[k], sems.at[slot]).wait()

    @pl.when(i == 0)
    def _():
        _for_row_groups(tm, lambda r: issue(dest_ref, 0, r))

    for cur in (0, 1):
        @pl.when(i % 2 == cur)
        def _():
            drain(cur)

            def body(g, carry):
                base = pl.multiple_of(g * COMBINE_ROWS, COMBINE_ROWS)
                for rr in range(COMBINE_ROWS):
                    issue(nxt_ref, 1 - cur, base + rr)
                rows = pl.ds(base, COMBINE_ROWS)
                w = w_ref[rows, :]
                moe = None
                for k in range(TOP_K):
                    yk = jnp.concatenate(
                        [bufs[cur][k, pl.ds(base * per_row + j, COMBINE_ROWS, stride=per_row), :]
                         for j in range(per_row)], axis=1)
                    moe = w[:, k:k + 1] * yk if moe is None else moe + w[:, k:k + 1] * yk
                x2 = x1_ref[rows, :] + g2_ref[...] * moe
                o_ref[rows, :] = x2 * lax.rsqrt(jnp.mean(x2 * x2, axis=-1, keepdims=True) + EPS) * fg_ref[...]
                return carry

            lax.fori_loop(0, tm // COMBINE_ROWS, body, 0)

            @pl.when(i == last)
            def _():
                drain(1 - cur)


def _combine_call(dest_tiles, x1, w_lanes, g2, final_g, y, *, tm):
    bsz, t, d = x1.shape
    nt = t // tm
    tiles = bsz * nt

    def table(shift):
        return pl.BlockSpec((1, 1, tm * TOP_K), lambda b, i: (jnp.minimum(b * nt + i + shift, tiles - 1), 0, 0),
                            memory_space=pltpu.SMEM)

    return pl.pallas_call(
        _combine_kernel,
        grid=(bsz, nt),
        in_specs=[table(0), table(1),
                  pl.BlockSpec((None, tm, d), lambda b, i: (b, i, 0)),
                  pl.BlockSpec((tm, LANES), lambda b, i: (b * nt + i, 0)),
                  pl.BlockSpec((None, 1, d), lambda b, i: (b, 0, 0)),
                  pl.BlockSpec((1, d), lambda b, i: (0, 0)),
                  pl.BlockSpec(memory_space=pl.ANY)],
        out_specs=pl.BlockSpec((None, tm, d), lambda b, i: (b, i, 0)),
        out_shape=jax.ShapeDtypeStruct((bsz, t, d), F32),
        scratch_shapes=[pltpu.VMEM((TOP_K, tm * d // LANES, LANES), F32)] * 2 + [pltpu.SemaphoreType.DMA((2,))],
        compiler_params=_cparams(("arbitrary", "arbitrary")),
        name="combine",
    )(dest_tiles, dest_tiles, x1, w_lanes, g2, final_g, y)


def kernel(x, c, ctx, c_ctx, norm1_g, norm2_g, w_mod, b_mod, w_in, lb_fwd, lb_bwd, gnorm_g, w_s, b_s,
           w_branch_a, w_branch_b, w_out, w_router, b_router, w1, b1, w2, b2, final_g):
    bsz, t, d = x.shape
    assert w_in.shape[0] == 1 and lb_fwd.shape[0] == 2, "single-layer block"
    assert w_in.shape[2] == N_SPLITS * d and d % HEAD_DIM == 0
    assert t % MLP_CHUNK == 0 and ctx.shape[1] % GLA_CHUNK == 0
    n = bsz * t

    pad_rows = (-(bsz + 1)) % 8
    cc = jnp.concatenate([c, c_ctx[None, :], jnp.zeros((pad_rows, d), F32)], axis=0)
    mod = _mod_call(cc, w_mod[0], b_mod[0][None, :])
    lat = mod[:bsz].reshape(bsz, N_MOD, 1, d)
    sh1, sc1, g1, sh2, sc2, g2 = (lat[:, m] for m in range(N_MOD))
    cmod = mod[bsz].reshape(N_MOD, 1, 1, d)
    csh1, csc1 = cmod[0], cmod[1]

    w_in_bf = w_in[0].astype(BF16)
    n1 = norm1_g[0][None, :]
    q, i_, ff, fb, og, u, v, ga, gb = _proj_call(
        x, n1, sh1, sc1, w_in_bf, lb_fwd, lb_bwd,
        splits=tuple(range(N_SPLITS)), tm=min(256, t), per_batch_mod=True)
    ci, cff, cfb = _proj_call(
        ctx, n1, csh1, csc1, w_in_bf, lb_fwd, lb_bwd,
        splits=(S_I, S_FF, S_FB), tm=ctx.shape[1], per_batch_mod=False)

    ya = _gla_call(q, i_, ff, fb, og, ci, cff, cfb, gnorm_g[0][None, :])

    n_exp = w_router.shape[2]
    tm_io = min(256, t)
    exp_rows = -(-n_exp // ROW_GROUP) * ROW_GROUP
    wr_t = jnp.zeros((exp_rows, d), BF16).at[:n_exp].set(w_router[0].T.astype(BF16))
    br_t = jnp.broadcast_to(jnp.full((exp_rows,), NEG_BIG, F32).at[:n_exp].set(b_router[0])[:, None],
                            (exp_rows, tm_io))
    bs_full = jnp.broadcast_to(b_s[0][:, :, None], b_s.shape[1:] + (HEAD_DIM,))
    x1, h2, e_t, w_t, r_t, cnt = _mix_call(
        x, u, v, ga, gb, ya, w_s[0].astype(BF16), bs_full,
        w_branch_a[0].astype(BF16), w_branch_b[0].astype(BF16), w_out[0].astype(BF16),
        g1, norm2_g[0][None, :], sh2, sc2, wr_t, br_t, tm=min(MIX_BLOCK, t))

    counts = cnt[:n_exp, 0].astype(jnp.int32)
    padded = (counts + MOE_BLOCK - 1) // MOE_BLOCK * MOE_BLOCK
    pad_ends = jnp.cumsum(padded)
    pad_starts = pad_ends - padded
    n_blocks = -(-(n * TOP_K) // MOE_BLOCK) + n_exp
    cap = n_blocks * MOE_BLOCK
    dest = _slot_call(pad_starts, e_t, r_t, tn=min(8192, n))[:TOP_K]
    blk_start = jnp.arange(n_blocks, dtype=jnp.int32) * MOE_BLOCK
    block_e = jnp.minimum(jnp.sum((pad_ends[None, :] <= blk_start[:, None]).astype(jnp.int32), axis=1),
                          n_exp - 1)
    n_used = (pad_ends[-1:] // MOE_BLOCK).astype(jnp.int32)

    dest_tiles = dest.reshape(TOP_K, n // tm_io, tm_io).transpose(1, 0, 2).reshape(n // tm_io, 1, TOP_K * tm_io)
    w_l = jnp.zeros((n, LANES), F32).at[:, :TOP_K].set(w_t[:TOP_K].T)
    xb = _dispatch_call(dest_tiles, h2, _padfill_call(pad_ends, padded, cap, d), tm=tm_io)
    y = _expert_call(block_e, n_used, xb, w1[0], b1[0][:, None, :], w2[0], b2[0][:, None, :])
    return _combine_call(dest_tiles, x1, w_l, g2, final_g[None, :], y, tm=tm_io)
```

```python
import functools

import jax
import jax.numpy as jnp
from jax import lax
from jax.experimental import pallas as pl
from jax.experimental.pallas import tpu as pltpu

F32 = jnp.float32
BF16 = jnp.bfloat16

EPS = 1e-6
N_MOD = 6
S_Q, S_I, S_FF, S_FB, S_OG, S_U, S_V, S_GA, S_GB = range(9)
N_SPLITS = 9
SPLIT_DTYPE = {S_FF: F32, S_FB: F32}
HEAD_DIM = 128
GLA_CHUNK = 64
GLA_HEADS_PER_STEP = 4
MIX_BLOCK = 512
MLP_CHUNK = 128
TOP_K = 4
MOE_BLOCK = 512
ROW_GROUP = 8
COMBINE_ROWS = 64
SWIGLU_LIMIT = 7.0
SWIGLU_ALPHA = 1.702
LANES = 128
NEG_BIG = -1e30
VMEM_LIMIT = 56 * 1024 * 1024


def _cparams(sem):
    return pltpu.CompilerParams(dimension_semantics=sem, vmem_limit_bytes=VMEM_LIMIT)


def _resident(shape):
    zeros = (0,) * len(shape)
    return pl.BlockSpec(shape, lambda *_: zeros, pipeline_mode=pl.Buffered(1))


def _mod_kernel(c_ref, w_ref, b_ref, o_ref):
    c = c_ref[...]
    h = (c * jax.nn.sigmoid(c)).astype(BF16)
    o_ref[...] = jnp.dot(h, w_ref[...].astype(BF16), preferred_element_type=F32) + b_ref[...]


def _mod_call(cc, w_mod, b_mod):
    rows, d = cc.shape
    width = w_mod.shape[1]
    return pl.pallas_call(
        _mod_kernel,
        grid=(width // d,),
        in_specs=[pl.BlockSpec((rows, d), lambda j: (0, 0)),
                  pl.BlockSpec((d, d), lambda j: (0, j)),
                  pl.BlockSpec((1, d), lambda j: (0, j))],
        out_specs=pl.BlockSpec((rows, d), lambda j: (0, j)),
        out_shape=jax.ShapeDtypeStruct((rows, width), F32),
        compiler_params=_cparams(("arbitrary",)),
        name="mod",
    )(cc, w_mod, b_mod)


def _lower_bound(lb_ref):
    a = lb_ref[0:1, :]
    b = lb_ref[1:2, :]
    m = jnp.maximum(a, b)
    ea = jnp.exp(a - m)
    eb = jnp.exp(b - m)
    return ea / (ea + eb)


def _gelu(z):
    return 0.5 * z * (1.0 + lax.erf(z * (2.0 ** -0.5)))


def _split_activation(s, z, lbf_ref, lbb_ref):
    if s in (S_Q, S_OG):
        return z * jax.nn.sigmoid(z)
    if s == S_I:
        return z
    if s in (S_FF, S_FB):
        lb = _lower_bound(lbf_ref if s == S_FF else lbb_ref)
        return lb + (1.0 - lb) * jax.nn.sigmoid(z)
    if s in (S_U, S_V):
        return _gelu(z)
    return jax.nn.sigmoid(z)


def _proj_kernel(x_ref, g_ref, sh_ref, sc_ref, w_ref, lbf_ref, lbb_ref, *o_refs, splits):
    d = x_ref.shape[1]
    x = x_ref[...]
    y = x * lax.rsqrt(jnp.mean(x * x, axis=-1, keepdims=True) + EPS) * g_ref[...]
    h = (y * (1.0 + sc_ref[...]) + sh_ref[...]).astype(BF16)
    for s, o_ref in zip(splits, o_refs):
        z = jnp.dot(h, w_ref[:, s * d:(s + 1) * d], preferred_element_type=F32)
        o_ref[...] = _split_activation(s, z, lbf_ref, lbb_ref).astype(o_ref.dtype)


def _proj_call(x, g, sh, sc, w_bf, lbf, lbb, *, splits, tm, per_batch_mod):
    bsz, t, d = x.shape
    mod_map = (lambda b, i: (b, 0, 0)) if per_batch_mod else (lambda b, i: (0, 0, 0))
    row = pl.BlockSpec((None, tm, d), lambda b, i: (b, i, 0))
    return pl.pallas_call(
        functools.partial(_proj_kernel, splits=splits),
        grid=(bsz, t // tm),
        in_specs=[row,
                  pl.BlockSpec((1, d), lambda b, i: (0, 0)),
                  pl.BlockSpec((None, 1, d), mod_map),
                  pl.BlockSpec((None, 1, d), mod_map),
                  _resident(w_bf.shape),
                  pl.BlockSpec((2, d), lambda b, i: (0, 0)),
                  pl.BlockSpec((2, d), lambda b, i: (0, 0))],
        out_specs=[row for _ in splits],
        out_shape=[jax.ShapeDtypeStruct((bsz, t, d), SPLIT_DTYPE.get(s, BF16)) for s in splits],
        compiler_params=_cparams(("arbitrary", "arbitrary")),
        name="proj",
    )(x, g, sh, sc, w_bf, lbf, lbb)


def _prefix_sum(x, reverse):
    n = x.shape[0]
    row = lax.broadcasted_iota(jnp.int32, x.shape, 0)
    s = 1
    while s < n:
        if reverse:
            x = x + jnp.where(row < n - s, pltpu.roll(x, n - s, 0), 0.0)
        else:
            x = x + jnp.where(row >= s, pltpu.roll(x, s, 0), 0.0)
        s *= 2
    return x


def _dot_nt(a, b):
    return lax.dot_general(a, b, (((1,), (1,)), ((), ())), preferred_element_type=F32)


def _dot_tn(a, b):
    return lax.dot_general(a, b, (((0,), (0,)), ((), ())), preferred_element_type=F32)


def _gla_prepare(q, f, reverse):
    c = q.shape[0]
    b = _prefix_sum(jnp.log(f), reverse)
    if reverse:
        b_mid = b[c // 2:c // 2 + 1, :]
        b_end = b[0:1, :]
    else:
        b_mid = b[c // 2 - 1:c // 2, :]
        b_end = b[c - 1:c, :]
    qe = q * jnp.exp(b - b_mid)
    ke = (1.0 - f) * jnp.exp(b_mid - b)
    kd = ke * jnp.exp(b_end - b_mid)
    return qe.astype(BF16), ke.astype(BF16), kd.astype(BF16), jnp.exp(b_mid), jnp.exp(b_end)


def _gla_scores(qe, ke, st, e_mid):
    rhs = jnp.concatenate([(st * e_mid).astype(BF16), ke], axis=0)
    return _dot_nt(qe, rhs)


def _gla_finish(both, kd, vb, st, e_end, reverse):
    c = both.shape[0]
    dv = st.shape[0]
    ti = lax.broadcasted_iota(jnp.int32, (c, c), 0)
    si = lax.broadcasted_iota(jnp.int32, (c, c), 1)
    keep = (si >= ti) if reverse else (si <= ti)
    scores = jnp.where(keep, both[:, dv:], 0.0)
    o = both[:, :dv] + jnp.dot(scores.astype(BF16), vb, preferred_element_type=F32)
    return o, st * e_end + _dot_tn(vb, kd)


def _context_state(f, vb, reverse_decay):
    g = jnp.log(f)
    tail = _prefix_sum(g, reverse_decay) - g
    kt = ((1.0 - f) * jnp.exp(tail)).astype(BF16)
    return _dot_tn(vb, kt)


def _gla_kernel(q_ref, i_ref, ff_ref, fb_ref, og_ref, ci_ref, cff_ref, cfb_ref, gn_ref,
                o_ref, of_ref, ob_ref, sf_ref, sb_ref, ops_f, ops_b, rows_f, rows_b):
    t = q_ref.shape[0]
    c = GLA_CHUNK
    n = t // c
    heads = [slice(h * HEAD_DIM, (h + 1) * HEAD_DIM) for h in range(q_ref.shape[1] // HEAD_DIM)]
    for h, cols in enumerate(heads):
        ci = ci_ref[:, cols]
        sf_ref[h] = _context_state(cff_ref[:, cols], ci, True)
        sb_ref[h] = _context_state(cfb_ref[:, cols], ci, False)

    directions = ((ff_ref, ops_f, rows_f, sf_ref, of_ref, False), (fb_ref, ops_b, rows_b, sb_ref, ob_ref, True))

    def chunk_rows(ck):
        return pl.ds(pl.multiple_of(ck * c, c), c)

    def prepare(step):
        slot = step % 2
        for cols in heads:
            for f_ref, ops, rows, _, _, reverse in directions:
                r = chunk_rows((n - 1 - step) if reverse else step)
                qe, ke, kd, e_mid, e_end = _gla_prepare(q_ref[r, cols].astype(F32), f_ref[r, cols], reverse)
                ops[slot, 0, :, cols] = qe
                ops[slot, 1, :, cols] = ke
                ops[slot, 2, :, cols] = kd
                rows[slot, 0, :, cols] = jnp.broadcast_to(e_mid, (8, HEAD_DIM))
                rows[slot, 1, :, cols] = jnp.broadcast_to(e_end, (8, HEAD_DIM))

    prepare(0)

    def scan(step, carry):
        slot = step % 2
        chains = []
        for h, cols in enumerate(heads):
            for _, ops, rows, s_ref, acc, reverse in directions:
                chains.append((h, cols, ops, rows, s_ref, acc, reverse,
                               chunk_rows((n - 1 - step) if reverse else step)))
        firsts = []
        for h, cols, ops, rows, s_ref, acc, reverse, r in chains:
            st = s_ref[h]
            firsts.append((st, _gla_scores(ops[slot, 0, :, cols], ops[slot, 1, :, cols], st,
                                           rows[slot, 0, 0:1, cols])))
        for (h, cols, ops, rows, s_ref, acc, reverse, r), (st, both) in zip(chains, firsts):
            o, st = _gla_finish(both, ops[slot, 2, :, cols], i_ref[r, cols], st, rows[slot, 1, 0:1, cols],
                                reverse)
            acc[r, cols] = o
            s_ref[h] = st
        prepare(jnp.minimum(step + 1, n - 1))
        return carry

    lax.fori_loop(0, n, scan, 0)
    for h, cols in enumerate(heads):
        o = of_ref[:, cols] + ob_ref[:, cols]
        y = o * lax.rsqrt(jnp.mean(o * o, axis=-1, keepdims=True) + EPS) * gn_ref[...]
        o_ref[:, cols] = (y * og_ref[:, cols].astype(F32)).astype(o_ref.dtype)


def _gla_call(q, i, ff, fb, og, ci, cff, cfb, gnorm):
    bsz, t, d = q.shape
    width = GLA_HEADS_PER_STEP * HEAD_DIM
    tc = ci.shape[1]
    lat = pl.BlockSpec((None, t, width), lambda b, h: (b, 0, h))
    ctx = pl.BlockSpec((None, tc, width), lambda b, h: (b, 0, h))
    out_acc = pltpu.VMEM((t, width), F32)
    state = pltpu.VMEM((GLA_HEADS_PER_STEP, HEAD_DIM, HEAD_DIM), F32)
    operands = pltpu.VMEM((2, 3, GLA_CHUNK, width), BF16)
    decay_rows = pltpu.VMEM((2, 2, 8, width), F32)
    return pl.pallas_call(
        _gla_kernel,
        grid=(bsz, d // width),
        in_specs=[lat, lat, lat, lat, lat, ctx, ctx, ctx,
                  pl.BlockSpec((1, HEAD_DIM), lambda b, h: (0, 0))],
        out_specs=lat,
        out_shape=jax.ShapeDtypeStruct((bsz, t, d), BF16),
        scratch_shapes=[out_acc, out_acc, state, state, operands, operands, decay_rows, decay_rows],
        compiler_params=_cparams(("arbitrary", "arbitrary")),
        name="gla",
    )(q, i, ff, fb, og, ci, cff, cfb, gnorm)


def _store_row_tiles(ref, value):
    rows, d = value.shape
    per_row = d // LANES
    for j in range(per_row):
        ref[pl.ds(j, rows, stride=per_row), :] = value[:, j * LANES:(j + 1) * LANES]


def _load_row_tiles(ref, d):
    per_row = d // LANES
    rows = ref.shape[0] // per_row
    return jnp.concatenate([ref[pl.ds(j, rows, stride=per_row), :] for j in range(per_row)], axis=1)


def _mix_kernel(x_ref, u_ref, v_ref, ga_ref, gb_ref, ya_ref, ws_ref, bs_ref, wa_ref, wb_ref, wo_ref,
                g1_ref, n2_ref, sh2_ref, sc2_ref, wr_ref, br_ref,
                x1_ref, h2_ref, e_ref, w_ref, rank_ref, cnt_ref, yb_ref, carry_ref):
    tm, d = x_ref.shape
    sub = carry_ref.shape[1]
    per_row = d // LANES
    first = (pl.program_id(0) == 0) & (pl.program_id(1) == 0)

    @pl.when(first)
    def _():
        carry_ref[...] = jnp.zeros_like(carry_ref)

    subs = [(s0, pl.ds(s0, sub), slice(s0, s0 + sub)) for s0 in range(0, tm, sub)]

    for s0, rs, _ in subs:
        v = v_ref[rs, :].astype(F32)
        mu = jnp.mean(v, axis=-1, keepdims=True)
        vc = v - mu
        var = jnp.mean(vc * vc, axis=-1, keepdims=True)
        vn = (vc * lax.rsqrt(var + EPS)).astype(BF16)
        for ck in range(sub // MLP_CHUNK):
            rows = slice(ck * MLP_CHUNK, (ck + 1) * MLP_CHUNK)
            orow = slice(s0 + ck * MLP_CHUNK, s0 + (ck + 1) * MLP_CHUNK)
            for g in range(d // HEAD_DIM):
                cols = slice(g * HEAD_DIM, (g + 1) * HEAD_DIM)
                s = jnp.dot(ws_ref[g], vn[rows, cols], preferred_element_type=F32) + bs_ref[g]
                yb_ref[orow, cols] = (u_ref[orow, cols].astype(F32) * s).astype(BF16)

    merged = []
    for _, rs, _ in subs:
        pa = jnp.dot(ya_ref[rs, :], wa_ref[...], preferred_element_type=F32)
        pb = jnp.dot(yb_ref[rs, :], wb_ref[...], preferred_element_type=F32)
        merged.append((ga_ref[rs, :].astype(F32) * pa + gb_ref[rs, :].astype(F32) * pb).astype(BF16))

    logits = []
    for (s0, rs, _), mg in zip(subs, merged):
        mixed = jnp.dot(mg, wo_ref[...], preferred_element_type=F32)
        x1 = x_ref[rs, :] + g1_ref[...] * mixed
        x1_ref[rs, :] = x1
        y = x1 * lax.rsqrt(jnp.mean(x1 * x1, axis=-1, keepdims=True) + EPS) * n2_ref[...]
        h2 = y * (1.0 + sc2_ref[...]) + sh2_ref[...]
        _store_row_tiles(h2_ref.at[pl.ds(s0 * per_row, sub * per_row)], h2)
        logits.append(_dot_nt(wr_ref[...], h2.astype(BF16)) + br_ref[...])

    for (_, _, ls), work in zip(subs, logits):
        n_rows = work.shape[0]
        eidx = lax.broadcasted_iota(jnp.int32, (n_rows, sub), 0)
        vals, idxs = [], []
        for _ in range(TOP_K):
            m = jnp.max(work, axis=0, keepdims=True)
            idx = jnp.min(jnp.where(work == m, eidx, n_rows), axis=0, keepdims=True)
            vals.append(m)
            idxs.append(idx)
            work = jnp.where(eidx == idx, -jnp.inf, work)
        exps = [jnp.exp(m - vals[0]) for m in vals]
        denom = exps[0] + exps[1] + exps[2] + exps[3]

        sel = jnp.zeros((n_rows, sub), F32)
        for idx in idxs:
            sel = sel + jnp.where(eidx == idx, 1.0, 0.0)
        tj = lax.broadcasted_iota(jnp.int32, (sub, sub), 0)
        tt = lax.broadcasted_iota(jnp.int32, (sub, sub), 1)
        tri = jnp.where(tj < tt, 1.0, 0.0).astype(BF16)
        before = carry_ref[...] + jnp.dot(sel.astype(BF16), tri, preferred_element_type=F32)

        out_shape = (e_ref.shape[0], sub)
        krow = lax.broadcasted_iota(jnp.int32, out_shape, 0)
        e_out = jnp.zeros(out_shape, jnp.int32)
        w_out = jnp.zeros(out_shape, F32)
        r_out = jnp.zeros(out_shape, F32)
        for k in range(TOP_K):
            rk = jnp.sum(jnp.where(eidx == idxs[k], before, 0.0), axis=0, keepdims=True)
            e_out = jnp.where(krow == k, idxs[k], e_out)
            w_out = jnp.where(krow == k, exps[k] / denom, w_out)
            r_out = jnp.where(krow == k, rk, r_out)
        e_ref[:, ls] = e_out
        w_ref[:, ls] = w_out
        rank_ref[:, ls] = r_out.astype(jnp.int32)
        carry_ref[...] = carry_ref[...] + jnp.sum(sel, axis=1, keepdims=True)
    cnt_ref[...] = carry_ref[...]


def _mix_call(x, u, v, ga, gb, ya, ws_bf, bs_full, wa_bf, wb_bf, wo_bf, g1, n2, sh2, sc2, wr_t, br_t, *, tm):
    bsz, t, d = x.shape
    n = bsz * t
    nt = t // tm
    n_rows, sub = br_t.shape
    row = pl.BlockSpec((None, tm, d), lambda b, i: (b, i, 0))
    per_batch = pl.BlockSpec((None, 1, d), lambda b, i: (b, 0, 0))
    per_token = pl.BlockSpec((ROW_GROUP, tm), lambda b, i: (0, b * nt + i))

    return pl.pallas_call(
        _mix_kernel,
        grid=(bsz, nt),
        in_specs=[row, row, row, row, row, row,
                  _resident(ws_bf.shape), _resident(bs_full.shape),
                  _resident((d, d)), _resident((d, d)), _resident((d, d)),
                  per_batch, _resident((1, d)), per_batch, per_batch,
                  _resident(wr_t.shape), _resident(br_t.shape)],
        out_specs=[row, pl.BlockSpec((tm * d // LANES, LANES), lambda b, i: (b * nt + i, 0)),
                   per_token, per_token, per_token,
                   pl.BlockSpec((n_rows, sub), lambda b, i: (0, 0))],
        out_shape=[jax.ShapeDtypeStruct((bsz, t, d), F32),
                   jax.ShapeDtypeStruct((n * d // LANES, LANES), F32),
                   jax.ShapeDtypeStruct((ROW_GROUP, n), jnp.int32),
                   jax.ShapeDtypeStruct((ROW_GROUP, n), F32),
                   jax.ShapeDtypeStruct((ROW_GROUP, n), jnp.int32),
                   jax.ShapeDtypeStruct((n_rows, sub), F32)],
        scratch_shapes=[pltpu.VMEM((tm, d), BF16), pltpu.VMEM((n_rows, sub), F32)],
        compiler_params=_cparams(("arbitrary", "arbitrary")),
        name="mix",
    )(x, u, v, ga, gb, ya, ws_bf, bs_full, wa_bf, wb_bf, wo_bf, g1, n2, sh2, sc2, wr_t, br_t)


def _slot_kernel(starts_ref, e_ref, r_ref, o_ref):
    e = e_ref[...]
    start = jnp.zeros(e.shape, jnp.int32)
    for x in range(starts_ref.shape[0]):
        start = jnp.where(e == x, starts_ref[x], start)
    o_ref[...] = start + r_ref[...]


def _slot_call(pad_starts, e_t, r_t, *, tn):
    rows, n = e_t.shape
    blk = pl.BlockSpec((rows, tn), lambda i, starts: (0, i))
    return pl.pallas_call(
        _slot_kernel,
        grid_spec=pltpu.PrefetchScalarGridSpec(
            num_scalar_prefetch=1, grid=(n // tn,), in_specs=[blk, blk], out_specs=blk),
        out_shape=jax.ShapeDtypeStruct((rows, n), jnp.int32),
        compiler_params=_cparams(("arbitrary",)),
        name="slots",
    )(pad_starts, e_t, r_t)


def _padfill_kernel(ends_ref, padded_ref, xb_ref, zero_ref, sem):
    zero_ref[...] = jnp.zeros_like(zero_ref)
    n_exp = ends_ref.shape[0]
    per_row = zero_ref.shape[0] // MOE_BLOCK

    def tail_copy(e):
        start = pl.multiple_of((ends_ref[e] - MOE_BLOCK) * per_row, MOE_BLOCK)
        return pltpu.make_async_copy(zero_ref, xb_ref.at[pl.ds(start, MOE_BLOCK * per_row)], sem)

    for e in range(n_exp):
        @pl.when(padded_ref[e] > 0)
        def _():
            tail_copy(e).start()
    for e in range(n_exp):
        @pl.when(padded_ref[e] > 0)
        def _():
            tail_copy(e).wait()


def _padfill_call(pad_ends, padded, cap, d):
    return pl.pallas_call(
        _padfill_kernel,
        grid_spec=pltpu.PrefetchScalarGridSpec(
            num_scalar_prefetch=2, grid=(1,), in_specs=[],
            out_specs=pl.BlockSpec(memory_space=pl.ANY),
            scratch_shapes=[pltpu.VMEM((MOE_BLOCK * d // LANES, LANES), F32), pltpu.SemaphoreType.DMA]),
        out_shape=jax.ShapeDtypeStruct((cap * d // LANES, LANES), F32),
        compiler_params=_cparams(("arbitrary",)),
        name="padfill",
    )(pad_ends, padded)


def _row_copy(src_ref, src_row, dst_ref, dst_row, sem, per_row):
    src = pl.ds(pl.multiple_of(src_row * per_row, per_row), per_row)
    dst = pl.ds(pl.multiple_of(dst_row * per_row, per_row), per_row)
    return pltpu.make_async_copy(src_ref.at[src], dst_ref.at[dst], sem)


def _for_row_groups(n_rows, fn):
    def body(g, carry):
        base = pl.multiple_of(g * ROW_GROUP, ROW_GROUP)
        for rr in range(ROW_GROUP):
            fn(base + rr)
        return carry
    lax.fori_loop(0, n_rows // ROW_GROUP, body, 0)


DISPATCH_RING = 3


def _dispatch_kernel(dest_ref, h_ref, xin_ref, xb_ref, hbuf, load_sems, row_sems):
    del xin_ref
    tm = dest_ref.shape[2] // TOP_K
    per_row = hbuf.shape[1] // tm
    i = pl.program_id(0)
    last = pl.num_programs(0) - 1
    slot = i % DISPATCH_RING

    def load(tile, s):
        start = pl.multiple_of(tile * tm * per_row, tm * per_row)
        return pltpu.make_async_copy(h_ref.at[pl.ds(start, tm * per_row)], hbuf.at[s], load_sems.at[s])

    def drain(s):
        for _ in range(TOP_K):
            pltpu.make_async_copy(hbuf.at[s], xb_ref.at[pl.ds(0, tm * per_row)], row_sems.at[s]).wait()

    @pl.when(i == 0)
    def _():
        load(0, 0).start()

    load(i, slot).wait()

    @pl.when(i < last)
    def _():
        load(i + 1, (i + 1) % DISPATCH_RING).start()

    def issue(r):
        for k in range(TOP_K):
            _row_copy(hbuf.at[slot], r, xb_ref, dest_ref[0, 0, k * tm + r],
                      row_sems.at[slot], per_row).start(priority=k % 2)

    _for_row_groups(tm, issue)

    @pl.when(i >= 1)
    def _():
        drain((i + DISPATCH_RING - 1) % DISPATCH_RING)

    @pl.when(i == last)
    def _():
        drain(slot)


def _dispatch_call(dest_tiles, h2, xb_init, *, tm):
    n = dest_tiles.shape[0] * tm
    return pl.pallas_call(
        _dispatch_kernel,
        grid=(n // tm,),
        in_specs=[pl.BlockSpec((1, 1, tm * TOP_K), lambda i: (i, 0, 0), memory_space=pltpu.SMEM),
                  pl.BlockSpec(memory_space=pl.ANY),
                  pl.BlockSpec(memory_space=pl.ANY)],
        out_specs=pl.BlockSpec(memory_space=pl.ANY),
        out_shape=jax.ShapeDtypeStruct(xb_init.shape, F32),
        scratch_shapes=[pltpu.VMEM((DISPATCH_RING, h2.shape[0] // n * tm, LANES), F32),
                        pltpu.SemaphoreType.DMA((DISPATCH_RING,)), pltpu.SemaphoreType.DMA((DISPATCH_RING,))],
        input_output_aliases={2: 0},
        compiler_params=_cparams(("arbitrary",)),
        name="dispatch",
    )(dest_tiles, h2, xb_init)


def _expert_kernel(be_ref, nu_ref, x_ref, w1_ref, b1_ref, w2_ref, b2_ref, y_ref, w1b_ref, w2b_ref):
    i = pl.program_id(0)
    f = w2_ref.shape[1]
    active = i < nu_ref[0]

    @pl.when(active & ((i == 0) | (be_ref[i] != be_ref[jnp.maximum(i - 1, 0)])))
    def _():
        w1b_ref[...] = w1_ref[0].astype(BF16)
        w2b_ref[...] = w2_ref[0].astype(BF16)

    @pl.when(active)
    def _():
        x = _load_row_tiles(x_ref, w1b_ref.shape[0])
        z = jnp.dot(x.astype(BF16), w1b_ref[...], preferred_element_type=F32) + b1_ref[0]
        gate = jnp.minimum(z[:, :f], SWIGLU_LIMIT)
        lin = jnp.clip(z[:, f:], -SWIGLU_LIMIT, SWIGLU_LIMIT)
        act = gate * jax.nn.sigmoid(SWIGLU_ALPHA * gate) * (lin + 1.0)
        _store_row_tiles(y_ref, jnp.dot(act.astype(BF16), w2b_ref[...], preferred_element_type=F32) + b2_ref[0])


def _expert_call(block_e, n_used, xb, w1, b1, w2, b2):
    d = w1.shape[1]
    per_row = d // LANES
    n_blocks = xb.shape[0] // (MOE_BLOCK * per_row)
    f2 = w1.shape[2]
    f = w2.shape[1]

    def blk(i, be, nu):
        return jnp.minimum(i, nu[0] - 1)

    grid_spec = pltpu.PrefetchScalarGridSpec(
        num_scalar_prefetch=2,
        grid=(n_blocks,),
        in_specs=[pl.BlockSpec((MOE_BLOCK * per_row, LANES), lambda i, be, nu: (blk(i, be, nu), 0)),
                  pl.BlockSpec((1, d, f2), lambda i, be, nu: (be[blk(i, be, nu)], 0, 0)),
                  pl.BlockSpec((1, 1, f2), lambda i, be, nu: (be[blk(i, be, nu)], 0, 0)),
                  pl.BlockSpec((1, f, d), lambda i, be, nu: (be[blk(i, be, nu)], 0, 0)),
                  pl.BlockSpec((1, 1, d), lambda i, be, nu: (be[blk(i, be, nu)], 0, 0))],
        out_specs=pl.BlockSpec((MOE_BLOCK * per_row, LANES), lambda i, be, nu: (blk(i, be, nu), 0)),
        scratch_shapes=[pltpu.VMEM((d, f2), BF16), pltpu.VMEM((f, d), BF16)],
    )
    return pl.pallas_call(
        _expert_kernel,
        grid_spec=grid_spec,
        out_shape=jax.ShapeDtypeStruct(xb.shape, F32),
        compiler_params=_cparams(("arbitrary",)),
        name="experts",
    )(block_e, n_used, xb, w1, b1, w2, b2)


COMBINE_RING = 3


def _combine_kernel(dest_ref, nxt_ref, far_ref, x1_ref, w_ref, g2_ref, fg_ref, y_ref, o_ref,
                    buf0, buf1, buf2, sems):
    tm, d = x1_ref.shape
    per_row = d // LANES
    nt = pl.num_programs(1)
    i = pl.program_id(0) * nt + pl.program_id(1)
    last = pl.num_programs(0) * nt - 1
    bufs = (buf0, buf1, buf2)

    def issue(tab_ref, slot, r):
        for k in range(TOP_K):
            _row_copy(y_ref, tab_ref[0, 0, k * tm + r], bufs[slot].at[k], r,
                      sems.at[slot], per_row).start(priority=k % 2)

    def drain(slot):
        for k in range(TOP_K):
            pltpu.make_async_copy(y_ref.at[pl.ds(0, tm * per_row)], bufs[slot].at[k], sems.at[slot]).wait()

    @pl.when(i == 0)
    def _():
        _for_row_groups(tm, lambda r: issue(dest_ref, 0, r))
        _for_row_groups(tm, lambda r: issue(nxt_ref, 1, r))

    for cur in range(COMBINE_RING):
        far = (cur + 2) % COMBINE_RING

        @pl.when(i % COMBINE_RING == cur)
        def _():
            drain(cur)

            def body(g, carry):
                base = pl.multiple_of(g * COMBINE_ROWS, COMBINE_ROWS)
                for rr in range(COMBINE_ROWS):
                    issue(far_ref, far, base + rr)
                rows = pl.ds(base, COMBINE_ROWS)
                w = w_ref[rows, :]
                moe = None
                for k in range(TOP_K):
                    yk = jnp.concatenate(
                        [bufs[cur][k, pl.ds(base * per_row + j, COMBINE_ROWS, stride=per_row), :]
                         for j in range(per_row)], axis=1)
                    moe = w[:, k:k + 1] * yk if moe is None else moe + w[:, k:k + 1] * yk
                x2 = x1_ref[rows, :] + g2_ref[...] * moe
                o_ref[rows, :] = x2 * lax.rsqrt(jnp.mean(x2 * x2, axis=-1, keepdims=True) + EPS) * fg_ref[...]
                return carry

            lax.fori_loop(0, tm // COMBINE_ROWS, body, 0)

            @pl.when(i == last)
            def _():
                drain((cur + 1) % COMBINE_RING)
                drain(far)


def _combine_call(dest_tiles, x1, w_lanes, g2, final_g, y, *, tm):
    bsz, t, d = x1.shape
    nt = t // tm
    tiles = bsz * nt

    def table(shift):
        return pl.BlockSpec((1, 1, tm * TOP_K), lambda b, i: (jnp.minimum(b * nt + i + shift, tiles - 1), 0, 0),
                            memory_space=pltpu.SMEM)

    return pl.pallas_call(
        _combine_kernel,
        grid=(bsz, nt),
        in_specs=[table(0), table(1), table(2),
                  pl.BlockSpec((None, tm, d), lambda b, i: (b, i, 0)),
                  pl.BlockSpec((tm, LANES), lambda b, i: (b * nt + i, 0)),
                  pl.BlockSpec((None, 1, d), lambda b, i: (b, 0, 0)),
                  pl.BlockSpec((1, d), lambda b, i: (0, 0)),
                  pl.BlockSpec(memory_space=pl.ANY)],
        out_specs=pl.BlockSpec((None, tm, d), lambda b, i: (b, i, 0)),
        out_shape=jax.ShapeDtypeStruct((bsz, t, d), F32),
        scratch_shapes=[pltpu.VMEM((TOP_K, tm * d // LANES, LANES), F32)] * COMBINE_RING
        + [pltpu.SemaphoreType.DMA((COMBINE_RING,))],
        compiler_params=_cparams(("arbitrary", "arbitrary")),
        name="combine",
    )(dest_tiles, dest_tiles, dest_tiles, x1, w_lanes, g2, final_g, y)


def kernel(x, c, ctx, c_ctx, norm1_g, norm2_g, w_mod, b_mod, w_in, lb_fwd, lb_bwd, gnorm_g, w_s, b_s,
           w_branch_a, w_branch_b, w_out, w_router, b_router, w1, b1, w2, b2, final_g):
    bsz, t, d = x.shape
    assert w_in.shape[0] == 1 and lb_fwd.shape[0] == 2, "single-layer block"
    assert w_in.shape[2] == N_SPLITS * d and d % HEAD_DIM == 0
    assert t % MLP_CHUNK == 0 and ctx.shape[1] % GLA_CHUNK == 0
    n = bsz * t

    pad_rows = (-(bsz + 1)) % 8
    cc = jnp.concatenate([c, c_ctx[None, :], jnp.zeros((pad_rows, d), F32)], axis=0)
    mod = _mod_call(cc, w_mod[0], b_mod[0][None, :])
    lat = mod[:bsz].reshape(bsz, N_MOD, 1, d)
    sh1, sc1, g1, sh2, sc2, g2 = (lat[:, m] for m in range(N_MOD))
    cmod = mod[bsz].reshape(N_MOD, 1, 1, d)
    csh1, csc1 = cmod[0], cmod[1]

    w_in_bf = w_in[0].astype(BF16)
    n1 = norm1_g[0][None, :]
    q, i_, ff, fb, og, u, v, ga, gb = _proj_call(
        x, n1, sh1, sc1, w_in_bf, lb_fwd, lb_bwd,
        splits=tuple(range(N_SPLITS)), tm=min(256, t), per_batch_mod=True)
    ci, cff, cfb = _proj_call(
        ctx, n1, csh1, csc1, w_in_bf, lb_fwd, lb_bwd,
        splits=(S_I, S_FF, S_FB), tm=ctx.shape[1], per_batch_mod=False)

    ya = _gla_call(q, i_, ff, fb, og, ci, cff, cfb, gnorm_g[0][None, :])

    n_exp = w_router.shape[2]
    tm_io = min(256, t)
    exp_rows = -(-n_exp // ROW_GROUP) * ROW_GROUP
    wr_t = jnp.zeros((exp_rows, d), BF16).at[:n_exp].set(w_router[0].T.astype(BF16))
    br_t = jnp.broadcast_to(jnp.full((exp_rows,), NEG_BIG, F32).at[:n_exp].set(b_router[0])[:, None],
                            (exp_rows, tm_io))
    bs_full = jnp.broadcast_to(b_s[0][:, :, None], b_s.shape[1:] + (HEAD_DIM,))
    x1, h2, e_t, w_t, r_t, cnt = _mix_call(
        x, u, v, ga, gb, ya, w_s[0].astype(BF16), bs_full,
        w_branch_a[0].astype(BF16), w_branch_b[0].astype(BF16), w_out[0].astype(BF16),
        g1, norm2_g[0][None, :], sh2, sc2, wr_t, br_t, tm=min(MIX_BLOCK, t))

    counts = cnt[:n_exp, 0].astype(jnp.int32)
    padded = (counts + MOE_BLOCK - 1) // MOE_BLOCK * MOE_BLOCK
    pad_ends = jnp.cumsum(padded)
    pad_starts = pad_ends - padded
    n_blocks = -(-(n * TOP_K) // MOE_BLOCK) + n_exp
    cap = n_blocks * MOE_BLOCK
    dest = _slot_call(pad_starts, e_t, r_t, tn=min(8192, n))[:TOP_K]
    blk_start = jnp.arange(n_blocks, dtype=jnp.int32) * MOE_BLOCK
    block_e = jnp.minimum(jnp.sum((pad_ends[None, :] <= blk_start[:, None]).astype(jnp.int32), axis=1),
                          n_exp - 1)
    n_used = (pad_ends[-1:] // MOE_BLOCK).astype(jnp.int32)

    dest_tiles = dest.reshape(TOP_K, n // tm_io, tm_io).transpose(1, 0, 2).reshape(n // tm_io, 1, TOP_K * tm_io)
    w_l = jnp.zeros((n, LANES), F32).at[:, :TOP_K].set(w_t[:TOP_K].T)
    xb = _dispatch_call(dest_tiles, h2, _padfill_call(pad_ends, padded, cap, d), tm=tm_io)
    y = _expert_call(block_e, n_used, xb, w1[0], b1[0][:, None, :], w2[0], b2[0][:, None, :])
    return _combine_call(dest_tiles, x1, w_l, g2, final_g[None, :], y, tm=tm_io)
```

```python
import functools

import jax
import jax.numpy as jnp
from jax import lax
from jax.experimental import pallas as pl
from jax.experimental.pallas import tpu as pltpu

F32 = jnp.float32
BF16 = jnp.bfloat16

EPS = 1e-6
N_MOD = 6
S_Q, S_I, S_FF, S_FB, S_OG, S_U, S_V, S_GA, S_GB = range(9)
N_SPLITS = 9
SPLIT_DTYPE = {S_FF: F32, S_FB: F32}
HEAD_DIM = 128
GLA_CHUNK = 64
GLA_HEADS_PER_STEP = 4
MIX_BLOCK = 512
MIX_SUB = 256
IO_TILE = 512
MLP_CHUNK = 128
TOP_K = 4
MOE_BLOCK = 512
ROW_GROUP = 8
COMBINE_ROWS = 64
SWIGLU_LIMIT = 7.0
SWIGLU_ALPHA = 1.702
LANES = 128
NEG_BIG = -1e30
VMEM_LIMIT = 56 * 1024 * 1024


def _cparams(sem):
    return pltpu.CompilerParams(dimension_semantics=sem, vmem_limit_bytes=VMEM_LIMIT)


def _resident(shape):
    zeros = (0,) * len(shape)
    return pl.BlockSpec(shape, lambda *_: zeros, pipeline_mode=pl.Buffered(1))


def _mod_kernel(c_ref, w_ref, b_ref, o_ref):
    c = c_ref[...]
    h = (c * jax.nn.sigmoid(c)).astype(BF16)
    o_ref[...] = jnp.dot(h, w_ref[...].astype(BF16), preferred_element_type=F32) + b_ref[...]


def _mod_call(cc, w_mod, b_mod):
    rows, d = cc.shape
    width = w_mod.shape[1]
    return pl.pallas_call(
        _mod_kernel,
        grid=(width // d,),
        in_specs=[pl.BlockSpec((rows, d), lambda j: (0, 0)),
                  pl.BlockSpec((d, d), lambda j: (0, j)),
                  pl.BlockSpec((1, d), lambda j: (0, j))],
        out_specs=pl.BlockSpec((rows, d), lambda j: (0, j)),
        out_shape=jax.ShapeDtypeStruct((rows, width), F32),
        compiler_params=_cparams(("arbitrary",)),
        name="mod",
    )(cc, w_mod, b_mod)


def _lower_bound(lb_ref):
    a = lb_ref[0:1, :]
    b = lb_ref[1:2, :]
    m = jnp.maximum(a, b)
    ea = jnp.exp(a - m)
    eb = jnp.exp(b - m)
    return ea / (ea + eb)


def _gelu(z):
    return 0.5 * z * (1.0 + lax.erf(z * (2.0 ** -0.5)))


def _split_activation(s, z, lbf_ref, lbb_ref):
    if s in (S_Q, S_OG):
        return z * jax.nn.sigmoid(z)
    if s == S_I:
        return z
    if s in (S_FF, S_FB):
        lb = _lower_bound(lbf_ref if s == S_FF else lbb_ref)
        return lb + (1.0 - lb) * jax.nn.sigmoid(z)
    if s in (S_U, S_V):
        return _gelu(z)
    return jax.nn.sigmoid(z)


def _proj_kernel(x_ref, g_ref, sh_ref, sc_ref, w_ref, lbf_ref, lbb_ref, *o_refs, splits):
    d = x_ref.shape[1]
    x = x_ref[...]
    y = x * lax.rsqrt(jnp.mean(x * x, axis=-1, keepdims=True) + EPS) * g_ref[...]
    h = (y * (1.0 + sc_ref[...]) + sh_ref[...]).astype(BF16)
    for s, o_ref in zip(splits, o_refs):
        z = jnp.dot(h, w_ref[:, s * d:(s + 1) * d], preferred_element_type=F32)
        o_ref[...] = _split_activation(s, z, lbf_ref, lbb_ref).astype(o_ref.dtype)


def _proj_call(x, g, sh, sc, w_bf, lbf, lbb, *, splits, tm, per_batch_mod):
    bsz, t, d = x.shape
    mod_map = (lambda b, i: (b, 0, 0)) if per_batch_mod else (lambda b, i: (0, 0, 0))
    row = pl.BlockSpec((None, tm, d), lambda b, i: (b, i, 0))
    return pl.pallas_call(
        functools.partial(_proj_kernel, splits=splits),
        grid=(bsz, t // tm),
        in_specs=[row,
                  pl.BlockSpec((1, d), lambda b, i: (0, 0)),
                  pl.BlockSpec((None, 1, d), mod_map),
                  pl.BlockSpec((None, 1, d), mod_map),
                  _resident(w_bf.shape),
                  pl.BlockSpec((2, d), lambda b, i: (0, 0)),
                  pl.BlockSpec((2, d), lambda b, i: (0, 0))],
        out_specs=[row for _ in splits],
        out_shape=[jax.ShapeDtypeStruct((bsz, t, d), SPLIT_DTYPE.get(s, BF16)) for s in splits],
        compiler_params=_cparams(("arbitrary", "arbitrary")),
        name="proj",
    )(x, g, sh, sc, w_bf, lbf, lbb)


def _prefix_sum(x, reverse):
    n = x.shape[0]
    row = lax.broadcasted_iota(jnp.int32, x.shape, 0)
    s = 1
    while s < n:
        if reverse:
            x = x + jnp.where(row < n - s, pltpu.roll(x, n - s, 0), 0.0)
        else:
            x = x + jnp.where(row >= s, pltpu.roll(x, s, 0), 0.0)
        s *= 2
    return x


def _dot_nt(a, b):
    return lax.dot_general(a, b, (((1,), (1,)), ((), ())), preferred_element_type=F32)


def _dot_tn(a, b):
    return lax.dot_general(a, b, (((0,), (0,)), ((), ())), preferred_element_type=F32)


def _gla_prepare(q, f, reverse):
    c = q.shape[0]
    b = _prefix_sum(jnp.log(f), reverse)
    if reverse:
        b_mid = b[c // 2:c // 2 + 1, :]
        b_end = b[0:1, :]
    else:
        b_mid = b[c // 2 - 1:c // 2, :]
        b_end = b[c - 1:c, :]
    qe = q * jnp.exp(b - b_mid)
    ke = (1.0 - f) * jnp.exp(b_mid - b)
    kd = ke * jnp.exp(b_end - b_mid)
    return qe.astype(BF16), ke.astype(BF16), kd.astype(BF16), jnp.exp(b_mid), jnp.exp(b_end)


def _gla_scores(qe, ke, st, e_mid):
    rhs = jnp.concatenate([(st * e_mid).astype(BF16), ke], axis=0)
    return _dot_nt(qe, rhs)


def _gla_finish(both, kd, vb, st, e_end, reverse):
    c = both.shape[0]
    dv = st.shape[0]
    ti = lax.broadcasted_iota(jnp.int32, (c, c), 0)
    si = lax.broadcasted_iota(jnp.int32, (c, c), 1)
    keep = (si >= ti) if reverse else (si <= ti)
    scores = jnp.where(keep, both[:, dv:], 0.0)
    o = both[:, :dv] + jnp.dot(scores.astype(BF16), vb, preferred_element_type=F32)
    return o, st * e_end + _dot_tn(vb, kd)


def _context_state(f, vb, reverse_decay):
    g = jnp.log(f)
    tail = _prefix_sum(g, reverse_decay) - g
    kt = ((1.0 - f) * jnp.exp(tail)).astype(BF16)
    return _dot_tn(vb, kt)


def _gla_kernel(q_ref, i_ref, ff_ref, fb_ref, og_ref, ci_ref, cff_ref, cfb_ref, gn_ref,
                o_ref, of_ref, ob_ref, sf_ref, sb_ref, ops_f, ops_b, rows_f, rows_b):
    t = q_ref.shape[0]
    c = GLA_CHUNK
    n = t // c
    heads = [slice(h * HEAD_DIM, (h + 1) * HEAD_DIM) for h in range(q_ref.shape[1] // HEAD_DIM)]
    for h, cols in enumerate(heads):
        ci = ci_ref[:, cols]
        sf_ref[h] = _context_state(cff_ref[:, cols], ci, True)
        sb_ref[h] = _context_state(cfb_ref[:, cols], ci, False)

    directions = ((ff_ref, ops_f, rows_f, sf_ref, of_ref, False), (fb_ref, ops_b, rows_b, sb_ref, ob_ref, True))

    def chunk_rows(ck):
        return pl.ds(pl.multiple_of(ck * c, c), c)

    def prepare(step):
        slot = step % 2
        for cols in heads:
            for f_ref, ops, rows, _, _, reverse in directions:
                r = chunk_rows((n - 1 - step) if reverse else step)
                qe, ke, kd, e_mid, e_end = _gla_prepare(q_ref[r, cols].astype(F32), f_ref[r, cols], reverse)
                ops[slot, 0, :, cols] = qe
                ops[slot, 1, :, cols] = ke
                ops[slot, 2, :, cols] = kd
                rows[slot, 0, :, cols] = jnp.broadcast_to(e_mid, (8, HEAD_DIM))
                rows[slot, 1, :, cols] = jnp.broadcast_to(e_end, (8, HEAD_DIM))

    prepare(0)

    def scan(step, carry):
        slot = step % 2
        chains = []
        for h, cols in enumerate(heads):
            for _, ops, rows, s_ref, acc, reverse in directions:
                chains.append((h, cols, ops, rows, s_ref, acc, reverse,
                               chunk_rows((n - 1 - step) if reverse else step)))
        firsts = []
        for h, cols, ops, rows, s_ref, acc, reverse, r in chains:
            st = s_ref[h]
            firsts.append((st, _gla_scores(ops[slot, 0, :, cols], ops[slot, 1, :, cols], st,
                                           rows[slot, 0, 0:1, cols])))
        for (h, cols, ops, rows, s_ref, acc, reverse, r), (st, both) in zip(chains, firsts):
            o, st = _gla_finish(both, ops[slot, 2, :, cols], i_ref[r, cols], st, rows[slot, 1, 0:1, cols],
                                reverse)
            acc[r, cols] = o
            s_ref[h] = st
        prepare(jnp.minimum(step + 1, n - 1))
        return carry

    lax.fori_loop(0, n, scan, 0)
    for h, cols in enumerate(heads):
        o = of_ref[:, cols] + ob_ref[:, cols]
        y = o * lax.rsqrt(jnp.mean(o * o, axis=-1, keepdims=True) + EPS) * gn_ref[...]
        o_ref[:, cols] = (y * og_ref[:, cols].astype(F32)).astype(o_ref.dtype)


def _gla_call(q, i, ff, fb, og, ci, cff, cfb, gnorm):
    bsz, t, d = q.shape
    width = GLA_HEADS_PER_STEP * HEAD_DIM
    tc = ci.shape[1]
    lat = pl.BlockSpec((None, t, width), lambda b, h: (b, 0, h))
    ctx = pl.BlockSpec((None, tc, width), lambda b, h: (b, 0, h))
    out_acc = pltpu.VMEM((t, width), F32)
    state = pltpu.VMEM((GLA_HEADS_PER_STEP, HEAD_DIM, HEAD_DIM), F32)
    operands = pltpu.VMEM((2, 3, GLA_CHUNK, width), BF16)
    decay_rows = pltpu.VMEM((2, 2, 8, width), F32)
    return pl.pallas_call(
        _gla_kernel,
        grid=(bsz, d // width),
        in_specs=[lat, lat, lat, lat, lat, ctx, ctx, ctx,
                  pl.BlockSpec((1, HEAD_DIM), lambda b, h: (0, 0))],
        out_specs=lat,
        out_shape=jax.ShapeDtypeStruct((bsz, t, d), BF16),
        scratch_shapes=[out_acc, out_acc, state, state, operands, operands, decay_rows, decay_rows],
        compiler_params=_cparams(("arbitrary", "arbitrary")),
        name="gla",
    )(q, i, ff, fb, og, ci, cff, cfb, gnorm)


def _store_row_tiles(ref, value):
    rows, d = value.shape
    per_row = d // LANES
    for j in range(per_row):
        ref[pl.ds(j, rows, stride=per_row), :] = value[:, j * LANES:(j + 1) * LANES]


def _load_row_tiles(ref, d):
    per_row = d // LANES
    rows = ref.shape[0] // per_row
    return jnp.concatenate([ref[pl.ds(j, rows, stride=per_row), :] for j in range(per_row)], axis=1)


def _mix_kernel(x_ref, u_ref, v_ref, ga_ref, gb_ref, ya_ref, ws_ref, bs_ref, wa_ref, wb_ref, wo_ref,
                g1_ref, n2_ref, sh2_ref, sc2_ref, wr_ref, br_ref,
                x1_ref, h2_ref, e_ref, w_ref, rank_ref, cnt_ref, yb_ref, carry_ref):
    tm, d = x_ref.shape
    sub = carry_ref.shape[1]
    per_row = d // LANES
    first = (pl.program_id(0) == 0) & (pl.program_id(1) == 0)

    @pl.when(first)
    def _():
        carry_ref[...] = jnp.zeros_like(carry_ref)

    subs = [(s0, pl.ds(s0, sub), slice(s0, s0 + sub)) for s0 in range(0, tm, sub)]

    for s0, rs, _ in subs:
        v = v_ref[rs, :].astype(F32)
        mu = jnp.mean(v, axis=-1, keepdims=True)
        vc = v - mu
        var = jnp.mean(vc * vc, axis=-1, keepdims=True)
        vn = (vc * lax.rsqrt(var + EPS)).astype(BF16)
        for ck in range(sub // MLP_CHUNK):
            rows = slice(ck * MLP_CHUNK, (ck + 1) * MLP_CHUNK)
            orow = slice(s0 + ck * MLP_CHUNK, s0 + (ck + 1) * MLP_CHUNK)
            for g in range(d // HEAD_DIM):
                cols = slice(g * HEAD_DIM, (g + 1) * HEAD_DIM)
                s = jnp.dot(ws_ref[g], vn[rows, cols], preferred_element_type=F32) + bs_ref[g]
                yb_ref[orow, cols] = (u_ref[orow, cols].astype(F32) * s).astype(BF16)

    merged = []
    for _, rs, _ in subs:
        pa = jnp.dot(ya_ref[rs, :], wa_ref[...], preferred_element_type=F32)
        pb = jnp.dot(yb_ref[rs, :], wb_ref[...], preferred_element_type=F32)
        merged.append((ga_ref[rs, :].astype(F32) * pa + gb_ref[rs, :].astype(F32) * pb).astype(BF16))

    logits = []
    for (s0, rs, _), mg in zip(subs, merged):
        mixed = jnp.dot(mg, wo_ref[...], preferred_element_type=F32)
        x1 = x_ref[rs, :] + g1_ref[...] * mixed
        x1_ref[rs, :] = x1
        y = x1 * lax.rsqrt(jnp.mean(x1 * x1, axis=-1, keepdims=True) + EPS) * n2_ref[...]
        h2 = y * (1.0 + sc2_ref[...]) + sh2_ref[...]
        _store_row_tiles(h2_ref.at[pl.ds(s0 * per_row, sub * per_row)], h2)
        logits.append(_dot_nt(wr_ref[...], h2.astype(BF16)) + br_ref[...])

    for (_, _, ls), work in zip(subs, logits):
        n_rows = work.shape[0]
        eidx = lax.broadcasted_iota(jnp.int32, (n_rows, sub), 0)
        vals, idxs = [], []
        for _ in range(TOP_K):
            m = jnp.max(work, axis=0, keepdims=True)
            idx = jnp.min(jnp.where(work == m, eidx, n_rows), axis=0, keepdims=True)
            vals.append(m)
            idxs.append(idx)
            work = jnp.where(eidx == idx, -jnp.inf, work)
        exps = [jnp.exp(m - vals[0]) for m in vals]
        denom = exps[0] + exps[1] + exps[2] + exps[3]

        sel = jnp.zeros((n_rows, sub), F32)
        for idx in idxs:
            sel = sel + jnp.where(eidx == idx, 1.0, 0.0)
        tj = lax.broadcasted_iota(jnp.int32, (sub, sub), 0)
        tt = lax.broadcasted_iota(jnp.int32, (sub, sub), 1)
        tri = jnp.where(tj < tt, 1.0, 0.0).astype(BF16)
        before = carry_ref[...] + jnp.dot(sel.astype(BF16), tri, preferred_element_type=F32)

        out_shape = (e_ref.shape[0], sub)
        krow = lax.broadcasted_iota(jnp.int32, out_shape, 0)
        e_out = jnp.zeros(out_shape, jnp.int32)
        w_out = jnp.zeros(out_shape, F32)
        r_out = jnp.zeros(out_shape, F32)
        for k in range(TOP_K):
            rk = jnp.sum(jnp.where(eidx == idxs[k], before, 0.0), axis=0, keepdims=True)
            e_out = jnp.where(krow == k, idxs[k], e_out)
            w_out = jnp.where(krow == k, exps[k] / denom, w_out)
            r_out = jnp.where(krow == k, rk, r_out)
        e_ref[:, ls] = e_out
        w_ref[:, ls] = w_out
        rank_ref[:, ls] = r_out.astype(jnp.int32)
        carry_ref[...] = carry_ref[...] + jnp.sum(sel, axis=1, keepdims=True)
    cnt_ref[...] = carry_ref[...]


def _mix_call(x, u, v, ga, gb, ya, ws_bf, bs_full, wa_bf, wb_bf, wo_bf, g1, n2, sh2, sc2, wr_t, br_t, *, tm):
    bsz, t, d = x.shape
    n = bsz * t
    nt = t // tm
    n_rows, sub = br_t.shape
    row = pl.BlockSpec((None, tm, d), lambda b, i: (b, i, 0))
    per_batch = pl.BlockSpec((None, 1, d), lambda b, i: (b, 0, 0))
    per_token = pl.BlockSpec((ROW_GROUP, tm), lambda b, i: (0, b * nt + i))

    return pl.pallas_call(
        _mix_kernel,
        grid=(bsz, nt),
        in_specs=[row, row, row, row, row, row,
                  _resident(ws_bf.shape), _resident(bs_full.shape),
                  _resident((d, d)), _resident((d, d)), _resident((d, d)),
                  per_batch, _resident((1, d)), per_batch, per_batch,
                  _resident(wr_t.shape), _resident(br_t.shape)],
        out_specs=[row, pl.BlockSpec((tm * d // LANES, LANES), lambda b, i: (b * nt + i, 0)),
                   per_token, per_token, per_token,
                   pl.BlockSpec((n_rows, sub), lambda b, i: (0, 0))],
        out_shape=[jax.ShapeDtypeStruct((bsz, t, d), F32),
                   jax.ShapeDtypeStruct((n * d // LANES, LANES), F32),
                   jax.ShapeDtypeStruct((ROW_GROUP, n), jnp.int32),
                   jax.ShapeDtypeStruct((ROW_GROUP, n), F32),
                   jax.ShapeDtypeStruct((ROW_GROUP, n), jnp.int32),
                   jax.ShapeDtypeStruct((n_rows, sub), F32)],
        scratch_shapes=[pltpu.VMEM((tm, d), BF16), pltpu.VMEM((n_rows, sub), F32)],
        compiler_params=_cparams(("arbitrary", "arbitrary")),
        name="mix",
    )(x, u, v, ga, gb, ya, ws_bf, bs_full, wa_bf, wb_bf, wo_bf, g1, n2, sh2, sc2, wr_t, br_t)


def _slot_kernel(starts_ref, e_ref, r_ref, o_ref):
    e = e_ref[...]
    start = jnp.zeros(e.shape, jnp.int32)
    for x in range(starts_ref.shape[0]):
        start = jnp.where(e == x, starts_ref[x], start)
    o_ref[...] = start + r_ref[...]


def _slot_call(pad_starts, e_t, r_t, *, tn):
    rows, n = e_t.shape
    blk = pl.BlockSpec((rows, tn), lambda i, starts: (0, i))
    return pl.pallas_call(
        _slot_kernel,
        grid_spec=pltpu.PrefetchScalarGridSpec(
            num_scalar_prefetch=1, grid=(n // tn,), in_specs=[blk, blk], out_specs=blk),
        out_shape=jax.ShapeDtypeStruct((rows, n), jnp.int32),
        compiler_params=_cparams(("arbitrary",)),
        name="slots",
    )(pad_starts, e_t, r_t)


def _padfill_kernel(ends_ref, padded_ref, xb_ref, zero_ref, sem):
    zero_ref[...] = jnp.zeros_like(zero_ref)
    n_exp = ends_ref.shape[0]
    per_row = zero_ref.shape[0] // MOE_BLOCK

    def tail_copy(e):
        start = pl.multiple_of((ends_ref[e] - MOE_BLOCK) * per_row, MOE_BLOCK)
        return pltpu.make_async_copy(zero_ref, xb_ref.at[pl.ds(start, MOE_BLOCK * per_row)], sem)

    for e in range(n_exp):
        @pl.when(padded_ref[e] > 0)
        def _():
            tail_copy(e).start()
    for e in range(n_exp):
        @pl.when(padded_ref[e] > 0)
        def _():
            tail_copy(e).wait()


def _padfill_call(pad_ends, padded, cap, d):
    return pl.pallas_call(
        _padfill_kernel,
        grid_spec=pltpu.PrefetchScalarGridSpec(
            num_scalar_prefetch=2, grid=(1,), in_specs=[],
            out_specs=pl.BlockSpec(memory_space=pl.ANY),
            scratch_shapes=[pltpu.VMEM((MOE_BLOCK * d // LANES, LANES), F32), pltpu.SemaphoreType.DMA]),
        out_shape=jax.ShapeDtypeStruct((cap * d // LANES, LANES), F32),
        compiler_params=_cparams(("arbitrary",)),
        name="padfill",
    )(pad_ends, padded)


def _row_copy(src_ref, src_row, dst_ref, dst_row, sem, per_row):
    src = pl.ds(pl.multiple_of(src_row * per_row, per_row), per_row)
    dst = pl.ds(pl.multiple_of(dst_row * per_row, per_row), per_row)
    return pltpu.make_async_copy(src_ref.at[src], dst_ref.at[dst], sem)


def _for_row_groups(n_rows, fn):
    def body(g, carry):
        base = pl.multiple_of(g * ROW_GROUP, ROW_GROUP)
        for rr in range(ROW_GROUP):
            fn(base + rr)
        return carry
    lax.fori_loop(0, n_rows // ROW_GROUP, body, 0)


DISPATCH_RING = 3


def _dispatch_kernel(dest_ref, h_ref, xin_ref, xb_ref, hbuf, load_sems, row_sems):
    del xin_ref
    tm = dest_ref.shape[2] // TOP_K
    per_row = hbuf.shape[1] // tm
    i = pl.program_id(0)
    last = pl.num_programs(0) - 1
    slot = i % DISPATCH_RING

    def load(tile, s):
        start = pl.multiple_of(tile * tm * per_row, tm * per_row)
        return pltpu.make_async_copy(h_ref.at[pl.ds(start, tm * per_row)], hbuf.at[s], load_sems.at[s])

    def drain(s):
        for _ in range(TOP_K):
            pltpu.make_async_copy(hbuf.at[s], xb_ref.at[pl.ds(0, tm * per_row)], row_sems.at[s]).wait()

    @pl.when(i == 0)
    def _():
        load(0, 0).start()

    load(i, slot).wait()

    @pl.when(i < last)
    def _():
        load(i + 1, (i + 1) % DISPATCH_RING).start()

    def issue(r):
        for k in range(TOP_K):
            _row_copy(hbuf.at[slot], r, xb_ref, dest_ref[0, 0, k * tm + r],
                      row_sems.at[slot], per_row).start(priority=k % 2)

    _for_row_groups(tm, issue)

    @pl.when(i >= 1)
    def _():
        drain((i + DISPATCH_RING - 1) % DISPATCH_RING)

    @pl.when(i == last)
    def _():
        drain(slot)


def _dispatch_call(dest_tiles, h2, xb_init, *, tm):
    n = dest_tiles.shape[0] * tm
    return pl.pallas_call(
        _dispatch_kernel,
        grid=(n // tm,),
        in_specs=[pl.BlockSpec((1, 1, tm * TOP_K), lambda i: (i, 0, 0), memory_space=pltpu.SMEM),
                  pl.BlockSpec(memory_space=pl.ANY),
                  pl.BlockSpec(memory_space=pl.ANY)],
        out_specs=pl.BlockSpec(memory_space=pl.ANY),
        out_shape=jax.ShapeDtypeStruct(xb_init.shape, F32),
        scratch_shapes=[pltpu.VMEM((DISPATCH_RING, h2.shape[0] // n * tm, LANES), F32),
                        pltpu.SemaphoreType.DMA((DISPATCH_RING,)), pltpu.SemaphoreType.DMA((DISPATCH_RING,))],
        input_output_aliases={2: 0},
        compiler_params=_cparams(("arbitrary",)),
        name="dispatch",
    )(dest_tiles, h2, xb_init)


def _expert_kernel(be_ref, nu_ref, x_ref, w1_ref, b1_ref, w2_ref, b2_ref, y_ref, w1b_ref, w2b_ref):
    i = pl.program_id(0)
    f = w2_ref.shape[1]
    active = i < nu_ref[0]

    @pl.when(active & ((i == 0) | (be_ref[i] != be_ref[jnp.maximum(i - 1, 0)])))
    def _():
        w1b_ref[...] = w1_ref[0].astype(BF16)
        w2b_ref[...] = w2_ref[0].astype(BF16)

    @pl.when(active)
    def _():
        x = _load_row_tiles(x_ref, w1b_ref.shape[0])
        z = jnp.dot(x.astype(BF16), w1b_ref[...], preferred_element_type=F32) + b1_ref[0]
        gate = jnp.minimum(z[:, :f], SWIGLU_LIMIT)
        lin = jnp.clip(z[:, f:], -SWIGLU_LIMIT, SWIGLU_LIMIT)
        act = gate * jax.nn.sigmoid(SWIGLU_ALPHA * gate) * (lin + 1.0)
        _store_row_tiles(y_ref, jnp.dot(act.astype(BF16), w2b_ref[...], preferred_element_type=F32) + b2_ref[0])


def _expert_call(block_e, n_used, xb, w1, b1, w2, b2):
    d = w1.shape[1]
    per_row = d // LANES
    n_blocks = xb.shape[0] // (MOE_BLOCK * per_row)
    f2 = w1.shape[2]
    f = w2.shape[1]

    def blk(i, be, nu):
        return jnp.minimum(i, nu[0] - 1)

    grid_spec = pltpu.PrefetchScalarGridSpec(
        num_scalar_prefetch=2,
        grid=(n_blocks,),
        in_specs=[pl.BlockSpec((MOE_BLOCK * per_row, LANES), lambda i, be, nu: (blk(i, be, nu), 0)),
                  pl.BlockSpec((1, d, f2), lambda i, be, nu: (be[blk(i, be, nu)], 0, 0)),
                  pl.BlockSpec((1, 1, f2), lambda i, be, nu: (be[blk(i, be, nu)], 0, 0)),
                  pl.BlockSpec((1, f, d), lambda i, be, nu: (be[blk(i, be, nu)], 0, 0)),
                  pl.BlockSpec((1, 1, d), lambda i, be, nu: (be[blk(i, be, nu)], 0, 0))],
        out_specs=pl.BlockSpec((MOE_BLOCK * per_row, LANES), lambda i, be, nu: (blk(i, be, nu), 0)),
        scratch_shapes=[pltpu.VMEM((d, f2), BF16), pltpu.VMEM((f, d), BF16)],
    )
    return pl.pallas_call(
        _expert_kernel,
        grid_spec=grid_spec,
        out_shape=jax.ShapeDtypeStruct(xb.shape, F32),
        compiler_params=_cparams(("arbitrary",)),
        name="experts",
    )(block_e, n_used, xb, w1, b1, w2, b2)


COMBINE_RING = 3


def _combine_kernel(dest_ref, nxt_ref, far_ref, x1_ref, w_ref, g2_ref, fg_ref, y_ref, o_ref,
                    buf0, buf1, buf2, sems):
    tm, d = x1_ref.shape
    per_row = d // LANES
    nt = pl.num_programs(1)
    i = pl.program_id(0) * nt + pl.program_id(1)
    last = pl.num_programs(0) * nt - 1
    bufs = (buf0, buf1, buf2)

    def issue(tab_ref, slot, r):
        for k in range(TOP_K):
            _row_copy(y_ref, tab_ref[0, 0, k * tm + r], bufs[slot].at[k], r,
                      sems.at[slot], per_row).start(priority=k % 2)

    def drain(slot):
        for k in range(TOP_K):
            pltpu.make_async_copy(y_ref.at[pl.ds(0, tm * per_row)], bufs[slot].at[k], sems.at[slot]).wait()

    @pl.when(i == 0)
    def _():
        _for_row_groups(tm, lambda r: issue(dest_ref, 0, r))
        _for_row_groups(tm, lambda r: issue(nxt_ref, 1, r))

    for cur in range(COMBINE_RING):
        far = (cur + 2) % COMBINE_RING

        @pl.when(i % COMBINE_RING == cur)
        def _():
            drain(cur)

            def body(g, carry):
                base = pl.multiple_of(g * COMBINE_ROWS, COMBINE_ROWS)
                for rr in range(COMBINE_ROWS):
                    issue(far_ref, far, base + rr)
                rows = pl.ds(base, COMBINE_ROWS)
                w = w_ref[rows, :]
                moe = None
                for k in range(TOP_K):
                    yk = jnp.concatenate(
                        [bufs[cur][k, pl.ds(base * per_row + j, COMBINE_ROWS, stride=per_row), :]
                         for j in range(per_row)], axis=1)
                    moe = w[:, k:k + 1] * yk if moe is None else moe + w[:, k:k + 1] * yk
                x2 = x1_ref[rows, :] + g2_ref[...] * moe
                o_ref[rows, :] = x2 * lax.rsqrt(jnp.mean(x2 * x2, axis=-1, keepdims=True) + EPS) * fg_ref[...]
                return carry

            lax.fori_loop(0, tm // COMBINE_ROWS, body, 0)

            @pl.when(i == last)
            def _():
                drain((cur + 1) % COMBINE_RING)
                drain(far)


def _combine_call(dest_tiles, x1, w_lanes, g2, final_g, y, *, tm):
    bsz, t, d = x1.shape
    nt = t // tm
    tiles = bsz * nt

    def table(shift):
        return pl.BlockSpec((1, 1, tm * TOP_K), lambda b, i: (jnp.minimum(b * nt + i + shift, tiles - 1), 0, 0),
                            memory_space=pltpu.SMEM)

    return pl.pallas_call(
        _combine_kernel,
        grid=(bsz, nt),
        in_specs=[table(0), table(1), table(2),
                  pl.BlockSpec((None, tm, d), lambda b, i: (b, i, 0)),
                  pl.BlockSpec((tm, LANES), lambda b, i: (b * nt + i, 0)),
                  pl.BlockSpec((None, 1, d), lambda b, i: (b, 0, 0)),
                  pl.BlockSpec((1, d), lambda b, i: (0, 0)),
                  pl.BlockSpec(memory_space=pl.ANY)],
        out_specs=pl.BlockSpec((None, tm, d), lambda b, i: (b, i, 0)),
        out_shape=jax.ShapeDtypeStruct((bsz, t, d), F32),
        scratch_shapes=[pltpu.VMEM((TOP_K, tm * d // LANES, LANES), F32)] * COMBINE_RING
        + [pltpu.SemaphoreType.DMA((COMBINE_RING,))],
        compiler_params=_cparams(("arbitrary", "arbitrary")),
        name="combine",
    )(dest_tiles, dest_tiles, dest_tiles, x1, w_lanes, g2, final_g, y)


def kernel(x, c, ctx, c_ctx, norm1_g, norm2_g, w_mod, b_mod, w_in, lb_fwd, lb_bwd, gnorm_g, w_s, b_s,
           w_branch_a, w_branch_b, w_out, w_router, b_router, w1, b1, w2, b2, final_g):
    bsz, t, d = x.shape
    assert w_in.shape[0] == 1 and lb_fwd.shape[0] == 2, "single-layer block"
    assert w_in.shape[2] == N_SPLITS * d and d % HEAD_DIM == 0
    assert t % MLP_CHUNK == 0 and ctx.shape[1] % GLA_CHUNK == 0
    n = bsz * t

    pad_rows = (-(bsz + 1)) % 8
    cc = jnp.concatenate([c, c_ctx[None, :], jnp.zeros((pad_rows, d), F32)], axis=0)
    mod = _mod_call(cc, w_mod[0], b_mod[0][None, :])
    lat = mod[:bsz].reshape(bsz, N_MOD, 1, d)
    sh1, sc1, g1, sh2, sc2, g2 = (lat[:, m] for m in range(N_MOD))
    cmod = mod[bsz].reshape(N_MOD, 1, 1, d)
    csh1, csc1 = cmod[0], cmod[1]

    w_in_bf = w_in[0].astype(BF16)
    n1 = norm1_g[0][None, :]
    q, i_, ff, fb, og, u, v, ga, gb = _proj_call(
        x, n1, sh1, sc1, w_in_bf, lb_fwd, lb_bwd,
        splits=tuple(range(N_SPLITS)), tm=min(256, t), per_batch_mod=True)
    ci, cff, cfb = _proj_call(
        ctx, n1, csh1, csc1, w_in_bf, lb_fwd, lb_bwd,
        splits=(S_I, S_FF, S_FB), tm=ctx.shape[1], per_batch_mod=False)

    ya = _gla_call(q, i_, ff, fb, og, ci, cff, cfb, gnorm_g[0][None, :])

    n_exp = w_router.shape[2]
    tm_io = min(IO_TILE, t)
    exp_rows = -(-n_exp // ROW_GROUP) * ROW_GROUP
    wr_t = jnp.zeros((exp_rows, d), BF16).at[:n_exp].set(w_router[0].T.astype(BF16))
    br_t = jnp.broadcast_to(jnp.full((exp_rows,), NEG_BIG, F32).at[:n_exp].set(b_router[0])[:, None],
                            (exp_rows, min(MIX_SUB, t)))
    bs_full = jnp.broadcast_to(b_s[0][:, :, None], b_s.shape[1:] + (HEAD_DIM,))
    x1, h2, e_t, w_t, r_t, cnt = _mix_call(
        x, u, v, ga, gb, ya, w_s[0].astype(BF16), bs_full,
        w_branch_a[0].astype(BF16), w_branch_b[0].astype(BF16), w_out[0].astype(BF16),
        g1, norm2_g[0][None, :], sh2, sc2, wr_t, br_t, tm=min(MIX_BLOCK, t))

    counts = cnt[:n_exp, 0].astype(jnp.int32)
    padded = (counts + MOE_BLOCK - 1) // MOE_BLOCK * MOE_BLOCK
    pad_ends = jnp.cumsum(padded)
    pad_starts = pad_ends - padded
    n_blocks = -(-(n * TOP_K) // MOE_BLOCK) + n_exp
    cap = n_blocks * MOE_BLOCK
    dest = _slot_call(pad_starts, e_t, r_t, tn=min(8192, n))[:TOP_K]
    blk_start = jnp.arange(n_blocks, dtype=jnp.int32) * MOE_BLOCK
    block_e = jnp.minimum(jnp.sum((pad_ends[None, :] <= blk_start[:, None]).astype(jnp.int32), axis=1),
                          n_exp - 1)
    n_used = (pad_ends[-1:] // MOE_BLOCK).astype(jnp.int32)

    dest_tiles = dest.reshape(TOP_K, n // tm_io, tm_io).transpose(1, 0, 2).reshape(n // tm_io, 1, TOP_K * tm_io)
    w_l = jnp.zeros((n, LANES), F32).at[:, :TOP_K].set(w_t[:TOP_K].T)
    xb = _dispatch_call(dest_tiles, h2, _padfill_call(pad_ends, padded, cap, d), tm=tm_io)
    y = _expert_call(block_e, n_used, xb, w1[0], b1[0][:, None, :], w2[0], b2[0][:, None, :])
    return _combine_call(dest_tiles, x1, w_l, g2, final_g[None, :], y, tm=tm_io)
```

```python
import functools

import jax
import jax.numpy as jnp
from jax import lax
from jax.experimental import pallas as pl
from jax.experimental.pallas import tpu as pltpu

F32 = jnp.float32
BF16 = jnp.bfloat16

EPS = 1e-6
N_MOD = 6
S_Q, S_I, S_FF, S_FB, S_OG, S_U, S_V, S_GA, S_GB = range(9)
N_SPLITS = 9
SPLIT_DTYPE = {S_FF: F32, S_FB: F32}
HEAD_DIM = 128
GLA_CHUNK = 64
GLA_HEADS_PER_STEP = 4
MIX_BLOCK = 512
MIX_SUB = 256
IO_TILE = 512
MLP_CHUNK = 128
TOP_K = 4
MOE_BLOCK = 512
ROW_GROUP = 8
COMBINE_ROWS = 128
SWIGLU_LIMIT = 7.0
SWIGLU_ALPHA = 1.702
LANES = 128
NEG_BIG = -1e30
VMEM_LIMIT = 56 * 1024 * 1024


def _cparams(sem):
    return pltpu.CompilerParams(dimension_semantics=sem, vmem_limit_bytes=VMEM_LIMIT)


def _resident(shape):
    zeros = (0,) * len(shape)
    return pl.BlockSpec(shape, lambda *_: zeros, pipeline_mode=pl.Buffered(1))


def _mod_kernel(c_ref, w_ref, b_ref, o_ref):
    c = c_ref[...]
    h = (c * jax.nn.sigmoid(c)).astype(BF16)
    o_ref[...] = jnp.dot(h, w_ref[...].astype(BF16), preferred_element_type=F32) + b_ref[...]


def _mod_call(cc, w_mod, b_mod):
    rows, d = cc.shape
    width = w_mod.shape[1]
    return pl.pallas_call(
        _mod_kernel,
        grid=(width // d,),
        in_specs=[pl.BlockSpec((rows, d), lambda j: (0, 0)),
                  pl.BlockSpec((d, d), lambda j: (0, j)),
                  pl.BlockSpec((1, d), lambda j: (0, j))],
        out_specs=pl.BlockSpec((rows, d), lambda j: (0, j)),
        out_shape=jax.ShapeDtypeStruct((rows, width), F32),
        compiler_params=_cparams(("arbitrary",)),
        name="mod",
    )(cc, w_mod, b_mod)


def _lower_bound(lb_ref):
    a = lb_ref[0:1, :]
    b = lb_ref[1:2, :]
    m = jnp.maximum(a, b)
    ea = jnp.exp(a - m)
    eb = jnp.exp(b - m)
    return ea / (ea + eb)


def _gelu(z):
    return 0.5 * z * (1.0 + lax.erf(z * (2.0 ** -0.5)))


def _split_activation(s, z, lbf_ref, lbb_ref):
    if s in (S_Q, S_OG):
        return z * jax.nn.sigmoid(z)
    if s == S_I:
        return z
    if s in (S_FF, S_FB):
        lb = _lower_bound(lbf_ref if s == S_FF else lbb_ref)
        return lb + (1.0 - lb) * jax.nn.sigmoid(z)
    if s in (S_U, S_V):
        return _gelu(z)
    return jax.nn.sigmoid(z)


def _proj_kernel(x_ref, g_ref, sh_ref, sc_ref, w_ref, lbf_ref, lbb_ref, *o_refs, splits):
    d = x_ref.shape[1]
    x = x_ref[...]
    y = x * lax.rsqrt(jnp.mean(x * x, axis=-1, keepdims=True) + EPS) * g_ref[...]
    h = (y * (1.0 + sc_ref[...]) + sh_ref[...]).astype(BF16)
    for s, o_ref in zip(splits, o_refs):
        z = jnp.dot(h, w_ref[:, s * d:(s + 1) * d], preferred_element_type=F32)
        o_ref[...] = _split_activation(s, z, lbf_ref, lbb_ref).astype(o_ref.dtype)


def _proj_call(x, g, sh, sc, w_bf, lbf, lbb, *, splits, tm, per_batch_mod):
    bsz, t, d = x.shape
    mod_map = (lambda b, i: (b, 0, 0)) if per_batch_mod else (lambda b, i: (0, 0, 0))
    row = pl.BlockSpec((None, tm, d), lambda b, i: (b, i, 0))
    return pl.pallas_call(
        functools.partial(_proj_kernel, splits=splits),
        grid=(bsz, t // tm),
        in_specs=[row,
                  pl.BlockSpec((1, d), lambda b, i: (0, 0)),
                  pl.BlockSpec((None, 1, d), mod_map),
                  pl.BlockSpec((None, 1, d), mod_map),
                  _resident(w_bf.shape),
                  pl.BlockSpec((2, d), lambda b, i: (0, 0)),
                  pl.BlockSpec((2, d), lambda b, i: (0, 0))],
        out_specs=[row for _ in splits],
        out_shape=[jax.ShapeDtypeStruct((bsz, t, d), SPLIT_DTYPE.get(s, BF16)) for s in splits],
        compiler_params=_cparams(("arbitrary", "arbitrary")),
        name="proj",
    )(x, g, sh, sc, w_bf, lbf, lbb)


def _prefix_sum(x, reverse):
    n = x.shape[0]
    row = lax.broadcasted_iota(jnp.int32, x.shape, 0)
    s = 1
    while s < n:
        if reverse:
            x = x + jnp.where(row < n - s, pltpu.roll(x, n - s, 0), 0.0)
        else:
            x = x + jnp.where(row >= s, pltpu.roll(x, s, 0), 0.0)
        s *= 2
    return x


def _dot_nt(a, b):
    return lax.dot_general(a, b, (((1,), (1,)), ((), ())), preferred_element_type=F32)


def _dot_tn(a, b):
    return lax.dot_general(a, b, (((0,), (0,)), ((), ())), preferred_element_type=F32)


def _gla_prepare(q, f, reverse):
    c = q.shape[0]
    b = _prefix_sum(jnp.log(f), reverse)
    if reverse:
        b_mid = b[c // 2:c // 2 + 1, :]
        b_end = b[0:1, :]
    else:
        b_mid = b[c // 2 - 1:c // 2, :]
        b_end = b[c - 1:c, :]
    qe = q * jnp.exp(b - b_mid)
    ke = (1.0 - f) * jnp.exp(b_mid - b)
    kd = ke * jnp.exp(b_end - b_mid)
    return qe.astype(BF16), ke.astype(BF16), kd.astype(BF16), jnp.exp(b_mid), jnp.exp(b_end)


def _gla_scores(qe, ke, st, e_mid):
    rhs = jnp.concatenate([(st * e_mid).astype(BF16), ke], axis=0)
    return _dot_nt(qe, rhs)


def _gla_finish(both, kd, vb, st, e_end, reverse):
    c = both.shape[0]
    dv = st.shape[0]
    ti = lax.broadcasted_iota(jnp.int32, (c, c), 0)
    si = lax.broadcasted_iota(jnp.int32, (c, c), 1)
    keep = (si >= ti) if reverse else (si <= ti)
    scores = jnp.where(keep, both[:, dv:], 0.0)
    o = both[:, :dv] + jnp.dot(scores.astype(BF16), vb, preferred_element_type=F32)
    return o, st * e_end + _dot_tn(vb, kd)


def _context_state(f, vb, reverse_decay):
    g = jnp.log(f)
    tail = _prefix_sum(g, reverse_decay) - g
    kt = ((1.0 - f) * jnp.exp(tail)).astype(BF16)
    return _dot_tn(vb, kt)


def _gla_kernel(q_ref, i_ref, ff_ref, fb_ref, og_ref, ci_ref, cff_ref, cfb_ref, gn_ref,
                o_ref, of_ref, ob_ref, sf_ref, sb_ref, ops_f, ops_b, rows_f, rows_b):
    t = q_ref.shape[0]
    c = GLA_CHUNK
    n = t // c
    heads = [slice(h * HEAD_DIM, (h + 1) * HEAD_DIM) for h in range(q_ref.shape[1] // HEAD_DIM)]
    for h, cols in enumerate(heads):
        ci = ci_ref[:, cols]
        sf_ref[h] = _context_state(cff_ref[:, cols], ci, True)
        sb_ref[h] = _context_state(cfb_ref[:, cols], ci, False)

    directions = ((ff_ref, ops_f, rows_f, sf_ref, of_ref, False), (fb_ref, ops_b, rows_b, sb_ref, ob_ref, True))

    def chunk_rows(ck):
        return pl.ds(pl.multiple_of(ck * c, c), c)

    def prepare(step):
        slot = step % 2
        for cols in heads:
            for f_ref, ops, rows, _, _, reverse in directions:
                r = chunk_rows((n - 1 - step) if reverse else step)
                qe, ke, kd, e_mid, e_end = _gla_prepare(q_ref[r, cols].astype(F32), f_ref[r, cols], reverse)
                ops[slot, 0, :, cols] = qe
                ops[slot, 1, :, cols] = ke
                ops[slot, 2, :, cols] = kd
                rows[slot, 0, :, cols] = jnp.broadcast_to(e_mid, (8, HEAD_DIM))
                rows[slot, 1, :, cols] = jnp.broadcast_to(e_end, (8, HEAD_DIM))

    prepare(0)

    def scan(step, carry):
        slot = step % 2
        chains = []
        for h, cols in enumerate(heads):
            for _, ops, rows, s_ref, acc, reverse in directions:
                chains.append((h, cols, ops, rows, s_ref, acc, reverse,
                               chunk_rows((n - 1 - step) if reverse else step)))
        firsts = []
        for h, cols, ops, rows, s_ref, acc, reverse, r in chains:
            st = s_ref[h]
            firsts.append((st, _gla_scores(ops[slot, 0, :, cols], ops[slot, 1, :, cols], st,
                                           rows[slot, 0, 0:1, cols])))
        for (h, cols, ops, rows, s_ref, acc, reverse, r), (st, both) in zip(chains, firsts):
            o, st = _gla_finish(both, ops[slot, 2, :, cols], i_ref[r, cols], st, rows[slot, 1, 0:1, cols],
                                reverse)
            acc[r, cols] = o
            s_ref[h] = st
        prepare(jnp.minimum(step + 1, n - 1))
        return carry

    lax.fori_loop(0, n, scan, 0)
    for h, cols in enumerate(heads):
        o = of_ref[:, cols] + ob_ref[:, cols]
        y = o * lax.rsqrt(jnp.mean(o * o, axis=-1, keepdims=True) + EPS) * gn_ref[...]
        o_ref[:, cols] = (y * og_ref[:, cols].astype(F32)).astype(o_ref.dtype)


def _gla_call(q, i, ff, fb, og, ci, cff, cfb, gnorm):
    bsz, t, d = q.shape
    width = GLA_HEADS_PER_STEP * HEAD_DIM
    tc = ci.shape[1]
    lat = pl.BlockSpec((None, t, width), lambda b, h: (b, 0, h))
    ctx = pl.BlockSpec((None, tc, width), lambda b, h: (b, 0, h))
    out_acc = pltpu.VMEM((t, width), F32)
    state = pltpu.VMEM((GLA_HEADS_PER_STEP, HEAD_DIM, HEAD_DIM), F32)
    operands = pltpu.VMEM((2, 3, GLA_CHUNK, width), BF16)
    decay_rows = pltpu.VMEM((2, 2, 8, width), F32)
    return pl.pallas_call(
        _gla_kernel,
        grid=(bsz, d // width),
        in_specs=[lat, lat, lat, lat, lat, ctx, ctx, ctx,
                  pl.BlockSpec((1, HEAD_DIM), lambda b, h: (0, 0))],
        out_specs=lat,
        out_shape=jax.ShapeDtypeStruct((bsz, t, d), BF16),
        scratch_shapes=[out_acc, out_acc, state, state, operands, operands, decay_rows, decay_rows],
        compiler_params=_cparams(("arbitrary", "arbitrary")),
        name="gla",
    )(q, i, ff, fb, og, ci, cff, cfb, gnorm)


def _store_row_tiles(ref, value):
    rows, d = value.shape
    per_row = d // LANES
    for j in range(per_row):
        ref[pl.ds(j, rows, stride=per_row), :] = value[:, j * LANES:(j + 1) * LANES]


def _load_row_tiles(ref, d):
    per_row = d // LANES
    rows = ref.shape[0] // per_row
    return jnp.concatenate([ref[pl.ds(j, rows, stride=per_row), :] for j in range(per_row)], axis=1)


def _mix_kernel(x_ref, u_ref, v_ref, ga_ref, gb_ref, ya_ref, ws_ref, bs_ref, wa_ref, wb_ref, wo_ref,
                g1_ref, n2_ref, sh2_ref, sc2_ref, wr_ref, br_ref,
                x1_ref, h2_ref, e_ref, w_ref, rank_ref, cnt_ref, yb_ref, carry_ref):
    tm, d = x_ref.shape
    sub = carry_ref.shape[1]
    per_row = d // LANES
    first = (pl.program_id(0) == 0) & (pl.program_id(1) == 0)

    @pl.when(first)
    def _():
        carry_ref[...] = jnp.zeros_like(carry_ref)

    subs = [(s0, pl.ds(s0, sub), slice(s0, s0 + sub)) for s0 in range(0, tm, sub)]

    for s0, rs, _ in subs:
        v = v_ref[rs, :].astype(F32)
        mu = jnp.mean(v, axis=-1, keepdims=True)
        vc = v - mu
        var = jnp.mean(vc * vc, axis=-1, keepdims=True)
        vn = (vc * lax.rsqrt(var + EPS)).astype(BF16)
        for ck in range(sub // MLP_CHUNK):
            rows = slice(ck * MLP_CHUNK, (ck + 1) * MLP_CHUNK)
            orow = slice(s0 + ck * MLP_CHUNK, s0 + (ck + 1) * MLP_CHUNK)
            for g in range(d // HEAD_DIM):
                cols = slice(g * HEAD_DIM, (g + 1) * HEAD_DIM)
                s = jnp.dot(ws_ref[g], vn[rows, cols], preferred_element_type=F32) + bs_ref[g]
                yb_ref[orow, cols] = (u_ref[orow, cols].astype(F32) * s).astype(BF16)

    merged = []
    for _, rs, _ in subs:
        pa = jnp.dot(ya_ref[rs, :], wa_ref[...], preferred_element_type=F32)
        pb = jnp.dot(yb_ref[rs, :], wb_ref[...], preferred_element_type=F32)
        merged.append((ga_ref[rs, :].astype(F32) * pa + gb_ref[rs, :].astype(F32) * pb).astype(BF16))

    logits = []
    for (s0, rs, _), mg in zip(subs, merged):
        mixed = jnp.dot(mg, wo_ref[...], preferred_element_type=F32)
        x1 = x_ref[rs, :] + g1_ref[...] * mixed
        x1_ref[rs, :] = x1
        y = x1 * lax.rsqrt(jnp.mean(x1 * x1, axis=-1, keepdims=True) + EPS) * n2_ref[...]
        h2 = y * (1.0 + sc2_ref[...]) + sh2_ref[...]
        _store_row_tiles(h2_ref.at[pl.ds(s0 * per_row, sub * per_row)], h2)
        logits.append(_dot_nt(wr_ref[...], h2.astype(BF16)) + br_ref[...])

    for (_, _, ls), work in zip(subs, logits):
        n_rows = work.shape[0]
        eidx = lax.broadcasted_iota(jnp.int32, (n_rows, sub), 0)
        vals, idxs = [], []
        for _ in range(TOP_K):
            m = jnp.max(work, axis=0, keepdims=True)
            idx = jnp.min(jnp.where(work == m, eidx, n_rows), axis=0, keepdims=True)
            vals.append(m)
            idxs.append(idx)
            work = jnp.where(eidx == idx, -jnp.inf, work)
        exps = [jnp.exp(m - vals[0]) for m in vals]
        denom = exps[0] + exps[1] + exps[2] + exps[3]

        sel = jnp.zeros((n_rows, sub), F32)
        for idx in idxs:
            sel = sel + jnp.where(eidx == idx, 1.0, 0.0)
        tj = lax.broadcasted_iota(jnp.int32, (sub, sub), 0)
        tt = lax.broadcasted_iota(jnp.int32, (sub, sub), 1)
        tri = jnp.where(tj < tt, 1.0, 0.0).astype(BF16)
        before = carry_ref[...] + jnp.dot(sel.astype(BF16), tri, preferred_element_type=F32)

        out_shape = (e_ref.shape[0], sub)
        krow = lax.broadcasted_iota(jnp.int32, out_shape, 0)
        e_out = jnp.zeros(out_shape, jnp.int32)
        w_out = jnp.zeros(out_shape, F32)
        r_out = jnp.zeros(out_shape, F32)
        for k in range(TOP_K):
            rk = jnp.sum(jnp.where(eidx == idxs[k], before, 0.0), axis=0, keepdims=True)
            e_out = jnp.where(krow == k, idxs[k], e_out)
            w_out = jnp.where(krow == k, exps[k] / denom, w_out)
            r_out = jnp.where(krow == k, rk, r_out)
        e_ref[:, ls] = e_out
        w_ref[:, ls] = w_out
        rank_ref[:, ls] = r_out.astype(jnp.int32)
        carry_ref[...] = carry_ref[...] + jnp.sum(sel, axis=1, keepdims=True)
    cnt_ref[...] = carry_ref[...]


def _mix_call(x, u, v, ga, gb, ya, ws_bf, bs_full, wa_bf, wb_bf, wo_bf, g1, n2, sh2, sc2, wr_t, br_t, *, tm):
    bsz, t, d = x.shape
    n = bsz * t
    nt = t // tm
    n_rows, sub = br_t.shape
    row = pl.BlockSpec((None, tm, d), lambda b, i: (b, i, 0))
    per_batch = pl.BlockSpec((None, 1, d), lambda b, i: (b, 0, 0))
    per_token = pl.BlockSpec((ROW_GROUP, tm), lambda b, i: (0, b * nt + i))

    return pl.pallas_call(
        _mix_kernel,
        grid=(bsz, nt),
        in_specs=[row, row, row, row, row, row,
                  _resident(ws_bf.shape), _resident(bs_full.shape),
                  _resident((d, d)), _resident((d, d)), _resident((d, d)),
                  per_batch, _resident((1, d)), per_batch, per_batch,
                  _resident(wr_t.shape), _resident(br_t.shape)],
        out_specs=[row, pl.BlockSpec((tm * d // LANES, LANES), lambda b, i: (b * nt + i, 0)),
                   per_token, per_token, per_token,
                   pl.BlockSpec((n_rows, sub), lambda b, i: (0, 0))],
        out_shape=[jax.ShapeDtypeStruct((bsz, t, d), F32),
                   jax.ShapeDtypeStruct((n * d // LANES, LANES), F32),
                   jax.ShapeDtypeStruct((ROW_GROUP, n), jnp.int32),
                   jax.ShapeDtypeStruct((ROW_GROUP, n), F32),
                   jax.ShapeDtypeStruct((ROW_GROUP, n), jnp.int32),
                   jax.ShapeDtypeStruct((n_rows, sub), F32)],
        scratch_shapes=[pltpu.VMEM((tm, d), BF16), pltpu.VMEM((n_rows, sub), F32)],
        compiler_params=_cparams(("arbitrary", "arbitrary")),
        name="mix",
    )(x, u, v, ga, gb, ya, ws_bf, bs_full, wa_bf, wb_bf, wo_bf, g1, n2, sh2, sc2, wr_t, br_t)


def _slot_kernel(starts_ref, e_ref, r_ref, o_ref):
    e = e_ref[...]
    start = jnp.zeros(e.shape, jnp.int32)
    for x in range(starts_ref.shape[0]):
        start = jnp.where(e == x, starts_ref[x], start)
    o_ref[...] = start + r_ref[...]


def _slot_call(pad_starts, e_t, r_t, *, tn):
    rows, n = e_t.shape
    blk = pl.BlockSpec((rows, tn), lambda i, starts: (0, i))
    return pl.pallas_call(
        _slot_kernel,
        grid_spec=pltpu.PrefetchScalarGridSpec(
            num_scalar_prefetch=1, grid=(n // tn,), in_specs=[blk, blk], out_specs=blk),
        out_shape=jax.ShapeDtypeStruct((rows, n), jnp.int32),
        compiler_params=_cparams(("arbitrary",)),
        name="slots",
    )(pad_starts, e_t, r_t)


def _padfill_kernel(ends_ref, padded_ref, xb_ref, zero_ref, sem):
    zero_ref[...] = jnp.zeros_like(zero_ref)
    n_exp = ends_ref.shape[0]
    per_row = zero_ref.shape[0] // MOE_BLOCK

    def tail_copy(e):
        start = pl.multiple_of((ends_ref[e] - MOE_BLOCK) * per_row, MOE_BLOCK)
        return pltpu.make_async_copy(zero_ref, xb_ref.at[pl.ds(start, MOE_BLOCK * per_row)], sem)

    for e in range(n_exp):
        @pl.when(padded_ref[e] > 0)
        def _():
            tail_copy(e).start()
    for e in range(n_exp):
        @pl.when(padded_ref[e] > 0)
        def _():
            tail_copy(e).wait()


def _padfill_call(pad_ends, padded, cap, d):
    return pl.pallas_call(
        _padfill_kernel,
        grid_spec=pltpu.PrefetchScalarGridSpec(
            num_scalar_prefetch=2, grid=(1,), in_specs=[],
            out_specs=pl.BlockSpec(memory_space=pl.ANY),
            scratch_shapes=[pltpu.VMEM((MOE_BLOCK * d // LANES, LANES), F32), pltpu.SemaphoreType.DMA]),
        out_shape=jax.ShapeDtypeStruct((cap * d // LANES, LANES), F32),
        compiler_params=_cparams(("arbitrary",)),
        name="padfill",
    )(pad_ends, padded)


def _row_copy(src_ref, src_row, dst_ref, dst_row, sem, per_row):
    src = pl.ds(pl.multiple_of(src_row * per_row, per_row), per_row)
    dst = pl.ds(pl.multiple_of(dst_row * per_row, per_row), per_row)
    return pltpu.make_async_copy(src_ref.at[src], dst_ref.at[dst], sem)


def _for_row_groups(n_rows, fn):
    def body(g, carry):
        base = pl.multiple_of(g * ROW_GROUP, ROW_GROUP)
        for rr in range(ROW_GROUP):
            fn(base + rr)
        return carry
    lax.fori_loop(0, n_rows // ROW_GROUP, body, 0)


DISPATCH_RING = 3


def _dispatch_kernel(dest_ref, h_ref, xin_ref, xb_ref, hbuf, load_sems, row_sems):
    del xin_ref
    tm = dest_ref.shape[2] // TOP_K
    per_row = hbuf.shape[1] // tm
    i = pl.program_id(0)
    last = pl.num_programs(0) - 1
    slot = i % DISPATCH_RING

    def load(tile, s):
        start = pl.multiple_of(tile * tm * per_row, tm * per_row)
        return pltpu.make_async_copy(h_ref.at[pl.ds(start, tm * per_row)], hbuf.at[s], load_sems.at[s])

    def drain(s):
        for _ in range(TOP_K):
            pltpu.make_async_copy(hbuf.at[s], xb_ref.at[pl.ds(0, tm * per_row)], row_sems.at[s]).wait()

    @pl.when(i == 0)
    def _():
        load(0, 0).start()

    load(i, slot).wait()

    @pl.when(i < last)
    def _():
        load(i + 1, (i + 1) % DISPATCH_RING).start()

    def issue(r):
        for k in range(TOP_K):
            _row_copy(hbuf.at[slot], r, xb_ref, dest_ref[0, 0, k * tm + r],
                      row_sems.at[slot], per_row).start(priority=k % 2)

    _for_row_groups(tm, issue)

    @pl.when(i >= 1)
    def _():
        drain((i + DISPATCH_RING - 1) % DISPATCH_RING)

    @pl.when(i == last)
    def _():
        drain(slot)


def _dispatch_call(dest_tiles, h2, xb_init, *, tm):
    n = dest_tiles.shape[0] * tm
    return pl.pallas_call(
        _dispatch_kernel,
        grid=(n // tm,),
        in_specs=[pl.BlockSpec((1, 1, tm * TOP_K), lambda i: (i, 0, 0), memory_space=pltpu.SMEM),
                  pl.BlockSpec(memory_space=pl.ANY),
                  pl.BlockSpec(memory_space=pl.ANY)],
        out_specs=pl.BlockSpec(memory_space=pl.ANY),
        out_shape=jax.ShapeDtypeStruct(xb_init.shape, F32),
        scratch_shapes=[pltpu.VMEM((DISPATCH_RING, h2.shape[0] // n * tm, LANES), F32),
                        pltpu.SemaphoreType.DMA((DISPATCH_RING,)), pltpu.SemaphoreType.DMA((DISPATCH_RING,))],
        input_output_aliases={2: 0},
        compiler_params=_cparams(("arbitrary",)),
        name="dispatch",
    )(dest_tiles, h2, xb_init)


def _expert_kernel(be_ref, nu_ref, x_ref, w1_ref, b1_ref, w2_ref, b2_ref, y_ref, w1b_ref, w2b_ref):
    i = pl.program_id(0)
    f = w2_ref.shape[1]
    active = i < nu_ref[0]

    @pl.when(active & ((i == 0) | (be_ref[i] != be_ref[jnp.maximum(i - 1, 0)])))
    def _():
        w1b_ref[...] = w1_ref[0].astype(BF16)
        w2b_ref[...] = w2_ref[0].astype(BF16)

    @pl.when(active)
    def _():
        x = _load_row_tiles(x_ref, w1b_ref.shape[0])
        z = jnp.dot(x.astype(BF16), w1b_ref[...], preferred_element_type=F32) + b1_ref[0]
        gate = jnp.minimum(z[:, :f], SWIGLU_LIMIT)
        lin = jnp.clip(z[:, f:], -SWIGLU_LIMIT, SWIGLU_LIMIT)
        act = gate * jax.nn.sigmoid(SWIGLU_ALPHA * gate) * (lin + 1.0)
        _store_row_tiles(y_ref, jnp.dot(act.astype(BF16), w2b_ref[...], preferred_element_type=F32) + b2_ref[0])


def _expert_call(block_e, n_used, xb, w1, b1, w2, b2):
    d = w1.shape[1]
    per_row = d // LANES
    n_blocks = xb.shape[0] // (MOE_BLOCK * per_row)
    f2 = w1.shape[2]
    f = w2.shape[1]

    def blk(i, be, nu):
        return jnp.minimum(i, nu[0] - 1)

    grid_spec = pltpu.PrefetchScalarGridSpec(
        num_scalar_prefetch=2,
        grid=(n_blocks,),
        in_specs=[pl.BlockSpec((MOE_BLOCK * per_row, LANES), lambda i, be, nu: (blk(i, be, nu), 0)),
                  pl.BlockSpec((1, d, f2), lambda i, be, nu: (be[blk(i, be, nu)], 0, 0)),
                  pl.BlockSpec((1, 1, f2), lambda i, be, nu: (be[blk(i, be, nu)], 0, 0)),
                  pl.BlockSpec((1, f, d), lambda i, be, nu: (be[blk(i, be, nu)], 0, 0)),
                  pl.BlockSpec((1, 1, d), lambda i, be, nu: (be[blk(i, be, nu)], 0, 0))],
        out_specs=pl.BlockSpec((MOE_BLOCK * per_row, LANES), lambda i, be, nu: (blk(i, be, nu), 0)),
        scratch_shapes=[pltpu.VMEM((d, f2), BF16), pltpu.VMEM((f, d), BF16)],
    )
    return pl.pallas_call(
        _expert_kernel,
        grid_spec=grid_spec,
        out_shape=jax.ShapeDtypeStruct(xb.shape, F32),
        compiler_params=_cparams(("arbitrary",)),
        name="experts",
    )(block_e, n_used, xb, w1, b1, w2, b2)


COMBINE_RING = 3


def _combine_kernel(dest_ref, nxt_ref, far_ref, x1_ref, w_ref, g2_ref, fg_ref, y_ref, o_ref,
                    buf0, buf1, buf2, sems):
    tm, d = x1_ref.shape
    per_row = d // LANES
    nt = pl.num_programs(1)
    i = pl.program_id(0) * nt + pl.program_id(1)
    last = pl.num_programs(0) * nt - 1
    bufs = (buf0, buf1, buf2)

    def issue(tab_ref, slot, r):
        for k in range(TOP_K):
            _row_copy(y_ref, tab_ref[0, 0, k * tm + r], bufs[slot].at[k], r,
                      sems.at[slot], per_row).start(priority=k % 2)

    def drain(slot):
        for k in range(TOP_K):
            pltpu.make_async_copy(y_ref.at[pl.ds(0, tm * per_row)], bufs[slot].at[k], sems.at[slot]).wait()

    @pl.when(i == 0)
    def _():
        _for_row_groups(tm, lambda r: issue(dest_ref, 0, r))
        _for_row_groups(tm, lambda r: issue(nxt_ref, 1, r))

    for cur in range(COMBINE_RING):
        far = (cur + 2) % COMBINE_RING

        @pl.when(i % COMBINE_RING == cur)
        def _():
            drain(cur)

            def body(g, carry):
                base = pl.multiple_of(g * COMBINE_ROWS, COMBINE_ROWS)
                for rr in range(COMBINE_ROWS):
                    issue(far_ref, far, base + rr)
                rows = pl.ds(base, COMBINE_ROWS)
                w = w_ref[rows, :]
                moe = None
                for k in range(TOP_K):
                    yk = jnp.concatenate(
                        [bufs[cur][k, pl.ds(base * per_row + j, COMBINE_ROWS, stride=per_row), :]
                         for j in range(per_row)], axis=1)
                    moe = w[:, k:k + 1] * yk if moe is None else moe + w[:, k:k + 1] * yk
                x2 = x1_ref[rows, :] + g2_ref[...] * moe
                o_ref[rows, :] = x2 * lax.rsqrt(jnp.mean(x2 * x2, axis=-1, keepdims=True) + EPS) * fg_ref[...]
                return carry

            lax.fori_loop(0, tm // COMBINE_ROWS, body, 0)

            @pl.when(i == last)
            def _():
                drain((cur + 1) % COMBINE_RING)
                drain(far)


def _combine_call(dest_tiles, x1, w_lanes, g2, final_g, y, *, tm):
    bsz, t, d = x1.shape
    nt = t // tm
    tiles = bsz * nt

    def table(shift):
        return pl.BlockSpec((1, 1, tm * TOP_K), lambda b, i: (jnp.minimum(b * nt + i + shift, tiles - 1), 0, 0),
                            memory_space=pltpu.SMEM)

    return pl.pallas_call(
        _combine_kernel,
        grid=(bsz, nt),
        in_specs=[table(0), table(1), table(2),
                  pl.BlockSpec((None, tm, d), lambda b, i: (b, i, 0)),
                  pl.BlockSpec((tm, LANES), lambda b, i: (b * nt + i, 0)),
                  pl.BlockSpec((None, 1, d), lambda b, i: (b, 0, 0)),
                  pl.BlockSpec((1, d), lambda b, i: (0, 0)),
                  pl.BlockSpec(memory_space=pl.ANY)],
        out_specs=pl.BlockSpec((None, tm, d), lambda b, i: (b, i, 0)),
        out_shape=jax.ShapeDtypeStruct((bsz, t, d), F32),
        scratch_shapes=[pltpu.VMEM((TOP_K, tm * d // LANES, LANES), F32)] * COMBINE_RING
        + [pltpu.SemaphoreType.DMA((COMBINE_RING,))],
        compiler_params=_cparams(("arbitrary", "arbitrary")),
        name="combine",
    )(dest_tiles, dest_tiles, dest_tiles, x1, w_lanes, g2, final_g, y)


def kernel(x, c, ctx, c_ctx, norm1_g, norm2_g, w_mod, b_mod, w_in, lb_fwd, lb_bwd, gnorm_g, w_s, b_s,
           w_branch_a, w_branch_b, w_out, w_router, b_router, w1, b1, w2, b2, final_g):
    bsz, t, d = x.shape
    assert w_in.shape[0] == 1 and lb_fwd.shape[0] == 2, "single-layer block"
    assert w_in.shape[2] == N_SPLITS * d and d % HEAD_DIM == 0
    assert t % MLP_CHUNK == 0 and ctx.shape[1] % GLA_CHUNK == 0
    n = bsz * t

    pad_rows = (-(bsz + 1)) % 8
    cc = jnp.concatenate([c, c_ctx[None, :], jnp.zeros((pad_rows, d), F32)], axis=0)
    mod = _mod_call(cc, w_mod[0], b_mod[0][None, :])
    lat = mod[:bsz].reshape(bsz, N_MOD, 1, d)
    sh1, sc1, g1, sh2, sc2, g2 = (lat[:, m] for m in range(N_MOD))
    cmod = mod[bsz].reshape(N_MOD, 1, 1, d)
    csh1, csc1 = cmod[0], cmod[1]

    w_in_bf = w_in[0].astype(BF16)
    n1 = norm1_g[0][None, :]
    q, i_, ff, fb, og, u, v, ga, gb = _proj_call(
        x, n1, sh1, sc1, w_in_bf, lb_fwd, lb_bwd,
        splits=tuple(range(N_SPLITS)), tm=min(256, t), per_batch_mod=True)
    ci, cff, cfb = _proj_call(
        ctx, n1, csh1, csc1, w_in_bf, lb_fwd, lb_bwd,
        splits=(S_I, S_FF, S_FB), tm=ctx.shape[1], per_batch_mod=False)

    ya = _gla_call(q, i_, ff, fb, og, ci, cff, cfb, gnorm_g[0][None, :])

    n_exp = w_router.shape[2]
    tm_io = min(IO_TILE, t)
    exp_rows = -(-n_exp // ROW_GROUP) * ROW_GROUP
    wr_t = jnp.zeros((exp_rows, d), BF16).at[:n_exp].set(w_router[0].T.astype(BF16))
    br_t = jnp.broadcast_to(jnp.full((exp_rows,), NEG_BIG, F32).at[:n_exp].set(b_router[0])[:, None],
                            (exp_rows, min(MIX_SUB, t)))
    bs_full = jnp.broadcast_to(b_s[0][:, :, None], b_s.shape[1:] + (HEAD_DIM,))
    x1, h2, e_t, w_t, r_t, cnt = _mix_call(
        x, u, v, ga, gb, ya, w_s[0].astype(BF16), bs_full,
        w_branch_a[0].astype(BF16), w_branch_b[0].astype(BF16), w_out[0].astype(BF16),
        g1, norm2_g[0][None, :], sh2, sc2, wr_t, br_t, tm=min(MIX_BLOCK, t))

    counts = cnt[:n_exp, 0].astype(jnp.int32)
    padded = (counts + MOE_BLOCK - 1) // MOE_BLOCK * MOE_BLOCK
    pad_ends = jnp.cumsum(padded)
    pad_starts = pad_ends - padded
    n_blocks = -(-(n * TOP_K) // MOE_BLOCK) + n_exp
    cap = n_blocks * MOE_BLOCK
    dest = _slot_call(pad_starts, e_t, r_t, tn=min(8192, n))[:TOP_K]
    blk_start = jnp.arange(n_blocks, dtype=jnp.int32) * MOE_BLOCK
    block_e = jnp.minimum(jnp.sum((pad_ends[None, :] <= blk_start[:, None]).astype(jnp.int32), axis=1),
                          n_exp - 1)
    n_used = (pad_ends[-1:] // MOE_BLOCK).astype(jnp.int32)

    dest_tiles = dest.reshape(TOP_K, n // tm_io, tm_io).transpose(1, 0, 2).reshape(n // tm_io, 1, TOP_K * tm_io)
    w_l = jnp.zeros((n, LANES), F32).at[:, :TOP_K].set(w_t[:TOP_K].T)
    xb = _dispatch_call(dest_tiles, h2, _padfill_call(pad_ends, padded, cap, d), tm=tm_io)
    y = _expert_call(block_e, n_used, xb, w1[0], b1[0][:, None, :], w2[0], b2[0][:, None, :])
    return _combine_call(dest_tiles, x1, w_l, g2, final_g[None, :], y, tm=tm_io)
```
